```python
import jax, jax.numpy as jnp
from jax import lax
import numpy as np

D_MODEL = 1024
BATCH = 2
SEQ = 8192
DEPTH = 4
DEC_BATCH = 32
DEC_SEQ = 1
PAST_LEN = 8192
PAGE_SIZE = 128

N_A_LAYERS = DEPTH // 2
N_B_LAYERS = DEPTH - N_A_LAYERS
POOL_WINDOWS = (2, 4, 8, 16)
N_POOL_GROUPS = len(POOL_WINDOWS)
POOL_GROUP = D_MODEL // N_POOL_GROUPS
POOL_BUF = max(POOL_WINDOWS) - 1
HEAD_DIM = 64
N_HEADS = D_MODEL // HEAD_DIM
N_KV_HEADS = 4
GQ = N_HEADS // N_KV_HEADS
ROT_DIM = HEAD_DIM // 4
ROPE_THETA = 500000.0
SEL_BLOCK = 64
N_SEL = 16
WINDOW = 512
CMP_HIDDEN = 128
N_KV_SLOTS = 4
Q_BLOCK = 64
FORCE_SCORE = 1.0e4
D_FF = -(-(8 * D_MODEL) // (3 * 256)) * 256
EPS = 1e-6

kernel_name = 'yoco_pool_nsa_decoder_step'


def rmsnorm(x, g):
    xf = x.astype(jnp.float32)
    y = xf * lax.rsqrt(jnp.mean(xf * xf, axis=-1, keepdims=True) + EPS)
    return (y * g.astype(jnp.float32)).astype(x.dtype)


def rope_partial(x, pos):
    half = ROT_DIM // 2
    inv = jnp.power(jnp.float32(ROPE_THETA), -jnp.arange(half, dtype=jnp.float32) * 2.0 / ROT_DIM)
    ang = pos.astype(jnp.float32)[:, None] * inv[None, :]
    cos = jnp.cos(ang)[:, None, :].astype(x.dtype)
    sin = jnp.sin(ang)[:, None, :].astype(x.dtype)
    x1 = x[..., :half]
    x2 = x[..., half:ROT_DIM]
    return jnp.concatenate([x1 * cos - x2 * sin, x2 * cos + x1 * sin, x[..., ROT_DIM:]], axis=-1)


def adaln(x, c, w, b, g):
    mod = (c @ w + b)[:, None, :]
    shift, scale, gate = jnp.split(mod, 3, axis=-1)
    return rmsnorm(x, g) * (1 + scale) + shift, gate


def swiglu(h, w_gu, w_down):
    g, u = jnp.split(h @ w_gu, 2, axis=-1)
    return (jax.nn.silu(g) * u) @ w_down


def masked_softmax(s, mask):
    s = jnp.where(mask, s.astype(jnp.float32), -1e30)
    m = jnp.max(s, axis=-1, keepdims=True)
    p = jnp.where(mask, jnp.exp(s - m), 0.0)
    return p / jnp.maximum(jnp.sum(p, axis=-1, keepdims=True), 1e-30)


def pool_mix(h, prefix, pos0, w_grp, scale):
    B, T, D = h.shape
    full = jnp.concatenate([prefix, h], axis=1).astype(jnp.float32)
    cs = jnp.concatenate([jnp.zeros((B, 1, D), jnp.float32), jnp.cumsum(full, axis=1)], axis=1)
    pos = pos0 + jnp.arange(T)
    hi = cs[:, POOL_BUF + 1:POOL_BUF + 1 + T]
    means = []
    for g, w in enumerate(POOL_WINDOWS):
        ch = slice(g * POOL_GROUP, (g + 1) * POOL_GROUP)
        lo = cs[:, POOL_BUF + 1 - w:POOL_BUF + 1 - w + T, ch]
        cnt = jnp.minimum(w, pos + 1).astype(jnp.float32)[None, :, None]
        means.append((hi[:, :, ch] - lo) / cnt)
    pooled = (jnp.concatenate(means, axis=-1) - h.astype(jnp.float32)).reshape(B, T, N_POOL_GROUPS, POOL_GROUP)
    y = jnp.einsum('btgc,gcd->btgd', pooled, w_grp.astype(jnp.float32)).reshape(B, T, D)
    return (y * scale.astype(jnp.float32)).astype(h.dtype)


def compress_blocks(rows, pe, w1, w2):
    B, nb = rows.shape[:2]
    z = (rows + pe[:, None, :]).transpose(0, 1, 3, 2, 4).reshape(B, nb, N_KV_HEADS, SEL_BLOCK * HEAD_DIM)
    return jax.nn.gelu(z @ w1) @ w2


def shared_kv(x, c, pos, kv_prefix, win_prefix, win_keep, p):
    B, T, _ = x.shape
    mod = (c @ p['ada_kv_w'] + p['ada_kv_b'])[:, None, :]
    shift, scale = jnp.split(mod, 2, axis=-1)
    h = rmsnorm(x, p['norm_kv']) * (1 + scale) + shift
    proj = (h @ p['w_kv']).reshape(B, T, 6, N_KV_HEADS, HEAD_DIM)
    k_sel = rope_partial(rmsnorm(proj[:, :, 2], p['k_norm'][1]), pos)
    k_win = rope_partial(rmsnorm(proj[:, :, 4], p['k_norm'][2]), pos)
    rows_new = jnp.stack([proj[:, :, 0], proj[:, :, 1], k_sel, proj[:, :, 3]], axis=2)
    win_full = jnp.concatenate([win_prefix, jnp.stack([k_win, proj[:, :, 5]], axis=2)], axis=1)
    rows = jnp.concatenate([kv_prefix, rows_new], axis=1)
    L = rows.shape[1]
    nb = -(-L // SEL_BLOCK)
    rows = jnp.pad(rows, ((0, 0), (0, nb * SEL_BLOCK - L), (0, 0), (0, 0), (0, 0)))
    blocks = rows.reshape(B, nb, SEL_BLOCK, N_KV_SLOTS, N_KV_HEADS, HEAD_DIM)
    kc = compress_blocks(blocks[:, :, :, 0], p['cmp_pe'][0], p['cmp_w1'][0], p['cmp_w2'][0])
    kc = rope_partial(rmsnorm(kc, p['k_norm'][0]), (jnp.arange(nb) + 1) * SEL_BLOCK - 1)
    vc = compress_blocks(blocks[:, :, :, 1], p['cmp_pe'][1], p['cmp_w1'][1], p['cmp_w2'][1])
    ksb = blocks[:, :, :, 2].transpose(0, 3, 1, 2, 4)
    vsb = blocks[:, :, :, 3].transpose(0, 3, 1, 2, 4)
    return rows_new, win_full[:, -win_keep:], kc, vc, ksb, vsb, win_full[:, :, 0], win_full[:, :, 1]


def nsa_attention(q, gates, kc, vc, ksb, vsb, kw, vw, q_start):
    B, T = q.shape[:2]
    nb = kc.shape[1]
    n_sel = min(N_SEL, nb)
    qb = Q_BLOCK if T % Q_BLOCK == 0 else T
    sm = HEAD_DIM ** -0.5
    blk = jnp.arange(nb)
    blk_end = (blk + 1) * SEL_BLOCK - 1
    b_idx = jnp.arange(B)[:, None, None, None]
    h_idx = jnp.arange(N_KV_HEADS)[None, :, None, None]

    def one_block(i):
        t0 = i * qb
        qq = lax.dynamic_slice_in_dim(q, t0, qb, axis=1).reshape(B, qb, N_KV_HEADS, GQ, HEAD_DIM)
        gg = lax.dynamic_slice_in_dim(gates, t0, qb, axis=1).reshape(B, qb, N_KV_HEADS, GQ, 3)
        pos = q_start + t0 + jnp.arange(qb)
        s_c = jnp.einsum('bqkgd,bnkd->bkgqn', qq, kc) * sm
        p_c = masked_softmax(s_c, blk_end[None, :] <= pos[:, None])
        o_c = jnp.einsum('bkgqn,bnkd->bqkgd', p_c.astype(vc.dtype), vc)
        cur = pos // SEL_BLOCK
        forced = (blk[None, :] == 0) | (blk[None, :] == cur[:, None]) | (blk[None, :] == cur[:, None] - 1)
        imp = jnp.sum(p_c, axis=2)
        imp = jnp.where(forced, FORCE_SCORE, jnp.where(blk[None, :] > cur[:, None], -1.0, imp))
        _, idx = lax.top_k(imp, n_sel)
        k_g = ksb[b_idx, h_idx, idx]
        v_g = vsb[b_idx, h_idx, idx]
        s_s = jnp.einsum('bqkgd,bkqnsd->bkgqns', qq, k_g) * sm
        key_pos = idx[..., None] * SEL_BLOCK + jnp.arange(SEL_BLOCK)
        mask_s = (key_pos <= pos[:, None, None]).reshape(B, N_KV_HEADS, 1, qb, n_sel * SEL_BLOCK)
        p_s = masked_softmax(s_s.reshape(B, N_KV_HEADS, GQ, qb, n_sel * SEL_BLOCK), mask_s)
        o_s = jnp.einsum('bkgqm,bkqmd->bqkgd', p_s.astype(v_g.dtype),
                         v_g.reshape(B, N_KV_HEADS, qb, n_sel * SEL_BLOCK, HEAD_DIM))
        kk = lax.dynamic_slice_in_dim(kw, t0, WINDOW + qb, axis=1)
        vv = lax.dynamic_slice_in_dim(vw, t0, WINDOW + qb, axis=1)
        wpos = q_start - WINDOW + t0 + jnp.arange(WINDOW + qb)
        dist = pos[:, None] - wpos[None, :]
        mask_w = (wpos[None, :] >= 0) & (dist >= 0) & (dist < WINDOW)
        s_w = jnp.einsum('bqkgd,bmkd->bkgqm', qq, kk) * sm
        p_w = masked_softmax(s_w, mask_w)
        o_w = jnp.einsum('bkgqm,bmkd->bqkgd', p_w.astype(vv.dtype), vv)
        o = gg[..., 0:1] * o_c + gg[..., 1:2] * o_s + gg[..., 2:3] * o_w
        return o.astype(q.dtype).reshape(B, qb, N_HEADS * HEAD_DIM)

    out = lax.map(one_block, jnp.arange(T // qb))
    return out.transpose(1, 0, 2, 3).reshape(B, T, N_HEADS * HEAD_DIM)


def forward_group(x, c, q_start, pool_prefix, kv_prefix, win_prefix, win_keep, p):
    B, T, _ = x.shape
    pos = q_start + jnp.arange(T)
    hq = N_HEADS * HEAD_DIM
    new_pool = []
    for l in range(DEPTH):
        if l < N_A_LAYERS:
            h, gate = adaln(x, c, p['ada_w'][l, 0], p['ada_b'][l, 0], p['norm_mix'][l])
            x = x + gate * pool_mix(h, pool_prefix[l], q_start, p['pool_w'][l], p['pool_scale'][l])
            new_pool.append(jnp.concatenate([pool_prefix[l], h], axis=1)[:, -POOL_BUF:])
        else:
            if l == N_A_LAYERS:
                rows_new, win_state, kc, vc, ksb, vsb, kw, vw = shared_kv(x, c, pos, kv_prefix, win_prefix, win_keep, p)
            j = l - N_A_LAYERS
            h, gate = adaln(x, c, p['ada_w'][l, 0], p['ada_b'][l, 0], p['norm_mix'][l])
            qg = h @ p['w_qg'][j]
            q = rope_partial(rmsnorm(qg[..., :hq].reshape(B, T, N_HEADS, HEAD_DIM), p['q_norm'][j]), pos)
            gates = jax.nn.sigmoid(qg[..., hq:]).reshape(B, T, N_HEADS, 3)
            o = nsa_attention(q, gates, kc, vc, ksb, vsb, kw, vw, q_start)
            x = x + gate * (o @ p['w_o'][j])
        h, gate = adaln(x, c, p['ada_w'][l, 1], p['ada_b'][l, 1], p['norm_ffn'][l])
        x = x + gate * swiglu(h, p['w_gate_up'][l], p['w_down'][l])
    return x, rows_new, win_state, jnp.stack(new_pool)


def setup_inputs(seed: int = 0) -> dict:
    key = jax.random.key(seed)
    ks = jax.random.split(key, 32)
    f32 = jnp.float32
    D = D_MODEL
    n_pages = PAST_LEN // PAGE_SIZE
    n_used = DEC_BATCH * n_pages
    n_phys = n_used + max(1, n_used // 4)
    w_buf = min(WINDOW, PAST_LEN)
    hq = N_HEADS * HEAD_DIM

    def nrm(k, shape, s=1.0):
        return jax.random.normal(k, shape, f32) * s

    def gain(k, shape):
        return 1.0 + 0.05 * jax.random.normal(k, shape, f32)

    page_table = jax.random.permutation(ks[5], n_phys)[:n_used].reshape(DEC_BATCH, n_pages).astype(jnp.int32)
    return {
        'x_prompt': nrm(ks[0], (BATCH, SEQ, D)),
        'x_sample': nrm(ks[1], (DEC_BATCH, DEC_SEQ, D)),
        'cache_kv': nrm(ks[2], (n_phys, PAGE_SIZE, N_KV_SLOTS, N_KV_HEADS, HEAD_DIM)),
        'state_kv_win': nrm(ks[3], (DEC_BATCH, w_buf, 2, N_KV_HEADS, HEAD_DIM)),
        'state_pool': nrm(ks[4], (N_A_LAYERS, DEC_BATCH, POOL_BUF, D)),
        'page_table': page_table,
        'c_prompt': nrm(ks[6], (BATCH, D)),
        'c_sample': nrm(ks[7], (DEC_BATCH, D)),
        'ada_w': nrm(ks[8], (DEPTH, 2, D, 3 * D), 0.3 * D ** -0.5),
        'ada_b': nrm(ks[9], (DEPTH, 2, 3 * D), 0.02),
        'norm_mix': gain(ks[10], (DEPTH, D)),
        'norm_ffn': gain(ks[11], (DEPTH, D)),
        'pool_w': nrm(ks[12], (N_A_LAYERS, N_POOL_GROUPS, POOL_GROUP, POOL_GROUP), POOL_GROUP ** -0.5),
        'pool_scale': gain(ks[13], (N_A_LAYERS, D)),
        'ada_kv_w': nrm(ks[14], (D, 2 * D), 0.3 * D ** -0.5),
        'ada_kv_b': nrm(ks[15], (2 * D,), 0.02),
        'norm_kv': gain(ks[16], (D,)),
        'w_kv': nrm(ks[17], (D, 6 * N_KV_HEADS * HEAD_DIM), D ** -0.5),
        'k_norm': gain(ks[18], (3, HEAD_DIM)),
        'cmp_pe': nrm(ks[19], (2, SEL_BLOCK, HEAD_DIM), 0.1),
        'cmp_w1': nrm(ks[20], (2, SEL_BLOCK * HEAD_DIM, CMP_HIDDEN), (SEL_BLOCK * HEAD_DIM) ** -0.5),
        'cmp_w2': nrm(ks[21], (2, CMP_HIDDEN, HEAD_DIM), CMP_HIDDEN ** -0.5),
        'w_qg': nrm(ks[22], (N_B_LAYERS, D, hq + 3 * N_HEADS), D ** -0.5),
        'q_norm': gain(ks[23], (N_B_LAYERS, HEAD_DIM)),
        'w_o': nrm(ks[24], (N_B_LAYERS, hq, D), hq ** -0.5),
        'w_gate_up': nrm(ks[25], (DEPTH, D, 2 * D_FF), D ** -0.5),
        'w_down': nrm(ks[26], (DEPTH, D_FF, D), D_FF ** -0.5),
    }


def reference(x_prompt, x_sample, cache_kv, state_kv_win, state_pool, page_table, c_prompt, c_sample,
              ada_w, ada_b, norm_mix, norm_ffn, pool_w, pool_scale, ada_kv_w, ada_kv_b, norm_kv, w_kv,
              k_norm, cmp_pe, cmp_w1, cmp_w2, w_qg, q_norm, w_o, w_gate_up, w_down):
    p = dict(ada_w=ada_w, ada_b=ada_b, norm_mix=norm_mix, norm_ffn=norm_ffn, pool_w=pool_w,
             pool_scale=pool_scale, ada_kv_w=ada_kv_w, ada_kv_b=ada_kv_b, norm_kv=norm_kv, w_kv=w_kv,
             k_norm=k_norm, cmp_pe=cmp_pe, cmp_w1=cmp_w1, cmp_w2=cmp_w2, w_qg=w_qg, q_norm=q_norm,
             w_o=w_o, w_gate_up=w_gate_up, w_down=w_down)
    b_p, seq = x_prompt.shape[:2]
    b_s = x_sample.shape[0]
    dt = x_prompt.dtype
    y_prompt, kv_rows_prompt, win_prompt, pool_prompt = forward_group(
        x_prompt, c_prompt, 0,
        jnp.zeros((N_A_LAYERS, b_p, POOL_BUF, D_MODEL), dt),
        jnp.zeros((b_p, 0, N_KV_SLOTS, N_KV_HEADS, HEAD_DIM), dt),
        jnp.zeros((b_p, WINDOW, 2, N_KV_HEADS, HEAD_DIM), dt),
        min(WINDOW, seq), p)
    past_len = page_table.shape[1] * PAGE_SIZE
    kv_past = cache_kv[page_table].reshape(b_s, past_len, N_KV_SLOTS, N_KV_HEADS, HEAD_DIM)
    w_buf = state_kv_win.shape[1]
    win_prefix = jnp.pad(state_kv_win, ((0, 0), (WINDOW - w_buf, 0), (0, 0), (0, 0), (0, 0)))
    y_sample, kv_rows_sample, win_sample, pool_sample = forward_group(
        x_sample, c_sample, past_len, state_pool, kv_past, win_prefix, w_buf, p)
    return (y_prompt, y_sample, kv_rows_prompt, kv_rows_sample, win_prompt, win_sample, pool_prompt, pool_sample)
```

```python
import functools
import math

import jax
import jax.numpy as jnp
from jax import lax
from jax.experimental import pallas as pl
from jax.experimental.pallas import tpu as pltpu

F32 = jnp.float32
BF16 = jnp.bfloat16

POOL_WINDOWS = (2, 4, 8, 16)
POOL_HALO = 16
HEAD_DIM = 64
N_KV_HEADS = 4
ROT_DIM = 16
ROPE_THETA = 500000.0
SEL_BLOCK = 64
N_SEL = 16
WINDOW = 512
PAGE_SIZE = 128
FORCE_SCORE = 1.0e4
EPS = 1e-6
NEG = -float(2 ** 30)
LOG2E = 1.4426950408889634
LANES = 128
BLOCK_ROW_STRIDE = 72
VMEM_LIMIT = 56 * 1024 * 1024


def _cparams(sem):
    return pltpu.CompilerParams(dimension_semantics=sem, vmem_limit_bytes=VMEM_LIMIT)


def _dot(a, b):
    return jnp.dot(a, b, preferred_element_type=F32)


def _split(a):
    hi = a.astype(BF16)
    lo = (a - hi.astype(F32)).astype(BF16)
    return hi, lo


def _adaln(x, g, shift, scale):
    ms = jnp.mean(x * x, axis=-1, keepdims=True)
    return x * lax.rsqrt(ms + EPS) * g * (1.0 + scale) + shift


def _head_rms(x, gain, seg):
    hi, lo = _split(x * x)
    ss = _dot(hi, seg) + _dot(lo, seg)
    return x * lax.rsqrt(ss * (1.0 / HEAD_DIM) + EPS) * gain


def _rope(x, cos, sin_up, sin_dn):
    outs = []
    for a in range(x.shape[1] // LANES):
        xa = x[:, a * LANES:(a + 1) * LANES]
        up = pltpu.roll(xa, LANES - ROT_DIM // 2, 1)
        dn = pltpu.roll(xa, ROT_DIM // 2, 1)
        outs.append(xa * cos + up * sin_up + dn * sin_dn)
    return outs[0] if len(outs) == 1 else jnp.concatenate(outs, axis=1)


def _gelu_tanh(x):
    return x * (0.5 * (1.0 + jnp.tanh(math.sqrt(2.0 / math.pi) * (x + 0.044715 * (x * x * x)))))


def _mods_kernel(c_ref, w_ref, b_ref, o_ref):
    ch, cl = _split(c_ref[...])
    wh, wl = _split(w_ref[0])
    o_ref[0] = _dot(ch, wh) + _dot(ch, wl) + _dot(cl, wh) + b_ref[0]


def _mods(c_all, w, b, tn=1024):
    n_l, d, n = w.shape
    m = c_all.shape[0]
    return pl.pallas_call(
        _mods_kernel,
        out_shape=jax.ShapeDtypeStruct((n_l, m, n), F32),
        grid=(n_l, n // tn),
        in_specs=[pl.BlockSpec((m, d), lambda l, j: (0, 0)),
                  pl.BlockSpec((1, d, tn), lambda l, j: (l, 0, j)),
                  pl.BlockSpec((1, 1, tn), lambda l, j: (l, 0, j))],
        out_specs=pl.BlockSpec((1, m, tn), lambda l, j: (l, 0, j)),
        compiler_params=_cparams(("parallel", "parallel")),
        name="mods",
    )(c_all, w, b)


def _pool_kernel(x_ref, pre_ref, mod_ref, g_ref, pw_ref, ps_ref, o_ref, np_ref, hb_ref, *, tt, pos0, d):
    i = pl.program_id(1)
    x = x_ref[0]
    mod = mod_ref[0]
    h = _adaln(x, g_ref[...], mod[:, :d], mod[:, d:2 * d])

    @pl.when(i == 0)
    def _():
        hb_ref[0:POOL_HALO, :] = pre_ref[0]

    hb_ref[POOL_HALO:POOL_HALO + tt, :] = h
    pos = pos0 + i * tt + lax.broadcasted_iota(jnp.int32, (tt, 1), 0)
    pg = d // len(POOL_WINDOWS)
    ys = []
    for gi, w in enumerate(POOL_WINDOWS):
        c0 = gi * pg
        hg = h[:, c0:c0 + pg]
        s = hg
        for j in range(1, w):
            s = s + hb_ref[POOL_HALO - j:POOL_HALO - j + tt, c0:c0 + pg]
        cnt = jnp.minimum(w, pos + 1).astype(F32)
        pooled = s / cnt - hg
        ys.append(_dot(pooled.astype(BF16), pw_ref[gi]))
    y = jnp.concatenate(ys, axis=1) * ps_ref[...]
    o_ref[0] = x + mod[:, 2 * d:] * y
    last = hb_ref[tt:tt + POOL_HALO, :]
    np_ref[0] = last
    hb_ref[0:POOL_HALO, :] = last


def _pool_layer(x, prefix16, mod, g, pw, ps, pos0, tt):
    b, t, d = x.shape
    kern = functools.partial(_pool_kernel, tt=tt, pos0=pos0, d=d)
    pg = d // len(POOL_WINDOWS)
    return pl.pallas_call(
        kern,
        out_shape=(jax.ShapeDtypeStruct((b, t, d), F32), jax.ShapeDtypeStruct((b, POOL_HALO, d), F32)),
        grid=(b, t // tt),
        in_specs=[pl.BlockSpec((1, tt, d), lambda bi, i: (bi, i, 0)),
                  pl.BlockSpec((1, POOL_HALO, d), lambda bi, i: (bi, 0, 0)),
                  pl.BlockSpec((1, 1, 3 * d), lambda bi, i: (bi, 0, 0)),
                  pl.BlockSpec((1, d), lambda bi, i: (0, 0)),
                  pl.BlockSpec((len(POOL_WINDOWS), pg, pg), lambda bi, i: (0, 0, 0)),
                  pl.BlockSpec((1, d), lambda bi, i: (0, 0))],
        out_specs=(pl.BlockSpec((1, tt, d), lambda bi, i: (bi, i, 0)),
                   pl.BlockSpec((1, POOL_HALO, d), lambda bi, i: (bi, 0, 0))),
        scratch_shapes=[pltpu.VMEM((POOL_HALO + max(tt, 8), d), F32)],
        compiler_params=_cparams(("parallel", "arbitrary")),
        name="pool_layer",
    )(x, prefix16, mod, g, pw, ps)


def _ffn_kernel(x_ref, mod_ref, g_ref, wg_ref, wu_ref, wd_ref, o_ref, h_ref, acc_ref, *, d):
    f = pl.program_id(2)

    @pl.when(f == 0)
    def _():
        mod = mod_ref[0]
        h_ref[...] = _adaln(x_ref[0], g_ref[...], mod[:, :d], mod[:, d:2 * d]).astype(BF16)
        acc_ref[...] = jnp.zeros_like(acc_ref)

    hb = h_ref[...]
    g = _dot(hb, wg_ref[...])
    u = _dot(hb, wu_ref[...])
    a = (g * (1.0 / (1.0 + jnp.exp(-g)))) * u
    acc_ref[...] += _dot(a.astype(BF16), wd_ref[...])

    @pl.when(f == pl.num_programs(2) - 1)
    def _():
        o_ref[0] = x_ref[0] + mod_ref[0][:, 2 * d:] * acc_ref[...]


def _ffn(x, mod, g, w_gu, w_dn, tm, tf):
    b, t, d = x.shape
    rm = mod.shape[1]
    ff = w_dn.shape[0]
    nf = ff // tf
    kern = functools.partial(_ffn_kernel, d=d)
    return pl.pallas_call(
        kern,
        out_shape=jax.ShapeDtypeStruct((b, t, d), F32),
        grid=(b, t // tm, nf),
        in_specs=[pl.BlockSpec((1, tm, d), lambda bi, i, f: (bi, i, 0)),
                  pl.BlockSpec((1, rm, 3 * d), lambda bi, i, f: (bi, 0, 0)),
                  pl.BlockSpec((1, d), lambda bi, i, f: (0, 0)),
                  pl.BlockSpec((d, tf), lambda bi, i, f: (0, f)),
                  pl.BlockSpec((d, tf), lambda bi, i, f: (0, nf + f)),
                  pl.BlockSpec((tf, d), lambda bi, i, f: (f, 0))],
        out_specs=pl.BlockSpec((1, tm, d), lambda bi, i, f: (bi, i, 0)),
        scratch_shapes=[pltpu.VMEM((tm, d), BF16), pltpu.VMEM((tm, d), F32)],
        compiler_params=_cparams(("parallel", "parallel", "arbitrary")),
        name="ffn",
    )(x, mod, g, w_gu, w_gu, w_dn)


def _kv_kernel(x_ref, mod_ref, g_ref, w_ref, kn_ref, seg_ref, cos_ref, su_ref, sd_ref, place_ref, one_ref,
               rows_ref, win_ref, *attn_refs, d, emit_attn):
    hq = N_KV_HEADS * HEAD_DIM
    mod = mod_ref[0]
    h = _adaln(x_ref[0], g_ref[...], mod[:, :d], mod[:, d:2 * d])
    proj = _dot(h.astype(BF16), w_ref[...])
    seg = seg_ref[...]
    cos, su, sd = cos_ref[...], su_ref[...], sd_ref[...]
    k_sel = _rope(_head_rms(proj[:, 2 * hq:3 * hq], kn_ref[0], seg), cos, su, sd)
    k_win = _rope(_head_rms(proj[:, 4 * hq:5 * hq], kn_ref[1], seg), cos, su, sd)
    v_sel = proj[:, 3 * hq:4 * hq]
    v_win = proj[:, 5 * hq:6 * hq]
    rows_ref[0] = jnp.concatenate([proj[:, :2 * hq], k_sel, v_sel], axis=1)
    win_ref[0] = jnp.concatenate([k_win, v_win], axis=1)
    if emit_attn:
        kst_ref, vs_ref, kwt_ref, vw_ref = attn_refs
        tm = k_sel.shape[0]
        zeros = jnp.zeros((HEAD_DIM, tm), BF16)
        for kt_ref, kk in ((kst_ref, k_sel), (kwt_ref, k_win)):
            kt = kk.T.astype(BF16)
            for k in range(N_KV_HEADS):
                kt_ref[0, k, 0:HEAD_DIM, :] = kt[k * HEAD_DIM:(k + 1) * HEAD_DIM, :]
                kt_ref[0, k, HEAD_DIM:2 * HEAD_DIM, :] = zeros
        for va_ref, vv in ((vs_ref, v_sel), (vw_ref, v_win)):
            va = (_dot(vv.astype(BF16), place_ref[...]) + one_ref[...]).astype(BF16)
            for k in range(N_KV_HEADS):
                va_ref[0, k] = va[:, k * LANES:(k + 1) * LANES]


def _kv_proj(x, mod, g, w_kv, kn, seg, tabs, place, one_row, tm, emit_attn):
    b, t, d = x.shape
    rm = mod.shape[1]
    hq = N_KV_HEADS * HEAD_DIM
    kern = functools.partial(_kv_kernel, d=d, emit_attn=emit_attn)
    const2 = lambda bi, i: (0, 0)
    tab_spec = pl.BlockSpec((tm, LANES), lambda bi, i: (i, 0))
    out_shape = [jax.ShapeDtypeStruct((b, t, 4 * hq), F32), jax.ShapeDtypeStruct((b, t, 2 * hq), F32)]
    out_specs = [pl.BlockSpec((1, tm, 4 * hq), lambda bi, i: (bi, i, 0)),
                 pl.BlockSpec((1, tm, 2 * hq), lambda bi, i: (bi, i, 0))]
    if emit_attn:
        kt_shape = jax.ShapeDtypeStruct((b, N_KV_HEADS, 2 * HEAD_DIM, t), BF16)
        va_shape = jax.ShapeDtypeStruct((b, N_KV_HEADS, t, LANES), BF16)
        kt_spec = pl.BlockSpec((1, N_KV_HEADS, 2 * HEAD_DIM, tm), lambda bi, i: (bi, 0, 0, i))
        va_spec = pl.BlockSpec((1, N_KV_HEADS, tm, LANES), lambda bi, i: (bi, 0, i, 0))
        out_shape += [kt_shape, va_shape, kt_shape, va_shape]
        out_specs += [kt_spec, va_spec, kt_spec, va_spec]
    return pl.pallas_call(
        kern,
        out_shape=tuple(out_shape),
        grid=(b, t // tm),
        in_specs=[pl.BlockSpec((1, tm, d), lambda bi, i: (bi, i, 0)),
                  pl.BlockSpec((1, rm, 2 * d), lambda bi, i: (bi, 0, 0)),
                  pl.BlockSpec((1, d), const2),
                  pl.BlockSpec((d, 6 * hq), const2),
                  pl.BlockSpec((2, 1, hq), lambda bi, i: (0, 0, 0)),
                  pl.BlockSpec((hq, hq), const2),
                  tab_spec, tab_spec, tab_spec,
                  pl.BlockSpec((hq, N_KV_HEADS * LANES), const2),
                  pl.BlockSpec((1, N_KV_HEADS * LANES), const2)],
        out_specs=tuple(out_specs),
        compiler_params=_cparams(("parallel", "parallel")),
        name="kv_proj",
    )(x, mod, g, w_kv, kn, seg, *tabs, place, one_row)


def _compress_kernel(pt_ref, cache_ref, pe_ref, w1_ref, w2_ref, o_ref, buf_ref, sem_ref, *, n_pages, n_b):
    hq = N_KV_HEADS * HEAD_DIM
    nb = 2 * n_pages
    sl = pl.program_id(0)
    b = pl.program_id(1)
    step = sl * n_b + b
    cur = step % 2

    def block_copy(sl_, b_, slot_, n, pair):
        page = pt_ref[b_, n // 2]
        col = pl.multiple_of(sl_ * hq + pair * LANES, LANES)
        return pltpu.make_async_copy(
            cache_ref.at[page, pl.ds((n % 2) * SEL_BLOCK, SEL_BLOCK), pl.ds(col, LANES)],
            buf_ref.at[slot_, pair, pl.ds(n * BLOCK_ROW_STRIDE, SEL_BLOCK), :],
            sem_ref.at[slot_])

    def issue(sl_, b_, slot_):
        for n in range(nb):
            for pair in range(2):
                block_copy(sl_, b_, slot_, n, pair).start()

    @pl.when(step == 0)
    def _():
        issue(sl, b, 0)

    @pl.when(step + 1 < 2 * n_b)
    def _():
        nxt = step + 1
        issue(nxt // n_b, nxt % n_b, 1 - cur)

    for n in range(nb):
        for pair in range(2):
            block_copy(sl, b, cur, n, pair).wait()

    pe = pe_ref[0]
    hid = [jnp.zeros((nb, hq), F32), jnp.zeros((nb, hq), F32)]
    for sp in range(SEL_BLOCK // 2):
        w = w1_ref[0, sp]
        for pair in range(2):
            xs = [buf_ref[cur, pair, pl.ds(s, nb, stride=BLOCK_ROW_STRIDE), :]
                  + pe[s:s + 1, pair * LANES:(pair + 1) * LANES] for s in (2 * sp, 2 * sp + 1)]
            hid[pair] = hid[pair] + _dot(jnp.concatenate(xs, axis=1).astype(BF16), w)
    act = _gelu_tanh(jnp.concatenate(hid, axis=1))
    o_ref[0, 0] = _dot(act.astype(BF16), w2_ref[0])


def _compress(page_table, cache3, pe_t, w1cat, w2bd):
    n_b, n_pages = page_table.shape
    hq = N_KV_HEADS * HEAD_DIM
    nb = 2 * n_pages
    kern = functools.partial(_compress_kernel, n_pages=n_pages, n_b=n_b)
    grid_spec = pltpu.PrefetchScalarGridSpec(
        num_scalar_prefetch=1,
        grid=(2, n_b),
        in_specs=[pl.BlockSpec(memory_space=pl.ANY),
                  pl.BlockSpec((1, SEL_BLOCK, hq), lambda s, b, pt: (s, 0, 0)),
                  pl.BlockSpec((1, SEL_BLOCK // 2, hq, hq), lambda s, b, pt: (s, 0, 0, 0)),
                  pl.BlockSpec((1, 2 * hq, hq), lambda s, b, pt: (s, 0, 0))],
        out_specs=pl.BlockSpec((1, 1, nb, hq), lambda s, b, pt: (s, b, 0, 0)),
        scratch_shapes=[pltpu.VMEM((2, 2, nb * BLOCK_ROW_STRIDE, LANES), F32),
                        pltpu.SemaphoreType.DMA((2,))])
    return pl.pallas_call(
        kern,
        out_shape=jax.ShapeDtypeStruct((2, n_b, nb, hq), F32),
        grid_spec=grid_spec,
        compiler_params=_cparams(("arbitrary", "arbitrary")),
        name="compress",
    )(page_table, cache3, pe_t, w1cat, w2bd)


def _tail_kernel(z_ref, pe_ref, w1_ref, w2_ref, o_ref):
    z = z_ref[...] + pe_ref[0]
    hid = _gelu_tanh(_dot(z.astype(BF16), w1_ref[0]))
    o_ref[0] = _dot(hid.astype(BF16), w2_ref[0])


def _compress_tail(z, pe_flat, w1, w2):
    m, kdim = z.shape[0] // 2, z.shape[1]
    hid = w1.shape[2]
    return pl.pallas_call(
        _tail_kernel,
        out_shape=jax.ShapeDtypeStruct((2, m, HEAD_DIM), F32),
        grid=(2,),
        in_specs=[pl.BlockSpec((m, kdim), lambda s: (s, 0)),
                  pl.BlockSpec((1, 1, kdim), lambda s: (s, 0, 0)),
                  pl.BlockSpec((1, kdim, hid), lambda s: (s, 0, 0)),
                  pl.BlockSpec((1, hid, HEAD_DIM), lambda s: (s, 0, 0))],
        out_specs=pl.BlockSpec((1, m, HEAD_DIM), lambda s: (s, 0, 0)),
        compiler_params=_cparams(("parallel",)),
        name="compress_tail",
    )(z, pe_flat, w1, w2)


def _cmp_finish_kernel(kc_ref, vc_ref, kn_ref, seg_ref, cos_ref, su_ref, sd_ref, place_ref, kt_ref, vp_ref):
    kc = _rope(_head_rms(kc_ref[0], kn_ref[...], seg_ref[...]), cos_ref[...], su_ref[...], sd_ref[...])
    kt = kc.T.astype(BF16)
    nbp = kc.shape[0]
    zeros = jnp.zeros((HEAD_DIM, nbp), BF16)
    vp = _dot(vc_ref[0].astype(BF16), place_ref[...]).astype(BF16)
    for k in range(N_KV_HEADS):
        kt_ref[0, k, 0:HEAD_DIM, :] = kt[k * HEAD_DIM:(k + 1) * HEAD_DIM, :]
        kt_ref[0, k, HEAD_DIM:2 * HEAD_DIM, :] = zeros
        vp_ref[0, k] = vp[:, k * LANES:(k + 1) * LANES]


def _cmp_finish(kc_raw, vc_raw, kn0, seg, tabs, place):
    b, nbp, hq = kc_raw.shape
    const2 = lambda bi: (0, 0)
    tab_spec = pl.BlockSpec((nbp, LANES), const2)
    return pl.pallas_call(
        _cmp_finish_kernel,
        out_shape=(jax.ShapeDtypeStruct((b, N_KV_HEADS, 2 * HEAD_DIM, nbp), BF16),
                   jax.ShapeDtypeStruct((b, N_KV_HEADS, nbp, LANES), BF16)),
        grid=(b,),
        in_specs=[pl.BlockSpec((1, nbp, hq), lambda bi: (bi, 0, 0)),
                  pl.BlockSpec((1, nbp, hq), lambda bi: (bi, 0, 0)),
                  pl.BlockSpec((1, hq), const2),
                  pl.BlockSpec((hq, hq), const2),
                  tab_spec, tab_spec, tab_spec,
                  pl.BlockSpec((hq, N_KV_HEADS * LANES), const2)],
        out_specs=(pl.BlockSpec((1, N_KV_HEADS, 2 * HEAD_DIM, nbp), lambda bi: (bi, 0, 0, 0)),
                   pl.BlockSpec((1, N_KV_HEADS, nbp, LANES), lambda bi: (bi, 0, 0, 0))),
        compiler_params=_cparams(("parallel",)),
        name="cmp_finish",
    )(kc_raw, vc_raw, kn0, seg, *tabs, place)


def _q_kernel(x_ref, mod_ref, g_ref, wq_ref, wg_ref, qn_ref, seg_ref, cos_ref, su_ref, sd_ref,
              q_ref, gate_ref, *, d, q_scale):
    hq = N_KV_HEADS * HEAD_DIM
    mod = mod_ref[0]
    hb = _adaln(x_ref[0], g_ref[...], mod[:, :d], mod[:, d:2 * d]).astype(BF16)
    q = _dot(hb, wq_ref[...])
    seg = seg_ref[...]
    cos, su, sd = cos_ref[...], su_ref[...], sd_ref[...]
    for c in range(d // hq):
        qc = _rope(_head_rms(q[:, c * hq:(c + 1) * hq], qn_ref[...], seg), cos, su, sd) * q_scale
        q_ref[0, :, c * hq:(c + 1) * hq] = qc.astype(q_ref.dtype)
    gl = _dot(hb, wg_ref[...])
    gate_ref[0] = 1.0 / (1.0 + jnp.exp(-gl))


def _q_proj(x, mod, g, w_q, w_g, qn, seg, tabs, tm, q_scale, q_dtype):
    b, t, d = x.shape
    rm = mod.shape[1]
    hq = N_KV_HEADS * HEAD_DIM
    ng = w_g.shape[1]
    kern = functools.partial(_q_kernel, d=d, q_scale=q_scale)
    const2 = lambda bi, i: (0, 0)
    tab_spec = pl.BlockSpec((tm, LANES), lambda bi, i: (i, 0))
    return pl.pallas_call(
        kern,
        out_shape=(jax.ShapeDtypeStruct((b, t, d), q_dtype), jax.ShapeDtypeStruct((b, t, ng), F32)),
        grid=(b, t // tm),
        in_specs=[pl.BlockSpec((1, tm, d), lambda bi, i: (bi, i, 0)),
                  pl.BlockSpec((1, rm, 3 * d), lambda bi, i: (bi, 0, 0)),
                  pl.BlockSpec((1, d), const2),
                  pl.BlockSpec((d, d), const2),
                  pl.BlockSpec((d, ng), const2),
                  pl.BlockSpec((1, hq), const2),
                  pl.BlockSpec((hq, hq), const2),
                  tab_spec, tab_spec, tab_spec],
        out_specs=(pl.BlockSpec((1, tm, d), lambda bi, i: (bi, i, 0)),
                   pl.BlockSpec((1, tm, ng), lambda bi, i: (bi, i, 0))),
        compiler_params=_cparams(("parallel", "parallel")),
        name="q_proj",
    )(x, mod, g, w_q, w_g, qn, seg, *tabs)


def _oproj_kernel(x_ref, o_ref, mod_ref, w_ref, y_ref, *, d):
    y_ref[0] = x_ref[0] + mod_ref[0][:, 2 * d:] * _dot(o_ref[0].astype(BF16), w_ref[...])


def _o_proj(x, o, mod, w_o, tm):
    b, t, d = x.shape
    rm = mod.shape[1]
    kern = functools.partial(_oproj_kernel, d=d)
    return pl.pallas_call(
        kern,
        out_shape=jax.ShapeDtypeStruct((b, t, d), F32),
        grid=(b, t // tm),
        in_specs=[pl.BlockSpec((1, tm, d), lambda bi, i: (bi, i, 0)),
                  pl.BlockSpec((1, tm, d), lambda bi, i: (bi, i, 0)),
                  pl.BlockSpec((1, rm, 3 * d), lambda bi, i: (bi, 0, 0)),
                  pl.BlockSpec((d, d), lambda bi, i: (0, 0))],
        out_specs=pl.BlockSpec((1, tm, d), lambda bi, i: (bi, i, 0)),
        compiler_params=_cparams(("parallel", "parallel")),
        name="o_proj",
    )(x, o, mod, w_o)


def _attn_kernel(q_ref, gate_ref, kct_ref, vcp_ref, kst_ref, vs_ref, kwt_ref, vw_ref, e_ref, pin_ref, pout_ref,
                 o_ref, qa_ref, m_ref, acc_ref, *, tq, ck, nb):
    gq = q_ref.shape[2] // HEAD_DIM
    rows = gq * tq
    i = pl.program_id(2)
    t0 = i * tq

    qslab = q_ref[0]
    qp = jnp.concatenate([_dot(qslab, pin_ref[g]).astype(BF16) for g in range(gq)], axis=0)
    pos_r = t0 + lax.broadcasted_iota(jnp.int32, (rows, 1), 0) % tq

    s_c = _dot(qp, kct_ref[0, 0])
    blk = lax.broadcasted_iota(jnp.int32, (rows, nb), 1)
    ok_c = (blk + 1) * SEL_BLOCK - 1 <= pos_r
    m_c = jnp.max(jnp.where(ok_c, s_c, -jnp.inf), axis=1, keepdims=True)
    m_c = jnp.where(m_c == -jnp.inf, 0.0, m_c)
    p_c = jnp.where(ok_c, jnp.exp2(s_c - m_c), 0.0)
    p_c = p_c / jnp.maximum(jnp.sum(p_c, axis=1, keepdims=True), 1e-30)
    o_c = _dot(p_c.astype(BF16), vcp_ref[0, 0])

    imp = p_c[0:tq]
    for g in range(1, gq):
        imp = imp + p_c[g * tq:(g + 1) * tq]
    imp_t = imp.T
    n_io = lax.broadcasted_iota(jnp.int32, (nb, tq), 0)
    cur = (t0 + lax.broadcasted_iota(jnp.int32, (nb, tq), 1)) // SEL_BLOCK
    forced = (n_io == 0) | (n_io == cur) | (n_io == cur - 1)
    v = jnp.where(forced, FORCE_SCORE, jnp.where(n_io > cur, -1.0, imp_t))
    sel = jnp.zeros((nb, tq), F32)
    for _ in range(N_SEL):
        mx = jnp.max(v, axis=0, keepdims=True)
        first = jnp.min(jnp.where(v == mx, n_io, nb), axis=0, keepdims=True)
        pick = n_io == first
        sel = jnp.where(pick, 1.0, sel)
        v = jnp.where(pick, -2.0, v)
    bias_t = jnp.where((sel > 0.0) & (n_io <= cur), 0.0, NEG)
    bias = bias_t.T.astype(BF16)
    qa_ref[...] = jnp.concatenate([jnp.concatenate([bias] * gq, axis=0), qp], axis=1)

    m_ref[...] = jnp.full(m_ref.shape, NEG * 4.0, F32)
    acc_ref[...] = jnp.zeros_like(acc_ref)

    def chunk(c, causal):
        k0 = pl.multiple_of(c * ck, ck)
        kaug = jnp.concatenate([e_ref[:, pl.ds(k0, ck)], kst_ref[0, 0, :, pl.ds(k0, ck)]], axis=0)
        s = _dot(qa_ref[...], kaug)
        if causal:
            kpos = k0 + lax.broadcasted_iota(jnp.int32, (rows, ck), 1)
            s = jnp.where(kpos <= pos_r, s, NEG)
        m_prev = m_ref[...]
        m_new = jnp.maximum(m_prev, jnp.max(s, axis=1, keepdims=True))
        p = jnp.exp2(s - m_new[:, 0:1])
        acc_ref[...] = acc_ref[...] * jnp.exp2(m_prev - m_new) + _dot(p.astype(BF16), vs_ref[0, 0, pl.ds(k0, ck), :])
        m_ref[...] = m_new

    c_last = t0 // ck

    def body(c, carry):
        chunk(c, False)
        return carry

    lax.fori_loop(0, c_last, body, 0)
    chunk(c_last, True)
    acc = acc_ref[...]
    o_s = acc / acc[:, HEAD_DIM:HEAD_DIM + 1]

    wl = WINDOW + tq
    w0 = pl.multiple_of(jnp.maximum(t0 - WINDOW, 0), LANES)
    s_w = _dot(qp, kwt_ref[0, 0, :, pl.ds(w0, wl)])
    kpos = w0 + lax.broadcasted_iota(jnp.int32, (rows, wl), 1)
    ok_w = (kpos <= pos_r) & (pos_r - kpos < WINDOW)
    m_w = jnp.max(jnp.where(ok_w, s_w, -jnp.inf), axis=1, keepdims=True)
    p_w = jnp.where(ok_w, jnp.exp2(s_w - m_w), 0.0)
    acc_w = _dot(p_w.astype(BF16), vw_ref[0, 0, pl.ds(w0, wl), :])
    o_w = acc_w / acc_w[:, HEAD_DIM:HEAD_DIM + 1]

    gates = gate_ref[0, 0]
    out = jnp.zeros((tq, gq * HEAD_DIM), F32)
    for g in range(gq):
        r = slice(g * tq, (g + 1) * tq)
        mix = (gates[:, 3 * g:3 * g + 1] * o_c[r] + gates[:, 3 * g + 1:3 * g + 2] * o_s[r]
               + gates[:, 3 * g + 2:3 * g + 3] * o_w[r])
        out = out + _dot(mix.astype(BF16), pout_ref[g])
    o_ref[0] = out.astype(o_ref.dtype)


def _attention(q, gates, kct, vcp, kst, vs, kwt, vw, e_mat, pin, pout, tq, ck):
    b, t, d = q.shape
    nb = kct.shape[3]
    gq = d // (N_KV_HEADS * HEAD_DIM)
    sw = gq * HEAD_DIM
    rows = gq * tq
    kern = functools.partial(_attn_kernel, tq=tq, ck=ck, nb=nb)
    per_head4 = lambda bi, k, i: (bi, k, 0, 0)
    return pl.pallas_call(
        kern,
        out_shape=jax.ShapeDtypeStruct((b, t, d), BF16),
        grid=(b, N_KV_HEADS, t // tq),
        in_specs=[pl.BlockSpec((1, tq, sw), lambda bi, k, i: (bi, i, k)),
                  pl.BlockSpec((1, 1, tq, LANES), lambda bi, k, i: (bi, k, i, 0)),
                  pl.BlockSpec((1, 1, 2 * HEAD_DIM, nb), per_head4),
                  pl.BlockSpec((1, 1, nb, LANES), per_head4),
                  pl.BlockSpec((1, 1, 2 * HEAD_DIM, t), per_head4),
                  pl.BlockSpec((1, 1, t, LANES), per_head4),
                  pl.BlockSpec((1, 1, 2 * HEAD_DIM, t), per_head4),
                  pl.BlockSpec((1, 1, t, LANES), per_head4),
                  pl.BlockSpec((nb, t), lambda bi, k, i: (0, 0)),
                  pl.BlockSpec((gq, sw, LANES), lambda bi, k, i: (0, 0, 0)),
                  pl.BlockSpec((gq, LANES, sw), lambda bi, k, i: (0, 0, 0))],
        out_specs=pl.BlockSpec((1, tq, sw), lambda bi, k, i: (bi, i, k)),
        scratch_shapes=[pltpu.VMEM((rows, nb + LANES), BF16),
                        pltpu.VMEM((rows, LANES), F32),
                        pltpu.VMEM((rows, LANES), F32)],
        compiler_params=_cparams(("parallel", "parallel", "arbitrary")),
        name="nsa_attention",
    )(q, gates, kct, vcp, kst, vs, kwt, vw, e_mat, pin, pout)


def _row_kvh(shape, gq):
    return lax.broadcasted_iota(jnp.int32, shape, 0) // gq


def _dec_a_kernel(q_ref, kct_ref, vcp_ref, oc_ref, idx_ref, *, pos, nb, gq):
    nh = q_ref.shape[1]
    nbp = kct_ref.shape[3]
    qb = q_ref[0].astype(BF16)
    rk = _row_kvh((nh, nbp), gq)
    s = jnp.zeros((nh, nbp), F32)
    for k in range(N_KV_HEADS):
        s = jnp.where(rk == k, _dot(qb, kct_ref[0, k]), s)
    blk = lax.broadcasted_iota(jnp.int32, (nh, nbp), 1)
    ok = ((blk + 1) * SEL_BLOCK - 1 <= pos) & (blk < nb)
    m = jnp.max(jnp.where(ok, s, -jnp.inf), axis=1, keepdims=True)
    m = jnp.where(m == -jnp.inf, 0.0, m)
    p = jnp.where(ok, jnp.exp(s - m), 0.0)
    p = p / jnp.maximum(jnp.sum(p, axis=1, keepdims=True), 1e-30)
    pb = p.astype(BF16)
    rk_o = _row_kvh((nh, LANES), gq)
    o_c = jnp.zeros((nh, LANES), F32)
    for k in range(N_KV_HEADS):
        o_c = jnp.where(rk_o == k, _dot(pb, vcp_ref[0, k]), o_c)
    oc_ref[0] = o_c
    imp = jnp.concatenate([jnp.sum(jnp.where(rk == k, p, 0.0), axis=0, keepdims=True)
                           for k in range(N_KV_HEADS)], axis=0)
    n_io = lax.broadcasted_iota(jnp.int32, (N_KV_HEADS, nbp), 1)
    cur = pos // SEL_BLOCK
    forced = (n_io == 0) | (n_io == cur) | (n_io == cur - 1)
    v = jnp.where(forced, FORCE_SCORE, jnp.where(n_io > cur, -1.0, imp))
    v = jnp.where(n_io < nb, v, -3.0)
    col = lax.broadcasted_iota(jnp.int32, (N_KV_HEADS, N_SEL), 1)
    idx = jnp.zeros((N_KV_HEADS, N_SEL), jnp.int32)
    for r in range(N_SEL):
        mx = jnp.max(v, axis=1, keepdims=True)
        first = jnp.min(jnp.where(v == mx, n_io, nbp), axis=1, keepdims=True)
        idx = jnp.where(col == r, first, idx)
        v = jnp.where(n_io == first, -4.0, v)
    idx_ref[0] = idx


def _decode_a(q_pad, kct, vcp, pos, nb, gq):
    b, nh, _ = q_pad.shape
    nbp = kct.shape[3]
    kern = functools.partial(_dec_a_kernel, pos=pos, nb=nb, gq=gq)
    return pl.pallas_call(
        kern,
        out_shape=(jax.ShapeDtypeStruct((b, nh, LANES), F32),
                   jax.ShapeDtypeStruct((b, N_KV_HEADS, N_SEL), jnp.int32)),
        grid=(b,),
        in_specs=[pl.BlockSpec((1, nh, LANES), lambda bi: (bi, 0, 0)),
                  pl.BlockSpec((1, N_KV_HEADS, 2 * HEAD_DIM, nbp), lambda bi: (bi, 0, 0, 0)),
                  pl.BlockSpec((1, N_KV_HEADS, nbp, LANES), lambda bi: (bi, 0, 0, 0))],
        out_specs=(pl.BlockSpec((1, nh, LANES), lambda bi: (bi, 0, 0)),
                   pl.BlockSpec((1, N_KV_HEADS, N_SEL), lambda bi: (bi, 0, 0))),
        compiler_params=_cparams(("parallel",)),
        name="decode_cmp_topk",
    )(q_pad, kct, vcp)


def _dec_b_kernel(idx_ref, pt_ref, qsel_ref, qwin_ref, gate_ref, oc_ref, cache_ref, new_ref, swin_ref, wnew_ref,
                  o_ref, kbuf_ref, vbuf_ref, sem_ref, *, pos, nb_past, gq, n_b):
    hq = N_KV_HEADS * HEAD_DIM
    nh = qsel_ref.shape[1]
    b = pl.program_id(0)
    cur = b % 2
    nkeys = N_SEL * SEL_BLOCK

    def copies(b_, slot_, k, j):
        n = idx_ref[b_, k * N_SEL + j]
        n_c = jnp.minimum(n, nb_past - 1)
        page = pt_ref[b_, n_c // 2]
        r0 = pl.multiple_of((n_c % 2) * SEL_BLOCK, SEL_BLOCK)
        pair = (k // 2) * LANES
        ck = pltpu.make_async_copy(cache_ref.at[page, pl.ds(r0, SEL_BLOCK), pl.ds(2 * hq + pair, LANES)],
                                   kbuf_ref.at[slot_, k, j], sem_ref.at[slot_])
        cv = pltpu.make_async_copy(cache_ref.at[page, pl.ds(r0, SEL_BLOCK), pl.ds(3 * hq + pair, LANES)],
                                   vbuf_ref.at[slot_, k, j], sem_ref.at[slot_])
        return n, ck, cv

    def issue(b_, slot_):
        for k in range(N_KV_HEADS):
            for j in range(N_SEL):
                n, ck, cv = copies(b_, slot_, k, j)

                @pl.when(n < nb_past)
                def _():
                    ck.start()
                    cv.start()

    @pl.when(b == 0)
    def _():
        issue(b, 0)

    @pl.when(b + 1 < n_b)
    def _():
        issue(b + 1, 1 - cur)

    new = new_ref[0]
    lane64 = lax.broadcasted_iota(jnp.int32, (1, nkeys), 1)
    kps = []
    for k in range(N_KV_HEADS):
        pair = (k // 2) * LANES
        kp = lane64 % SEL_BLOCK
        for j in range(N_SEL):
            n, ck, cv = copies(b, cur, k, j)

            @pl.when(n < nb_past)
            def _():
                ck.wait()
                cv.wait()

            @pl.when(n >= nb_past)
            def _():
                row0 = lax.broadcasted_iota(jnp.int32, (SEL_BLOCK, LANES), 0) == 0
                kbuf_ref[cur, k, j] = jnp.where(row0, new[:, 2 * hq + pair:2 * hq + pair + LANES], 0.0)
                vbuf_ref[cur, k, j] = jnp.where(row0, new[:, 3 * hq + pair:3 * hq + pair + LANES], 0.0)

            kp = kp + jnp.where(lane64 // SEL_BLOCK == j, n * SEL_BLOCK, 0)
        kps.append(kp)

    qs = qsel_ref[0].astype(BF16)
    rk = _row_kvh((nh, nkeys), gq)
    s = jnp.zeros((nh, nkeys), F32)
    kpos = jnp.zeros((nh, nkeys), jnp.int32)
    for k in range(N_KV_HEADS):
        kk = kbuf_ref[cur, k].reshape(nkeys, LANES).astype(BF16)
        sk = lax.dot_general(qs, kk, (((1,), (1,)), ((), ())), preferred_element_type=F32)
        s = jnp.where(rk == k, sk, s)
        kpos = jnp.where(rk == k, kps[k], kpos)
    ok = kpos <= pos
    m = jnp.max(jnp.where(ok, s, -jnp.inf), axis=1, keepdims=True)
    p = jnp.where(ok, jnp.exp(s - m), 0.0)
    p = p / jnp.maximum(jnp.sum(p, axis=1, keepdims=True), 1e-30)
    pb = p.astype(BF16)
    rk_o = _row_kvh((nh, LANES), gq)
    o_s = jnp.zeros((nh, LANES), F32)
    for k in range(N_KV_HEADS):
        vk = vbuf_ref[cur, k].reshape(nkeys, LANES).astype(BF16)
        o_s = jnp.where(rk_o == k, _dot(pb, vk), o_s)
    o_s = jnp.where(rk_o % 2 == 1, pltpu.roll(o_s, HEAD_DIM, 1), o_s)

    sw = swin_ref[0]
    wn = wnew_ref[0]
    qw = qwin_ref[0]
    s_w = lax.dot_general(qw.astype(BF16), sw[:, :hq].astype(BF16), (((1,), (1,)), ((), ())),
                          preferred_element_type=F32)
    s_n = jnp.sum(qw * wn[:, :hq], axis=1, keepdims=True)
    w_buf = sw.shape[0]
    ridx = lax.broadcasted_iota(jnp.int32, (nh, w_buf), 1)
    ok_w = w_buf - ridx < WINDOW
    m_w = jnp.maximum(jnp.max(jnp.where(ok_w, s_w, -jnp.inf), axis=1, keepdims=True), s_n)
    p_w = jnp.where(ok_w, jnp.exp(s_w - m_w), 0.0)
    p_n = jnp.exp(s_n - m_w)
    l_w = jnp.sum(p_w, axis=1, keepdims=True) + p_n
    o_full = (_dot((p_w / l_w).astype(BF16), sw[:, hq:].astype(BF16)) + (p_n / l_w) * wn[:, hq:])
    o_w = jnp.where(rk_o >= 2, o_full[:, LANES:2 * LANES], o_full[:, 0:LANES])
    o_w = jnp.where(rk_o % 2 == 1, pltpu.roll(o_w, HEAD_DIM, 1), o_w)

    gates = gate_ref[0]
    o_ref[0] = gates[:, 0:1] * oc_ref[0] + gates[:, 1:2] * o_s + gates[:, 2:3] * o_w


def _decode_b(idx_flat, page_table, q_sel, q_win, gates, o_c, cache3, new_rows, state_win, win_new, pos, gq):
    b, nh, _ = q_sel.shape
    hq = N_KV_HEADS * HEAD_DIM
    w_buf = state_win.shape[1]
    nb_past = page_table.shape[1] * (PAGE_SIZE // SEL_BLOCK)
    kern = functools.partial(_dec_b_kernel, pos=pos, nb_past=nb_past, gq=gq, n_b=b)
    grid_spec = pltpu.PrefetchScalarGridSpec(
        num_scalar_prefetch=2,
        grid=(b,),
        in_specs=[pl.BlockSpec((1, nh, LANES), lambda bi, ix, pt: (bi, 0, 0)),
                  pl.BlockSpec((1, nh, hq), lambda bi, ix, pt: (bi, 0, 0)),
                  pl.BlockSpec((1, nh, LANES), lambda bi, ix, pt: (bi, 0, 0)),
                  pl.BlockSpec((1, nh, LANES), lambda bi, ix, pt: (bi, 0, 0)),
                  pl.BlockSpec(memory_space=pl.ANY),
                  pl.BlockSpec((1, 1, 4 * hq), lambda bi, ix, pt: (bi, 0, 0)),
                  pl.BlockSpec((1, w_buf, 2 * hq), lambda bi, ix, pt: (bi, 0, 0)),
                  pl.BlockSpec((1, 1, 2 * hq), lambda bi, ix, pt: (bi, 0, 0))],
        out_specs=pl.BlockSpec((1, nh, LANES), lambda bi, ix, pt: (bi, 0, 0)),
        scratch_shapes=[pltpu.VMEM((2, N_KV_HEADS, N_SEL, SEL_BLOCK, LANES), F32),
                        pltpu.VMEM((2, N_KV_HEADS, N_SEL, SEL_BLOCK, LANES), F32),
                        pltpu.SemaphoreType.DMA((2,))])
    return pl.pallas_call(
        kern,
        out_shape=jax.ShapeDtypeStruct((b, nh, LANES), F32),
        grid_spec=grid_spec,
        compiler_params=_cparams(("arbitrary",)),
        name="decode_sel_win",
    )(idx_flat, page_table, q_sel, q_win, gates, o_c, cache3, new_rows, state_win, win_new)


def _rope_tables(pos):
    half = ROT_DIM // 2
    inv = jnp.power(jnp.float32(ROPE_THETA), -jnp.arange(half, dtype=F32) * 2.0 / ROT_DIM)
    ang = pos.astype(F32)[:, None] * inv[None, :]
    cos, sin = jnp.cos(ang), jnp.sin(ang)
    r = jnp.arange(LANES) % HEAD_DIM
    f = r % half
    cos_t = jnp.where(r < ROT_DIM, cos[:, f], 1.0)
    sin_up = jnp.where(r < half, -sin[:, f], 0.0)
    sin_dn = jnp.where((r >= half) & (r < ROT_DIM), sin[:, f], 0.0)
    return cos_t, sin_up, sin_dn


def _tile_heads(v, n):
    return jnp.tile(v.astype(F32), n)[None, :]


def kernel(x_prompt, x_sample, cache_kv, state_kv_win, state_pool, page_table, c_prompt, c_sample, ada_w, ada_b, norm_mix, norm_ffn, pool_w, pool_scale, ada_kv_w, ada_kv_b, norm_kv, w_kv, k_norm, cmp_pe, cmp_w1, cmp_w2, w_qg, q_norm, w_o, w_gate_up, w_down):
    b_p, seq, d = x_prompt.shape
    b_s, dec_seq, _ = x_sample.shape
    depth = ada_w.shape[0]
    n_a = pool_w.shape[0]
    n_heads = d // HEAD_DIM
    gq = n_heads // N_KV_HEADS
    hq = N_KV_HEADS * HEAD_DIM
    n_pages = page_table.shape[1]
    past_len = n_pages * PAGE_SIZE
    w_buf = state_kv_win.shape[1]
    d_ff = w_down.shape[1]
    assert dec_seq == 1 and hq == 2 * LANES and seq % PAGE_SIZE == 0
    sm = HEAD_DIM ** -0.5

    w_gu_b = w_gate_up.astype(BF16)
    w_dn_b = w_down.astype(BF16)
    w_kv_b = w_kv.astype(BF16)
    pool_w_b = pool_w.astype(BF16)
    w_q_b = w_qg[:, :, :d].astype(BF16)
    wg_cols = w_qg[:, :, d:].reshape(-1, d, N_KV_HEADS, 3 * gq)
    w_g_b = jnp.pad(wg_cols, ((0, 0), (0, 0), (0, 0), (0, LANES - 3 * gq))).reshape(-1, d, N_KV_HEADS * LANES).astype(BF16)
    w_o_b = w_o.astype(BF16)

    head_of = jnp.arange(hq) // HEAD_DIM
    seg = (head_of[:, None] == head_of[None, :]).astype(BF16)
    lane_in = jnp.arange(hq)
    lane_out = jnp.arange(N_KV_HEADS * LANES)
    place = ((lane_out[None, :] // LANES == lane_in[:, None] // HEAD_DIM)
             & (lane_out[None, :] % LANES == lane_in[:, None] % HEAD_DIM)).astype(BF16)
    one_row = (lane_out % LANES == HEAD_DIM).astype(F32)[None, :]
    sw = gq * HEAD_DIM
    cin = jnp.arange(sw)
    pin = jnp.stack([((cin[:, None] // HEAD_DIM == g) & (jnp.arange(LANES)[None, :] == cin[:, None] % HEAD_DIM))
                     for g in range(gq)]).astype(BF16)
    pout = jnp.stack([((jnp.arange(LANES)[:, None] < HEAD_DIM)
                       & (cin[None, :] == g * HEAD_DIM + jnp.arange(LANES)[:, None]))
                      for g in range(gq)]).astype(BF16)

    kn = jnp.stack([_tile_heads(k_norm[1], N_KV_HEADS), _tile_heads(k_norm[2], N_KV_HEADS)])
    kn0 = _tile_heads(k_norm[0], N_KV_HEADS)
    pe_t = jnp.tile(cmp_pe, (1, 1, N_KV_HEADS))
    eye2 = jnp.eye(2, dtype=F32)
    w1s = cmp_w1.reshape(2, SEL_BLOCK, HEAD_DIM, -1)
    w1bd = jnp.einsum('ab,zsdj->zsadbj', eye2, w1s).reshape(2, SEL_BLOCK // 2, 2 * 2 * HEAD_DIM, 2 * w1s.shape[-1])
    w1cat = w1bd.astype(BF16)
    eye4 = jnp.eye(N_KV_HEADS, dtype=F32)
    w2bd = jnp.einsum('ab,zje->zajbe', eye4, cmp_w2).reshape(2, N_KV_HEADS * cmp_w2.shape[1], hq).astype(BF16)

    c_all = jnp.concatenate([c_prompt, c_sample], axis=0)
    m_all = c_all.shape[0]
    m_pad = -(-m_all // 8) * 8
    c_all = jnp.pad(c_all, ((0, m_pad - m_all), (0, 0)))
    mods = _mods(c_all, ada_w.reshape(depth * 2, d, 3 * d), ada_b.reshape(depth * 2, 1, 3 * d)).reshape(depth, 2, m_pad, 3 * d)
    mod_kv = _mods(c_all, ada_kv_w[None], ada_kv_b[None, None, :])[0]

    def mod_p(l, j):
        return mods[l, j, :b_p][:, None, :]

    def mod_s_tok(l, j):
        return mods[l, j, b_p:m_all][None]

    def mod_s_seq(l, j):
        return mods[l, j, b_p:m_all][:, None, :]

    tm = min(512, seq)
    tf = d_ff // 2 if (d_ff // 2) % LANES == 0 else d_ff
    tq = 128
    ck = 512

    x = x_prompt
    pool_p = []
    for l in range(n_a):
        x, npool = _pool_layer(x, jnp.zeros((b_p, POOL_HALO, d), F32), mod_p(l, 0), norm_mix[l][None], pool_w_b[l],
                               pool_scale[l][None], 0, tm)
        pool_p.append(npool[:, 1:])
        x = _ffn(x, mod_p(l, 1), norm_ffn[l][None], w_gu_b[l], w_dn_b[l], tm, tf)

    tabs_p = _rope_tables(jnp.arange(seq))
    rows_p, win_p, kst, vs, kwt, vw = _kv_proj(x, mod_kv[:b_p][:, None, :], norm_kv[None], w_kv_b, kn, seg, tabs_p,
                                               place, one_row, tm, True)
    nb_p = seq // SEL_BLOCK
    pt_p = jnp.arange(b_p * (seq // PAGE_SIZE), dtype=jnp.int32).reshape(b_p, seq // PAGE_SIZE)
    raw_p = _compress(pt_p, rows_p.reshape(-1, PAGE_SIZE, 4 * hq), pe_t, w1cat, w2bd)
    tabs_blk_p = _rope_tables((jnp.arange(nb_p) + 1) * SEL_BLOCK - 1)
    kct_p, vcp_p = _cmp_finish(raw_p[0], raw_p[1], kn0, seg, tabs_blk_p, place)
    e_mat = (jnp.arange(seq)[None, :] // SEL_BLOCK == jnp.arange(nb_p)[:, None]).astype(BF16)
    for l in range(n_a, depth):
        j = l - n_a
        q, gates = _q_proj(x, mod_p(l, 0), norm_mix[l][None], w_q_b[j], w_g_b[j], _tile_heads(q_norm[j], N_KV_HEADS),
                           seg, tabs_p, tm, sm * LOG2E, BF16)
        gates = gates.reshape(b_p, seq, N_KV_HEADS, LANES).transpose(0, 2, 1, 3)
        o = _attention(q, gates, kct_p, vcp_p, kst, vs, kwt, vw, e_mat, pin, pout, tq, ck)
        x = _o_proj(x, o, mod_p(l, 0), w_o_b[j], tm)
        x = _ffn(x, mod_p(l, 1), norm_ffn[l][None], w_gu_b[l], w_dn_b[l], tm, tf)
    y_prompt = x
    kv_rows_prompt = rows_p.reshape(b_p, seq, 4, N_KV_HEADS, HEAD_DIM)
    win_keep_p = min(WINDOW, seq)
    win_prompt = win_p[:, seq - win_keep_p:].reshape(b_p, win_keep_p, 2, N_KV_HEADS, HEAD_DIM)
    pool_prompt = jnp.stack(pool_p)

    pos_s = past_len
    xs = x_sample
    pool_s = []
    for l in range(n_a):
        pre = jnp.pad(state_pool[l], ((0, 0), (POOL_HALO - state_pool.shape[2], 0), (0, 0)))
        xs, npool = _pool_layer(xs, pre, mod_s_seq(l, 0), norm_mix[l][None], pool_w_b[l], pool_scale[l][None], pos_s, 1)
        pool_s.append(npool[:, 1:])
        xs = _ffn(xs.reshape(1, b_s, d), mod_s_tok(l, 1), norm_ffn[l][None], w_gu_b[l], w_dn_b[l], b_s, tf).reshape(b_s, 1, d)
    xt = xs.reshape(1, b_s, d)
    tabs_s = _rope_tables(jnp.full((b_s,), pos_s))
    rows_s, win_s = _kv_proj(xt, mod_kv[b_p:m_all][None], norm_kv[None], w_kv_b, kn, seg, tabs_s, place, one_row,
                             b_s, False)
    rows_s = rows_s.reshape(b_s, 1, 4 * hq)
    win_new = win_s.reshape(b_s, 1, 2 * hq)

    cache3 = cache_kv.reshape(-1, PAGE_SIZE, 4 * hq)
    raw_s = _compress(page_table, cache3, pe_t, w1cat, w2bd)
    nb_past = past_len // SEL_BLOCK
    nb_s = -(-(past_len + 1) // SEL_BLOCK)
    z_tail = jnp.pad(rows_s[:, 0, :2 * hq].reshape(b_s, 2, N_KV_HEADS, HEAD_DIM).transpose(1, 0, 2, 3)
                     .reshape(2 * b_s * N_KV_HEADS, HEAD_DIM), ((0, 0), (0, (SEL_BLOCK - 1) * HEAD_DIM)))
    raw_tail = _compress_tail(z_tail, cmp_pe.reshape(2, 1, SEL_BLOCK * HEAD_DIM), cmp_w1.astype(BF16),
                              cmp_w2.astype(BF16)).reshape(2, b_s, 1, hq)
    nbp_s = -(-nb_s // LANES) * LANES
    raw_all = jnp.pad(jnp.concatenate([raw_s, raw_tail], axis=2), ((0, 0), (0, 0), (0, nbp_s - nb_s), (0, 0)))
    tabs_blk_s = _rope_tables((jnp.arange(nbp_s) + 1) * SEL_BLOCK - 1)
    kct_s, vcp_s = _cmp_finish(raw_all[0], raw_all[1], kn0, seg, tabs_blk_s, place)

    state_win3 = state_kv_win.reshape(b_s, w_buf, 2 * hq)
    head = jnp.arange(n_heads)
    for l in range(n_a, depth):
        j = l - n_a
        q_s, gates_s = _q_proj(xt, mod_s_tok(l, 0), norm_mix[l][None], w_q_b[j], w_g_b[j],
                               _tile_heads(q_norm[j], N_KV_HEADS), seg, tabs_s, b_s, sm, F32)
        q3 = q_s.reshape(b_s, n_heads, HEAD_DIM)
        q_cmp = jnp.pad(q3, ((0, 0), (0, 0), (0, LANES - HEAD_DIM)))
        odd = ((head // gq) % 2 == 1)[None, :, None]
        q_sel = jnp.where(odd, jnp.pad(q3, ((0, 0), (0, 0), (HEAD_DIM, 0))), q_cmp)
        q_win = (q3[:, :, None, :] * (jnp.arange(N_KV_HEADS)[None, None, :, None] == (head // gq)[None, :, None, None])
                 ).reshape(b_s, n_heads, hq)
        g3 = gates_s.reshape(b_s, N_KV_HEADS, LANES)[:, :, :3 * gq].reshape(b_s, n_heads, 3)
        g3 = jnp.pad(g3, ((0, 0), (0, 0), (0, LANES - 3)))
        o_c, idx = _decode_a(q_cmp, kct_s, vcp_s, pos_s, nb_s, gq)
        o_s = _decode_b(idx.reshape(b_s, N_KV_HEADS * N_SEL), page_table, q_sel, q_win, g3, o_c, cache3, rows_s,
                        state_win3, win_new, pos_s, gq)
        o_tok = o_s[:, :, :HEAD_DIM].reshape(1, b_s, d)
        xt = _o_proj(xt, o_tok, mod_s_tok(l, 0), w_o_b[j], b_s)
        xt = _ffn(xt, mod_s_tok(l, 1), norm_ffn[l][None], w_gu_b[l], w_dn_b[l], b_s, tf)
    y_sample = xt.reshape(b_s, 1, d)
    kv_rows_sample = rows_s.reshape(b_s, 1, 4, N_KV_HEADS, HEAD_DIM)
    win_full_s = jnp.concatenate([state_win3, win_new], axis=1)
    win_sample = win_full_s[:, win_full_s.shape[1] - w_buf:].reshape(b_s, w_buf, 2, N_KV_HEADS, HEAD_DIM)
    pool_sample = jnp.stack(pool_s)
    return (y_prompt, y_sample, kv_rows_prompt, kv_rows_sample, win_prompt, win_sample, pool_prompt, pool_sample)
```

```python
import functools
import math

import jax
import jax.numpy as jnp
from jax import lax
from jax.experimental import pallas as pl
from jax.experimental.pallas import tpu as pltpu

F32 = jnp.float32
BF16 = jnp.bfloat16

POOL_WINDOWS = (2, 4, 8, 16)
POOL_HALO = 16
HEAD_DIM = 64
N_KV_HEADS = 4
ROT_DIM = 16
ROPE_THETA = 500000.0
SEL_BLOCK = 64
N_SEL = 16
WINDOW = 512
PAGE_SIZE = 128
FORCE_SCORE = 1.0e4
EPS = 1e-6
NEG = -float(2 ** 30)
LOG2E = 1.4426950408889634
LANES = 128
BLOCK_ROW_STRIDE = 72
VMEM_LIMIT = 56 * 1024 * 1024


def _cparams(sem):
    return pltpu.CompilerParams(dimension_semantics=sem, vmem_limit_bytes=VMEM_LIMIT)


def _dot(a, b):
    return jnp.dot(a, b, preferred_element_type=F32)


def _split(a):
    hi = a.astype(BF16)
    lo = (a - hi.astype(F32)).astype(BF16)
    return hi, lo


def _adaln(x, g, shift, scale):
    ms = jnp.mean(x * x, axis=-1, keepdims=True)
    return x * lax.rsqrt(ms + EPS) * g * (1.0 + scale) + shift


def _head_rms(x, gain, seg):
    hi, lo = _split(x * x)
    ss = _dot(hi, seg) + _dot(lo, seg)
    return x * lax.rsqrt(ss * (1.0 / HEAD_DIM) + EPS) * gain


def _rope(x, cos, sin_up, sin_dn):
    outs = []
    for a in range(x.shape[1] // LANES):
        xa = x[:, a * LANES:(a + 1) * LANES]
        up = pltpu.roll(xa, LANES - ROT_DIM // 2, 1)
        dn = pltpu.roll(xa, ROT_DIM // 2, 1)
        outs.append(xa * cos + up * sin_up + dn * sin_dn)
    return outs[0] if len(outs) == 1 else jnp.concatenate(outs, axis=1)


def _gelu_tanh(x):
    return x * (0.5 * (1.0 + jnp.tanh(math.sqrt(2.0 / math.pi) * (x + 0.044715 * (x * x * x)))))


def _mods_kernel(c_ref, w_ref, b_ref, o_ref):
    ch, cl = _split(c_ref[...])
    wh, wl = _split(w_ref[0])
    o_ref[0] = _dot(ch, wh) + _dot(ch, wl) + _dot(cl, wh) + b_ref[0]


def _mods(c_all, w, b, tn=1024):
    n_l, d, n = w.shape
    m = c_all.shape[0]
    return pl.pallas_call(
        _mods_kernel,
        out_shape=jax.ShapeDtypeStruct((n_l, m, n), F32),
        grid=(n_l, n // tn),
        in_specs=[pl.BlockSpec((m, d), lambda l, j: (0, 0)),
                  pl.BlockSpec((1, d, tn), lambda l, j: (l, 0, j)),
                  pl.BlockSpec((1, 1, tn), lambda l, j: (l, 0, j))],
        out_specs=pl.BlockSpec((1, m, tn), lambda l, j: (l, 0, j)),
        compiler_params=_cparams(("parallel", "parallel")),
        name="mods",
    )(c_all, w, b)


def _pool_kernel(x_ref, pre_ref, mod_ref, g_ref, pw_ref, ps_ref, o_ref, np_ref, hb_ref, *, tt, pos0, d):
    i = pl.program_id(1)
    x = x_ref[0]
    mod = mod_ref[0]
    h = _adaln(x, g_ref[...], mod[:, :d], mod[:, d:2 * d])

    @pl.when(i == 0)
    def _():
        hb_ref[0:POOL_HALO, :] = pre_ref[0]

    hb_ref[POOL_HALO:POOL_HALO + tt, :] = h
    pos = pos0 + i * tt + lax.broadcasted_iota(jnp.int32, (tt, 1), 0)
    pg = d // len(POOL_WINDOWS)
    ys = []
    for gi, w in enumerate(POOL_WINDOWS):
        c0 = gi * pg
        hg = h[:, c0:c0 + pg]
        s = hg
        for j in range(1, w):
            s = s + hb_ref[POOL_HALO - j:POOL_HALO - j + tt, c0:c0 + pg]
        cnt = jnp.minimum(w, pos + 1).astype(F32)
        pooled = s / cnt - hg
        ys.append(_dot(pooled.astype(BF16), pw_ref[gi]))
    y = jnp.concatenate(ys, axis=1) * ps_ref[...]
    o_ref[0] = x + mod[:, 2 * d:] * y
    last = hb_ref[tt:tt + POOL_HALO, :]
    np_ref[0] = last
    hb_ref[0:POOL_HALO, :] = last


def _pool_layer(x, prefix16, mod, g, pw, ps, pos0, tt):
    b, t, d = x.shape
    kern = functools.partial(_pool_kernel, tt=tt, pos0=pos0, d=d)
    pg = d // len(POOL_WINDOWS)
    return pl.pallas_call(
        kern,
        out_shape=(jax.ShapeDtypeStruct((b, t, d), F32), jax.ShapeDtypeStruct((b, POOL_HALO, d), F32)),
        grid=(b, t // tt),
        in_specs=[pl.BlockSpec((1, tt, d), lambda bi, i: (bi, i, 0)),
                  pl.BlockSpec((1, POOL_HALO, d), lambda bi, i: (bi, 0, 0)),
                  pl.BlockSpec((1, 1, 3 * d), lambda bi, i: (bi, 0, 0)),
                  pl.BlockSpec((1, d), lambda bi, i: (0, 0)),
                  pl.BlockSpec((len(POOL_WINDOWS), pg, pg), lambda bi, i: (0, 0, 0)),
                  pl.BlockSpec((1, d), lambda bi, i: (0, 0))],
        out_specs=(pl.BlockSpec((1, tt, d), lambda bi, i: (bi, i, 0)),
                   pl.BlockSpec((1, POOL_HALO, d), lambda bi, i: (bi, 0, 0))),
        scratch_shapes=[pltpu.VMEM((POOL_HALO + max(tt, 8), d), F32)],
        compiler_params=_cparams(("parallel", "arbitrary")),
        name="pool_layer",
    )(x, prefix16, mod, g, pw, ps)


def _ffn_kernel(x_ref, mod_ref, g_ref, wg_ref, wu_ref, wd_ref, o_ref, h_ref, acc_ref, *, d):
    f = pl.program_id(2)

    @pl.when(f == 0)
    def _():
        mod = mod_ref[0]
        h_ref[...] = _adaln(x_ref[0], g_ref[...], mod[:, :d], mod[:, d:2 * d]).astype(BF16)
        acc_ref[...] = jnp.zeros_like(acc_ref)

    hb = h_ref[...]
    g = _dot(hb, wg_ref[...])
    u = _dot(hb, wu_ref[...])
    a = (g * (1.0 / (1.0 + jnp.exp(-g)))) * u
    acc_ref[...] += _dot(a.astype(BF16), wd_ref[...])

    @pl.when(f == pl.num_programs(2) - 1)
    def _():
        o_ref[0] = x_ref[0] + mod_ref[0][:, 2 * d:] * acc_ref[...]


def _ffn(x, mod, g, w_gu, w_dn, tm, tf):
    b, t, d = x.shape
    rm = mod.shape[1]
    ff = w_dn.shape[0]
    nf = ff // tf
    kern = functools.partial(_ffn_kernel, d=d)
    return pl.pallas_call(
        kern,
        out_shape=jax.ShapeDtypeStruct((b, t, d), F32),
        grid=(b, t // tm, nf),
        in_specs=[pl.BlockSpec((1, tm, d), lambda bi, i, f: (bi, i, 0)),
                  pl.BlockSpec((1, rm, 3 * d), lambda bi, i, f: (bi, 0, 0)),
                  pl.BlockSpec((1, d), lambda bi, i, f: (0, 0)),
                  pl.BlockSpec((d, tf), lambda bi, i, f: (0, f)),
                  pl.BlockSpec((d, tf), lambda bi, i, f: (0, nf + f)),
                  pl.BlockSpec((tf, d), lambda bi, i, f: (f, 0))],
        out_specs=pl.BlockSpec((1, tm, d), lambda bi, i, f: (bi, i, 0)),
        scratch_shapes=[pltpu.VMEM((tm, d), BF16), pltpu.VMEM((tm, d), F32)],
        compiler_params=_cparams(("parallel", "parallel", "arbitrary")),
        name="ffn",
    )(x, mod, g, w_gu, w_gu, w_dn)


def _kv_kernel(x_ref, mod_ref, g_ref, w_ref, kn_ref, seg_ref, cos_ref, su_ref, sd_ref, place_ref, one_ref,
               rows_ref, win_ref, *attn_refs, d, emit_attn):
    hq = N_KV_HEADS * HEAD_DIM
    mod = mod_ref[0]
    h = _adaln(x_ref[0], g_ref[...], mod[:, :d], mod[:, d:2 * d])
    proj = _dot(h.astype(BF16), w_ref[...])
    seg = seg_ref[...]
    cos, su, sd = cos_ref[...], su_ref[...], sd_ref[...]
    k_sel = _rope(_head_rms(proj[:, 2 * hq:3 * hq], kn_ref[0], seg), cos, su, sd)
    k_win = _rope(_head_rms(proj[:, 4 * hq:5 * hq], kn_ref[1], seg), cos, su, sd)
    v_sel = proj[:, 3 * hq:4 * hq]
    v_win = proj[:, 5 * hq:6 * hq]
    rows_ref[0] = jnp.concatenate([proj[:, :2 * hq], k_sel, v_sel], axis=1)
    win_ref[0] = jnp.concatenate([k_win, v_win], axis=1)
    if emit_attn:
        kst_ref, vs_ref, kwt_ref, vw_ref = attn_refs
        tm = k_sel.shape[0]
        zeros = jnp.zeros((HEAD_DIM, tm), BF16)
        for kt_ref, kk in ((kst_ref, k_sel), (kwt_ref, k_win)):
            kt = kk.T.astype(BF16)
            for k in range(N_KV_HEADS):
                kt_ref[0, k, 0:HEAD_DIM, :] = kt[k * HEAD_DIM:(k + 1) * HEAD_DIM, :]
                kt_ref[0, k, HEAD_DIM:2 * HEAD_DIM, :] = zeros
        for va_ref, vv in ((vs_ref, v_sel), (vw_ref, v_win)):
            va = (_dot(vv.astype(BF16), place_ref[...]) + one_ref[...]).astype(BF16)
            for k in range(N_KV_HEADS):
                va_ref[0, k] = va[:, k * LANES:(k + 1) * LANES]


def _kv_proj(x, mod, g, w_kv, kn, seg, tabs, place, one_row, tm, emit_attn):
    b, t, d = x.shape
    rm = mod.shape[1]
    hq = N_KV_HEADS * HEAD_DIM
    kern = functools.partial(_kv_kernel, d=d, emit_attn=emit_attn)
    const2 = lambda bi, i: (0, 0)
    tab_spec = pl.BlockSpec((tm, LANES), lambda bi, i: (i, 0))
    out_shape = [jax.ShapeDtypeStruct((b, t, 4 * hq), F32), jax.ShapeDtypeStruct((b, t, 2 * hq), F32)]
    out_specs = [pl.BlockSpec((1, tm, 4 * hq), lambda bi, i: (bi, i, 0)),
                 pl.BlockSpec((1, tm, 2 * hq), lambda bi, i: (bi, i, 0))]
    if emit_attn:
        kt_shape = jax.ShapeDtypeStruct((b, N_KV_HEADS, 2 * HEAD_DIM, t), BF16)
        va_shape = jax.ShapeDtypeStruct((b, N_KV_HEADS, t, LANES), BF16)
        kt_spec = pl.BlockSpec((1, N_KV_HEADS, 2 * HEAD_DIM, tm), lambda bi, i: (bi, 0, 0, i))
        va_spec = pl.BlockSpec((1, N_KV_HEADS, tm, LANES), lambda bi, i: (bi, 0, i, 0))
        out_shape += [kt_shape, va_shape, kt_shape, va_shape]
        out_specs += [kt_spec, va_spec, kt_spec, va_spec]
    return pl.pallas_call(
        kern,
        out_shape=tuple(out_shape),
        grid=(b, t // tm),
        in_specs=[pl.BlockSpec((1, tm, d), lambda bi, i: (bi, i, 0)),
                  pl.BlockSpec((1, rm, 2 * d), lambda bi, i: (bi, 0, 0)),
                  pl.BlockSpec((1, d), const2),
                  pl.BlockSpec((d, 6 * hq), const2),
                  pl.BlockSpec((2, 1, hq), lambda bi, i: (0, 0, 0)),
                  pl.BlockSpec((hq, hq), const2),
                  tab_spec, tab_spec, tab_spec,
                  pl.BlockSpec((hq, N_KV_HEADS * LANES), const2),
                  pl.BlockSpec((1, N_KV_HEADS * LANES), const2)],
        out_specs=tuple(out_specs),
        compiler_params=_cparams(("parallel", "parallel")),
        name="kv_proj",
    )(x, mod, g, w_kv, kn, seg, *tabs, place, one_row)


def _compress_kernel(pt_ref, cache_ref, pe_ref, w1_ref, w2_ref, o_ref, buf_ref, sem_ref, *, n_pages, n_b):
    hq = N_KV_HEADS * HEAD_DIM
    nb = 2 * n_pages
    sl = pl.program_id(0)
    b = pl.program_id(1)
    step = sl * n_b + b
    cur = step % 2

    def block_copy(sl_, b_, slot_, n, pair):
        page = pt_ref[b_, n // 2]
        col = pl.multiple_of(sl_ * hq + pair * LANES, LANES)
        return pltpu.make_async_copy(
            cache_ref.at[page, pl.ds((n % 2) * SEL_BLOCK, SEL_BLOCK), pl.ds(col, LANES)],
            buf_ref.at[slot_, pair, pl.ds(n * BLOCK_ROW_STRIDE, SEL_BLOCK), :],
            sem_ref.at[slot_])

    def issue(sl_, b_, slot_):
        for n in range(nb):
            for pair in range(2):
                block_copy(sl_, b_, slot_, n, pair).start()

    @pl.when(step == 0)
    def _():
        issue(sl, b, 0)

    @pl.when(step + 1 < 2 * n_b)
    def _():
        nxt = step + 1
        issue(nxt // n_b, nxt % n_b, 1 - cur)

    for n in range(nb):
        for pair in range(2):
            block_copy(sl, b, cur, n, pair).wait()

    pe = pe_ref[0]
    hid = [jnp.zeros((nb, hq), F32), jnp.zeros((nb, hq), F32)]
    for sp in range(SEL_BLOCK // 2):
        w = w1_ref[0, sp]
        for pair in range(2):
            xs = [buf_ref[cur, pair, pl.ds(s, nb, stride=BLOCK_ROW_STRIDE), :]
                  + pe[s:s + 1, pair * LANES:(pair + 1) * LANES] for s in (2 * sp, 2 * sp + 1)]
            hid[pair] = hid[pair] + _dot(jnp.concatenate(xs, axis=1).astype(BF16), w)
    act = _gelu_tanh(jnp.concatenate(hid, axis=1))
    o_ref[0, 0] = _dot(act.astype(BF16), w2_ref[0])


def _compress(page_table, cache3, pe_t, w1cat, w2bd):
    n_b, n_pages = page_table.shape
    hq = N_KV_HEADS * HEAD_DIM
    nb = 2 * n_pages
    kern = functools.partial(_compress_kernel, n_pages=n_pages, n_b=n_b)
    grid_spec = pltpu.PrefetchScalarGridSpec(
        num_scalar_prefetch=1,
        grid=(2, n_b),
        in_specs=[pl.BlockSpec(memory_space=pl.ANY),
                  pl.BlockSpec((1, SEL_BLOCK, hq), lambda s, b, pt: (s, 0, 0)),
                  pl.BlockSpec((1, SEL_BLOCK // 2, hq, hq), lambda s, b, pt: (s, 0, 0, 0)),
                  pl.BlockSpec((1, 2 * hq, hq), lambda s, b, pt: (s, 0, 0))],
        out_specs=pl.BlockSpec((1, 1, nb, hq), lambda s, b, pt: (s, b, 0, 0)),
        scratch_shapes=[pltpu.VMEM((2, 2, nb * BLOCK_ROW_STRIDE, LANES), F32),
                        pltpu.SemaphoreType.DMA((2,))])
    return pl.pallas_call(
        kern,
        out_shape=jax.ShapeDtypeStruct((2, n_b, nb, hq), F32),
        grid_spec=grid_spec,
        compiler_params=_cparams(("arbitrary", "arbitrary")),
        name="compress",
    )(page_table, cache3, pe_t, w1cat, w2bd)


def _tail_kernel(z_ref, pe_ref, w1_ref, w2_ref, o_ref):
    z = z_ref[...] + pe_ref[0]
    hid = _gelu_tanh(_dot(z.astype(BF16), w1_ref[0]))
    o_ref[0] = _dot(hid.astype(BF16), w2_ref[0])


def _compress_tail(z, pe_flat, w1, w2):
    m, kdim = z.shape[0] // 2, z.shape[1]
    hid = w1.shape[2]
    return pl.pallas_call(
        _tail_kernel,
        out_shape=jax.ShapeDtypeStruct((2, m, HEAD_DIM), F32),
        grid=(2,),
        in_specs=[pl.BlockSpec((m, kdim), lambda s: (s, 0)),
                  pl.BlockSpec((1, 1, kdim), lambda s: (s, 0, 0)),
                  pl.BlockSpec((1, kdim, hid), lambda s: (s, 0, 0)),
                  pl.BlockSpec((1, hid, HEAD_DIM), lambda s: (s, 0, 0))],
        out_specs=pl.BlockSpec((1, m, HEAD_DIM), lambda s: (s, 0, 0)),
        compiler_params=_cparams(("parallel",)),
        name="compress_tail",
    )(z, pe_flat, w1, w2)


def _cmp_finish_kernel(kc_ref, vc_ref, kn_ref, seg_ref, cos_ref, su_ref, sd_ref, place_ref, kt_ref, vp_ref):
    kc = _rope(_head_rms(kc_ref[0], kn_ref[...], seg_ref[...]), cos_ref[...], su_ref[...], sd_ref[...])
    kt = kc.T.astype(BF16)
    nbp = kc.shape[0]
    zeros = jnp.zeros((HEAD_DIM, nbp), BF16)
    vp = _dot(vc_ref[0].astype(BF16), place_ref[...]).astype(BF16)
    for k in range(N_KV_HEADS):
        kt_ref[0, k, 0:HEAD_DIM, :] = kt[k * HEAD_DIM:(k + 1) * HEAD_DIM, :]
        kt_ref[0, k, HEAD_DIM:2 * HEAD_DIM, :] = zeros
        vp_ref[0, k] = vp[:, k * LANES:(k + 1) * LANES]


def _cmp_finish(kc_raw, vc_raw, kn0, seg, tabs, place):
    b, nbp, hq = kc_raw.shape
    const2 = lambda bi: (0, 0)
    tab_spec = pl.BlockSpec((nbp, LANES), const2)
    return pl.pallas_call(
        _cmp_finish_kernel,
        out_shape=(jax.ShapeDtypeStruct((b, N_KV_HEADS, 2 * HEAD_DIM, nbp), BF16),
                   jax.ShapeDtypeStruct((b, N_KV_HEADS, nbp, LANES), BF16)),
        grid=(b,),
        in_specs=[pl.BlockSpec((1, nbp, hq), lambda bi: (bi, 0, 0)),
                  pl.BlockSpec((1, nbp, hq), lambda bi: (bi, 0, 0)),
                  pl.BlockSpec((1, hq), const2),
                  pl.BlockSpec((hq, hq), const2),
                  tab_spec, tab_spec, tab_spec,
                  pl.BlockSpec((hq, N_KV_HEADS * LANES), const2)],
        out_specs=(pl.BlockSpec((1, N_KV_HEADS, 2 * HEAD_DIM, nbp), lambda bi: (bi, 0, 0, 0)),
                   pl.BlockSpec((1, N_KV_HEADS, nbp, LANES), lambda bi: (bi, 0, 0, 0))),
        compiler_params=_cparams(("parallel",)),
        name="cmp_finish",
    )(kc_raw, vc_raw, kn0, seg, *tabs, place)


def _q_kernel(x_ref, mod_ref, g_ref, wq_ref, wg_ref, qn_ref, seg_ref, cos_ref, su_ref, sd_ref,
              q_ref, gate_ref, *, d, q_scale):
    hq = N_KV_HEADS * HEAD_DIM
    mod = mod_ref[0]
    hb = _adaln(x_ref[0], g_ref[...], mod[:, :d], mod[:, d:2 * d]).astype(BF16)
    q = _dot(hb, wq_ref[...])
    seg = seg_ref[...]
    cos, su, sd = cos_ref[...], su_ref[...], sd_ref[...]
    for c in range(d // hq):
        qc = _rope(_head_rms(q[:, c * hq:(c + 1) * hq], qn_ref[...], seg), cos, su, sd) * q_scale
        q_ref[0, :, c * hq:(c + 1) * hq] = qc.astype(q_ref.dtype)
    gl = _dot(hb, wg_ref[...])
    gate_ref[0] = 1.0 / (1.0 + jnp.exp(-gl))


def _q_proj(x, mod, g, w_q, w_g, qn, seg, tabs, tm, q_scale, q_dtype):
    b, t, d = x.shape
    rm = mod.shape[1]
    hq = N_KV_HEADS * HEAD_DIM
    ng = w_g.shape[1]
    kern = functools.partial(_q_kernel, d=d, q_scale=q_scale)
    const2 = lambda bi, i: (0, 0)
    tab_spec = pl.BlockSpec((tm, LANES), lambda bi, i: (i, 0))
    return pl.pallas_call(
        kern,
        out_shape=(jax.ShapeDtypeStruct((b, t, d), q_dtype), jax.ShapeDtypeStruct((b, t, ng), F32)),
        grid=(b, t // tm),
        in_specs=[pl.BlockSpec((1, tm, d), lambda bi, i: (bi, i, 0)),
                  pl.BlockSpec((1, rm, 3 * d), lambda bi, i: (bi, 0, 0)),
                  pl.BlockSpec((1, d), const2),
                  pl.BlockSpec((d, d), const2),
                  pl.BlockSpec((d, ng), const2),
                  pl.BlockSpec((1, hq), const2),
                  pl.BlockSpec((hq, hq), const2),
                  tab_spec, tab_spec, tab_spec],
        out_specs=(pl.BlockSpec((1, tm, d), lambda bi, i: (bi, i, 0)),
                   pl.BlockSpec((1, tm, ng), lambda bi, i: (bi, i, 0))),
        compiler_params=_cparams(("parallel", "parallel")),
        name="q_proj",
    )(x, mod, g, w_q, w_g, qn, seg, *tabs)


def _oproj_kernel(x_ref, o_ref, mod_ref, w_ref, y_ref, *, d):
    y_ref[0] = x_ref[0] + mod_ref[0][:, 2 * d:] * _dot(o_ref[0].astype(BF16), w_ref[...])


def _o_proj(x, o, mod, w_o, tm):
    b, t, d = x.shape
    rm = mod.shape[1]
    kern = functools.partial(_oproj_kernel, d=d)
    return pl.pallas_call(
        kern,
        out_shape=jax.ShapeDtypeStruct((b, t, d), F32),
        grid=(b, t // tm),
        in_specs=[pl.BlockSpec((1, tm, d), lambda bi, i: (bi, i, 0)),
                  pl.BlockSpec((1, tm, d), lambda bi, i: (bi, i, 0)),
                  pl.BlockSpec((1, rm, 3 * d), lambda bi, i: (bi, 0, 0)),
                  pl.BlockSpec((d, d), lambda bi, i: (0, 0))],
        out_specs=pl.BlockSpec((1, tm, d), lambda bi, i: (bi, i, 0)),
        compiler_params=_cparams(("parallel", "parallel")),
        name="o_proj",
    )(x, o, mod, w_o)


def _attn_kernel(q_ref, gate_ref, kct_ref, vcp_ref, kst_ref, vs_ref, kwt_ref, vw_ref, e_ref, pin_ref, pout_ref,
                 dmask_ref, wmask_ref, o_ref, qa_ref, m_ref, acc_ref, s_ref, *, tq, ck, ck_big, nb):
    gq = q_ref.shape[2] // HEAD_DIM
    rows = gq * tq
    i = pl.program_id(2)
    t0 = i * tq

    qslab = q_ref[0]
    qp = jnp.concatenate([_dot(qslab, pin_ref[g]).astype(BF16) for g in range(gq)], axis=0)
    pos_r = t0 + lax.broadcasted_iota(jnp.int32, (rows, 1), 0) % tq

    s_c = _dot(qp, kct_ref[0, 0])
    blk = lax.broadcasted_iota(jnp.int32, (rows, nb), 1)
    ok_c = (blk + 1) * SEL_BLOCK - 1 <= pos_r
    m_c = jnp.max(jnp.where(ok_c, s_c, -jnp.inf), axis=1, keepdims=True)
    m_c = jnp.where(m_c == -jnp.inf, 0.0, m_c)
    p_c = jnp.where(ok_c, jnp.exp2(s_c - m_c), 0.0)
    p_c = p_c / jnp.maximum(jnp.sum(p_c, axis=1, keepdims=True), 1e-30)
    o_c = _dot(p_c.astype(BF16), vcp_ref[0, 0])

    imp = p_c[0:tq]
    for g in range(1, gq):
        imp = imp + p_c[g * tq:(g + 1) * tq]
    imp_t = imp.T
    n_io = lax.broadcasted_iota(jnp.int32, (nb, tq), 0)
    cur = (t0 + lax.broadcasted_iota(jnp.int32, (nb, tq), 1)) // SEL_BLOCK
    forced = (n_io == 0) | (n_io == cur) | (n_io == cur - 1)
    v = jnp.where(forced, -2.0, jnp.where(n_io > cur, -1.0, imp_t))
    sel = jnp.where(forced, 1.0, 0.0)
    for _ in range(N_SEL - 3):
        mx = jnp.max(v, axis=0, keepdims=True)
        first = jnp.min(jnp.where(v == mx, n_io, nb), axis=0, keepdims=True)
        pick = n_io == first
        sel = jnp.where(pick, 1.0, sel)
        v = jnp.where(pick, -2.0, v)
    bias_t = jnp.where((sel > 0.0) & (n_io <= cur), 0.0, NEG)
    bias = bias_t.T.astype(BF16)
    qa_ref[...] = jnp.concatenate([jnp.concatenate([bias] * gq, axis=0), qp], axis=1)

    m_ref[...] = jnp.full(m_ref.shape, NEG * 4.0, F32)

    def score_chunk(k0, width, causal):
        kaug = jnp.concatenate([e_ref[:, pl.ds(k0, width)], kst_ref[0, 0, :, pl.ds(k0, width)]], axis=0)
        s = _dot(qa_ref[...], kaug)
        if causal:
            s = s + jnp.concatenate([dmask_ref[0]] * gq, axis=0)
        s_ref[:, pl.ds(k0, width)] = s
        m = m_ref[...]
        for a in range(width // LANES):
            m = jnp.maximum(m, s[:, a * LANES:(a + 1) * LANES])
        m_ref[...] = m

    def value_chunk(k0, width):
        mb = m_ref[...]
        p = jnp.exp2(s_ref[:, pl.ds(k0, width)] - jnp.concatenate([mb] * (width // LANES), axis=1))
        acc_ref[...] += _dot(p.astype(BF16), vs_ref[0, 0, pl.ds(k0, width), :])

    c_last = t0 // ck
    per_big = ck_big // ck
    n_big = c_last // per_big
    n_small = c_last - n_big * per_big
    small0 = n_big * ck_big

    def loop(n, fn):
        def body(c, carry):
            fn(c)
            return carry
        lax.fori_loop(0, n, body, 0)

    loop(n_big, lambda c: score_chunk(pl.multiple_of(c * ck_big, ck_big), ck_big, False))
    loop(n_small, lambda c: score_chunk(pl.multiple_of(small0 + c * ck, ck), ck, False))
    score_chunk(pl.multiple_of(c_last * ck, ck), ck, True)

    m_row = jnp.max(m_ref[...], axis=1, keepdims=True)
    m_ref[...] = jnp.broadcast_to(m_row, m_ref.shape)
    acc_ref[...] = jnp.zeros_like(acc_ref)
    loop(n_big, lambda c: value_chunk(pl.multiple_of(c * ck_big, ck_big), ck_big))
    loop(n_small + 1, lambda c: value_chunk(pl.multiple_of(small0 + c * ck, ck), ck))
    acc = acc_ref[...]
    o_s = acc / acc[:, HEAD_DIM:HEAD_DIM + 1]

    wl = WINDOW + tq
    w0 = pl.multiple_of(jnp.maximum(t0 - WINDOW, 0), LANES)
    s_w = _dot(qp, kwt_ref[0, 0, :, pl.ds(w0, wl)]) + jnp.concatenate([wmask_ref[0]] * gq, axis=0)
    m_w = jnp.max(s_w, axis=1, keepdims=True)
    p_w = jnp.exp2(s_w - m_w)
    acc_w = _dot(p_w.astype(BF16), vw_ref[0, 0, pl.ds(w0, wl), :])
    o_w = acc_w / acc_w[:, HEAD_DIM:HEAD_DIM + 1]

    gates = gate_ref[0, 0]
    out = jnp.zeros((tq, gq * HEAD_DIM), F32)
    for g in range(gq):
        r = slice(g * tq, (g + 1) * tq)
        mix = (gates[:, 3 * g:3 * g + 1] * o_c[r] + gates[:, 3 * g + 1:3 * g + 2] * o_s[r]
               + gates[:, 3 * g + 2:3 * g + 3] * o_w[r])
        out = out + _dot(mix.astype(BF16), pout_ref[g])
    o_ref[0] = out.astype(o_ref.dtype)


def _attention(q, gates, kct, vcp, kst, vs, kwt, vw, e_mat, pin, pout, tq, ck, ck_big):
    b, t, d = q.shape
    nb = kct.shape[3]
    gq = d // (N_KV_HEADS * HEAD_DIM)
    sw = gq * HEAD_DIM
    rows = gq * tq
    kern = functools.partial(_attn_kernel, tq=tq, ck=ck, ck_big=ck_big, nb=nb)
    per_head4 = lambda bi, k, i: (bi, k, 0, 0)
    tt = jnp.arange(tq)[None, :, None]
    n_phase = ck // tq
    jd = jnp.arange(ck)[None, None, :]
    dmask = jnp.where(jd <= jnp.arange(n_phase)[:, None, None] * tq + tt, 0.0, NEG).astype(F32)
    wl = WINDOW + tq
    n_early = WINDOW // tq
    jw = jnp.arange(wl)[None, None, :]
    pos_e = jnp.arange(n_early)[:, None, None] * tq + tt
    early = (jw <= pos_e) & (pos_e - jw < WINDOW)
    steady = (jw > tt) & (jw <= tt + WINDOW)
    wmask = jnp.where(jnp.concatenate([early, steady], axis=0), 0.0, NEG).astype(F32)
    return pl.pallas_call(
        kern,
        out_shape=jax.ShapeDtypeStruct((b, t, d), BF16),
        grid=(b, N_KV_HEADS, t // tq),
        in_specs=[pl.BlockSpec((1, tq, sw), lambda bi, k, i: (bi, i, k)),
                  pl.BlockSpec((1, 1, tq, LANES), lambda bi, k, i: (bi, k, i, 0)),
                  pl.BlockSpec((1, 1, 2 * HEAD_DIM, nb), per_head4),
                  pl.BlockSpec((1, 1, nb, LANES), per_head4),
                  pl.BlockSpec((1, 1, 2 * HEAD_DIM, t), per_head4),
                  pl.BlockSpec((1, 1, t, LANES), per_head4),
                  pl.BlockSpec((1, 1, 2 * HEAD_DIM, t), per_head4),
                  pl.BlockSpec((1, 1, t, LANES), per_head4),
                  pl.BlockSpec((nb, t), lambda bi, k, i: (0, 0)),
                  pl.BlockSpec((gq, sw, LANES), lambda bi, k, i: (0, 0, 0)),
                  pl.BlockSpec((gq, LANES, sw), lambda bi, k, i: (0, 0, 0)),
                  pl.BlockSpec((1, tq, ck), lambda bi, k, i: (i % n_phase, 0, 0)),
                  pl.BlockSpec((1, tq, wl), lambda bi, k, i: (jnp.minimum(i, n_early), 0, 0))],
        out_specs=pl.BlockSpec((1, tq, sw), lambda bi, k, i: (bi, i, k)),
        scratch_shapes=[pltpu.VMEM((rows, nb + LANES), BF16),
                        pltpu.VMEM((rows, LANES), F32),
                        pltpu.VMEM((rows, LANES), F32),
                        pltpu.VMEM((rows, t), F32)],
        compiler_params=_cparams(("parallel", "parallel", "arbitrary")),
        name="nsa_attention",
    )(q, gates, kct, vcp, kst, vs, kwt, vw, e_mat, pin, pout, dmask, wmask)


def _row_kvh(shape, gq):
    return lax.broadcasted_iota(jnp.int32, shape, 0) // gq


def _dec_a_kernel(q_ref, kct_ref, vcp_ref, oc_ref, idx_ref, *, pos, nb, gq):
    nh = q_ref.shape[1]
    nbp = kct_ref.shape[3]
    qb = q_ref[0].astype(BF16)
    rk = _row_kvh((nh, nbp), gq)
    s = jnp.zeros((nh, nbp), F32)
    for k in range(N_KV_HEADS):
        s = jnp.where(rk == k, _dot(qb, kct_ref[0, k]), s)
    blk = lax.broadcasted_iota(jnp.int32, (nh, nbp), 1)
    ok = ((blk + 1) * SEL_BLOCK - 1 <= pos) & (blk < nb)
    m = jnp.max(jnp.where(ok, s, -jnp.inf), axis=1, keepdims=True)
    m = jnp.where(m == -jnp.inf, 0.0, m)
    p = jnp.where(ok, jnp.exp(s - m), 0.0)
    p = p / jnp.maximum(jnp.sum(p, axis=1, keepdims=True), 1e-30)
    pb = p.astype(BF16)
    rk_o = _row_kvh((nh, LANES), gq)
    o_c = jnp.zeros((nh, LANES), F32)
    for k in range(N_KV_HEADS):
        o_c = jnp.where(rk_o == k, _dot(pb, vcp_ref[0, k]), o_c)
    oc_ref[0] = o_c
    imp = jnp.concatenate([jnp.sum(jnp.where(rk == k, p, 0.0), axis=0, keepdims=True)
                           for k in range(N_KV_HEADS)], axis=0)
    n_io = lax.broadcasted_iota(jnp.int32, (N_KV_HEADS, nbp), 1)
    cur = pos // SEL_BLOCK
    forced = (n_io == 0) | (n_io == cur) | (n_io == cur - 1)
    v = jnp.where(forced, FORCE_SCORE, jnp.where(n_io > cur, -1.0, imp))
    v = jnp.where(n_io < nb, v, -3.0)
    col = lax.broadcasted_iota(jnp.int32, (N_KV_HEADS, N_SEL), 1)
    idx = jnp.zeros((N_KV_HEADS, N_SEL), jnp.int32)
    for r in range(N_SEL):
        mx = jnp.max(v, axis=1, keepdims=True)
        first = jnp.min(jnp.where(v == mx, n_io, nbp), axis=1, keepdims=True)
        idx = jnp.where(col == r, first, idx)
        v = jnp.where(n_io == first, -4.0, v)
    idx_ref[0] = idx


def _decode_a(q_pad, kct, vcp, pos, nb, gq):
    b, nh, _ = q_pad.shape
    nbp = kct.shape[3]
    kern = functools.partial(_dec_a_kernel, pos=pos, nb=nb, gq=gq)
    return pl.pallas_call(
        kern,
        out_shape=(jax.ShapeDtypeStruct((b, nh, LANES), F32),
                   jax.ShapeDtypeStruct((b, N_KV_HEADS, N_SEL), jnp.int32)),
        grid=(b,),
        in_specs=[pl.BlockSpec((1, nh, LANES), lambda bi: (bi, 0, 0)),
                  pl.BlockSpec((1, N_KV_HEADS, 2 * HEAD_DIM, nbp), lambda bi: (bi, 0, 0, 0)),
                  pl.BlockSpec((1, N_KV_HEADS, nbp, LANES), lambda bi: (bi, 0, 0, 0))],
        out_specs=(pl.BlockSpec((1, nh, LANES), lambda bi: (bi, 0, 0)),
                   pl.BlockSpec((1, N_KV_HEADS, N_SEL), lambda bi: (bi, 0, 0))),
        compiler_params=_cparams(("parallel",)),
        name="decode_cmp_topk",
    )(q_pad, kct, vcp)


def _dec_b_kernel(idx_ref, pt_ref, qsel_ref, qwin_ref, gate_ref, oc_ref, cache_ref, new_ref, swin_ref, wnew_ref,
                  o_ref, kbuf_ref, vbuf_ref, sem_ref, *, pos, nb_past, gq, n_b):
    hq = N_KV_HEADS * HEAD_DIM
    nh = qsel_ref.shape[1]
    b = pl.program_id(0)
    cur = b % 2
    nkeys = N_SEL * SEL_BLOCK

    def copies(b_, slot_, k, j):
        n = idx_ref[b_, k * N_SEL + j]
        n_c = jnp.minimum(n, nb_past - 1)
        page = pt_ref[b_, n_c // 2]
        r0 = pl.multiple_of((n_c % 2) * SEL_BLOCK, SEL_BLOCK)
        pair = (k // 2) * LANES
        ck = pltpu.make_async_copy(cache_ref.at[page, pl.ds(r0, SEL_BLOCK), pl.ds(2 * hq + pair, LANES)],
                                   kbuf_ref.at[slot_, k, j], sem_ref.at[slot_])
        cv = pltpu.make_async_copy(cache_ref.at[page, pl.ds(r0, SEL_BLOCK), pl.ds(3 * hq + pair, LANES)],
                                   vbuf_ref.at[slot_, k, j], sem_ref.at[slot_])
        return n, ck, cv

    def issue(b_, slot_):
        for k in range(N_KV_HEADS):
            for j in range(N_SEL):
                n, ck, cv = copies(b_, slot_, k, j)

                @pl.when(n < nb_past)
                def _():
                    ck.start()
                    cv.start()

    @pl.when(b == 0)
    def _():
        issue(b, 0)

    @pl.when(b + 1 < n_b)
    def _():
        issue(b + 1, 1 - cur)

    new = new_ref[0]
    lane64 = lax.broadcasted_iota(jnp.int32, (1, nkeys), 1)
    kps = []
    for k in range(N_KV_HEADS):
        pair = (k // 2) * LANES
        kp = lane64 % SEL_BLOCK
        for j in range(N_SEL):
            n, ck, cv = copies(b, cur, k, j)

            @pl.when(n < nb_past)
            def _():
                ck.wait()
                cv.wait()

            @pl.when(n >= nb_past)
            def _():
                row0 = lax.broadcasted_iota(jnp.int32, (SEL_BLOCK, LANES), 0) == 0
                kbuf_ref[cur, k, j] = jnp.where(row0, new[:, 2 * hq + pair:2 * hq + pair + LANES], 0.0)
                vbuf_ref[cur, k, j] = jnp.where(row0, new[:, 3 * hq + pair:3 * hq + pair + LANES], 0.0)

            kp = kp + jnp.where(lane64 // SEL_BLOCK == j, n * SEL_BLOCK, 0)
        kps.append(kp)

    qs = qsel_ref[0].astype(BF16)
    rk = _row_kvh((nh, nkeys), gq)
    s = jnp.zeros((nh, nkeys), F32)
    kpos = jnp.zeros((nh, nkeys), jnp.int32)
    for k in range(N_KV_HEADS):
        kk = kbuf_ref[cur, k].reshape(nkeys, LANES).astype(BF16)
        sk = lax.dot_general(qs, kk, (((1,), (1,)), ((), ())), preferred_element_type=F32)
        s = jnp.where(rk == k, sk, s)
        kpos = jnp.where(rk == k, kps[k], kpos)
    ok = kpos <= pos
    m = jnp.max(jnp.where(ok, s, -jnp.inf), axis=1, keepdims=True)
    p = jnp.where(ok, jnp.exp(s - m), 0.0)
    p = p / jnp.maximum(jnp.sum(p, axis=1, keepdims=True), 1e-30)
    pb = p.astype(BF16)
    rk_o = _row_kvh((nh, LANES), gq)
    o_s = jnp.zeros((nh, LANES), F32)
    for k in range(N_KV_HEADS):
        vk = vbuf_ref[cur, k].reshape(nkeys, LANES).astype(BF16)
        o_s = jnp.where(rk_o == k, _dot(pb, vk), o_s)
    o_s = jnp.where(rk_o % 2 == 1, pltpu.roll(o_s, HEAD_DIM, 1), o_s)

    sw = swin_ref[0]
    wn = wnew_ref[0]
    qw = qwin_ref[0]
    s_w = lax.dot_general(qw.astype(BF16), sw[:, :hq].astype(BF16), (((1,), (1,)), ((), ())),
                          preferred_element_type=F32)
    s_n = jnp.sum(qw * wn[:, :hq], axis=1, keepdims=True)
    w_buf = sw.shape[0]
    ridx = lax.broadcasted_iota(jnp.int32, (nh, w_buf), 1)
    ok_w = w_buf - ridx < WINDOW
    m_w = jnp.maximum(jnp.max(jnp.where(ok_w, s_w, -jnp.inf), axis=1, keepdims=True), s_n)
    p_w = jnp.where(ok_w, jnp.exp(s_w - m_w), 0.0)
    p_n = jnp.exp(s_n - m_w)
    l_w = jnp.sum(p_w, axis=1, keepdims=True) + p_n
    o_full = (_dot((p_w / l_w).astype(BF16), sw[:, hq:].astype(BF16)) + (p_n / l_w) * wn[:, hq:])
    o_w = jnp.where(rk_o >= 2, o_full[:, LANES:2 * LANES], o_full[:, 0:LANES])
    o_w = jnp.where(rk_o % 2 == 1, pltpu.roll(o_w, HEAD_DIM, 1), o_w)

    gates = gate_ref[0]
    o_ref[0] = gates[:, 0:1] * oc_ref[0] + gates[:, 1:2] * o_s + gates[:, 2:3] * o_w


def _decode_b(idx_flat, page_table, q_sel, q_win, gates, o_c, cache3, new_rows, state_win, win_new, pos, gq):
    b, nh, _ = q_sel.shape
    hq = N_KV_HEADS * HEAD_DIM
    w_buf = state_win.shape[1]
    nb_past = page_table.shape[1] * (PAGE_SIZE // SEL_BLOCK)
    kern = functools.partial(_dec_b_kernel, pos=pos, nb_past=nb_past, gq=gq, n_b=b)
    grid_spec = pltpu.PrefetchScalarGridSpec(
        num_scalar_prefetch=2,
        grid=(b,),
        in_specs=[pl.BlockSpec((1, nh, LANES), lambda bi, ix, pt: (bi, 0, 0)),
                  pl.BlockSpec((1, nh, hq), lambda bi, ix, pt: (bi, 0, 0)),
                  pl.BlockSpec((1, nh, LANES), lambda bi, ix, pt: (bi, 0, 0)),
                  pl.BlockSpec((1, nh, LANES), lambda bi, ix, pt: (bi, 0, 0)),
                  pl.BlockSpec(memory_space=pl.ANY),
                  pl.BlockSpec((1, 1, 4 * hq), lambda bi, ix, pt: (bi, 0, 0)),
                  pl.BlockSpec((1, w_buf, 2 * hq), lambda bi, ix, pt: (bi, 0, 0)),
                  pl.BlockSpec((1, 1, 2 * hq), lambda bi, ix, pt: (bi, 0, 0))],
        out_specs=pl.BlockSpec((1, nh, LANES), lambda bi, ix, pt: (bi, 0, 0)),
        scratch_shapes=[pltpu.VMEM((2, N_KV_HEADS, N_SEL, SEL_BLOCK, LANES), F32),
                        pltpu.VMEM((2, N_KV_HEADS, N_SEL, SEL_BLOCK, LANES), F32),
                        pltpu.SemaphoreType.DMA((2,))])
    return pl.pallas_call(
        kern,
        out_shape=jax.ShapeDtypeStruct((b, nh, LANES), F32),
        grid_spec=grid_spec,
        compiler_params=_cparams(("arbitrary",)),
        name="decode_sel_win",
    )(idx_flat, page_table, q_sel, q_win, gates, o_c, cache3, new_rows, state_win, win_new)


def _rope_tables(pos):
    half = ROT_DIM // 2
    inv = jnp.power(jnp.float32(ROPE_THETA), -jnp.arange(half, dtype=F32) * 2.0 / ROT_DIM)
    ang = pos.astype(F32)[:, None] * inv[None, :]
    cos, sin = jnp.cos(ang), jnp.sin(ang)
    r = jnp.arange(LANES) % HEAD_DIM
    f = r % half
    cos_t = jnp.where(r < ROT_DIM, cos[:, f], 1.0)
    sin_up = jnp.where(r < half, -sin[:, f], 0.0)
    sin_dn = jnp.where((r >= half) & (r < ROT_DIM), sin[:, f], 0.0)
    return cos_t, sin_up, sin_dn


def _tile_heads(v, n):
    return jnp.tile(v.astype(F32), n)[None, :]


def kernel(x_prompt, x_sample, cache_kv, state_kv_win, state_pool, page_table, c_prompt, c_sample, ada_w, ada_b, norm_mix, norm_ffn, pool_w, pool_scale, ada_kv_w, ada_kv_b, norm_kv, w_kv, k_norm, cmp_pe, cmp_w1, cmp_w2, w_qg, q_norm, w_o, w_gate_up, w_down):
    b_p, seq, d = x_prompt.shape
    b_s, dec_seq, _ = x_sample.shape
    depth = ada_w.shape[0]
    n_a = pool_w.shape[0]
    n_heads = d // HEAD_DIM
    gq = n_heads // N_KV_HEADS
    hq = N_KV_HEADS * HEAD_DIM
    n_pages = page_table.shape[1]
    past_len = n_pages * PAGE_SIZE
    w_buf = state_kv_win.shape[1]
    d_ff = w_down.shape[1]
    assert dec_seq == 1 and hq == 2 * LANES and seq % PAGE_SIZE == 0
    sm = HEAD_DIM ** -0.5

    w_gu_b = w_gate_up.astype(BF16)
    w_dn_b = w_down.astype(BF16)
    w_kv_b = w_kv.astype(BF16)
    pool_w_b = pool_w.astype(BF16)
    w_q_b = w_qg[:, :, :d].astype(BF16)
    wg_cols = w_qg[:, :, d:].reshape(-1, d, N_KV_HEADS, 3 * gq)
    w_g_b = jnp.pad(wg_cols, ((0, 0), (0, 0), (0, 0), (0, LANES - 3 * gq))).reshape(-1, d, N_KV_HEADS * LANES).astype(BF16)
    w_o_b = w_o.astype(BF16)

    head_of = jnp.arange(hq) // HEAD_DIM
    seg = (head_of[:, None] == head_of[None, :]).astype(BF16)
    lane_in = jnp.arange(hq)
    lane_out = jnp.arange(N_KV_HEADS * LANES)
    place = ((lane_out[None, :] // LANES == lane_in[:, None] // HEAD_DIM)
             & (lane_out[None, :] % LANES == lane_in[:, None] % HEAD_DIM)).astype(BF16)
    one_row = (lane_out % LANES == HEAD_DIM).astype(F32)[None, :]
    sw = gq * HEAD_DIM
    cin = jnp.arange(sw)
    pin = jnp.stack([((cin[:, None] // HEAD_DIM == g) & (jnp.arange(LANES)[None, :] == cin[:, None] % HEAD_DIM))
                     for g in range(gq)]).astype(BF16)
    pout = jnp.stack([((jnp.arange(LANES)[:, None] < HEAD_DIM)
                       & (cin[None, :] == g * HEAD_DIM + jnp.arange(LANES)[:, None]))
                      for g in range(gq)]).astype(BF16)

    kn = jnp.stack([_tile_heads(k_norm[1], N_KV_HEADS), _tile_heads(k_norm[2], N_KV_HEADS)])
    kn0 = _tile_heads(k_norm[0], N_KV_HEADS)
    pe_t = jnp.tile(cmp_pe, (1, 1, N_KV_HEADS))
    eye2 = jnp.eye(2, dtype=F32)
    w1s = cmp_w1.reshape(2, SEL_BLOCK, HEAD_DIM, -1)
    w1bd = jnp.einsum('ab,zsdj->zsadbj', eye2, w1s).reshape(2, SEL_BLOCK // 2, 2 * 2 * HEAD_DIM, 2 * w1s.shape[-1])
    w1cat = w1bd.astype(BF16)
    eye4 = jnp.eye(N_KV_HEADS, dtype=F32)
    w2bd = jnp.einsum('ab,zje->zajbe', eye4, cmp_w2).reshape(2, N_KV_HEADS * cmp_w2.shape[1], hq).astype(BF16)

    c_all = jnp.concatenate([c_prompt, c_sample], axis=0)
    m_all = c_all.shape[0]
    m_pad = -(-m_all // 8) * 8
    c_all = jnp.pad(c_all, ((0, m_pad - m_all), (0, 0)))
    mods = _mods(c_all, ada_w.reshape(depth * 2, d, 3 * d), ada_b.reshape(depth * 2, 1, 3 * d)).reshape(depth, 2, m_pad, 3 * d)
    mod_kv = _mods(c_all, ada_kv_w[None], ada_kv_b[None, None, :])[0]

    def mod_p(l, j):
        return mods[l, j, :b_p][:, None, :]

    def mod_s_tok(l, j):
        return mods[l, j, b_p:m_all][None]

    def mod_s_seq(l, j):
        return mods[l, j, b_p:m_all][:, None, :]

    tm = min(512, seq)
    tf = d_ff // 2 if (d_ff // 2) % LANES == 0 else d_ff
    tq = 128
    ck = 512
    ck_big = min(2048, seq // 2)

    x = x_prompt
    pool_p = []
    for l in range(n_a):
        x, npool = _pool_layer(x, jnp.zeros((b_p, POOL_HALO, d), F32), mod_p(l, 0), norm_mix[l][None], pool_w_b[l],
                               pool_scale[l][None], 0, tm)
        pool_p.append(npool[:, 1:])
        x = _ffn(x, mod_p(l, 1), norm_ffn[l][None], w_gu_b[l], w_dn_b[l], tm, tf)

    tabs_p = _rope_tables(jnp.arange(seq))
    rows_p, win_p, kst, vs, kwt, vw = _kv_proj(x, mod_kv[:b_p][:, None, :], norm_kv[None], w_kv_b, kn, seg, tabs_p,
                                               place, one_row, tm, True)
    nb_p = seq // SEL_BLOCK
    pt_p = jnp.arange(b_p * (seq // PAGE_SIZE), dtype=jnp.int32).reshape(b_p, seq // PAGE_SIZE)
    raw_p = _compress(pt_p, rows_p.reshape(-1, PAGE_SIZE, 4 * hq), pe_t, w1cat, w2bd)
    tabs_blk_p = _rope_tables((jnp.arange(nb_p) + 1) * SEL_BLOCK - 1)
    kct_p, vcp_p = _cmp_finish(raw_p[0], raw_p[1], kn0, seg, tabs_blk_p, place)
    e_mat = (jnp.arange(seq)[None, :] // SEL_BLOCK == jnp.arange(nb_p)[:, None]).astype(BF16)
    for l in range(n_a, depth):
        j = l - n_a
        q, gates = _q_proj(x, mod_p(l, 0), norm_mix[l][None], w_q_b[j], w_g_b[j], _tile_heads(q_norm[j], N_KV_HEADS),
                           seg, tabs_p, tm, sm * LOG2E, BF16)
        gates = gates.reshape(b_p, seq, N_KV_HEADS, LANES).transpose(0, 2, 1, 3)
        o = _attention(q, gates, kct_p, vcp_p, kst, vs, kwt, vw, e_mat, pin, pout, tq, ck, ck_big)
        x = _o_proj(x, o, mod_p(l, 0), w_o_b[j], tm)
        x = _ffn(x, mod_p(l, 1), norm_ffn[l][None], w_gu_b[l], w_dn_b[l], tm, tf)
    y_prompt = x
    kv_rows_prompt = rows_p.reshape(b_p, seq, 4, N_KV_HEADS, HEAD_DIM)
    win_keep_p = min(WINDOW, seq)
    win_prompt = win_p[:, seq - win_keep_p:].reshape(b_p, win_keep_p, 2, N_KV_HEADS, HEAD_DIM)
    pool_prompt = jnp.stack(pool_p)

    pos_s = past_len
    xs = x_sample
    pool_s = []
    for l in range(n_a):
        pre = jnp.pad(state_pool[l], ((0, 0), (POOL_HALO - state_pool.shape[2], 0), (0, 0)))
        xs, npool = _pool_layer(xs, pre, mod_s_seq(l, 0), norm_mix[l][None], pool_w_b[l], pool_scale[l][None], pos_s, 1)
        pool_s.append(npool[:, 1:])
        xs = _ffn(xs.reshape(1, b_s, d), mod_s_tok(l, 1), norm_ffn[l][None], w_gu_b[l], w_dn_b[l], b_s, tf).reshape(b_s, 1, d)
    xt = xs.reshape(1, b_s, d)
    tabs_s = _rope_tables(jnp.full((b_s,), pos_s))
    rows_s, win_s = _kv_proj(xt, mod_kv[b_p:m_all][None], norm_kv[None], w_kv_b, kn, seg, tabs_s, place, one_row,
                             b_s, False)
    rows_s = rows_s.reshape(b_s, 1, 4 * hq)
    win_new = win_s.reshape(b_s, 1, 2 * hq)

    cache3 = cache_kv.reshape(-1, PAGE_SIZE, 4 * hq)
    raw_s = _compress(page_table, cache3, pe_t, w1cat, w2bd)
    nb_past = past_len // SEL_BLOCK
    nb_s = -(-(past_len + 1) // SEL_BLOCK)
    z_tail = jnp.pad(rows_s[:, 0, :2 * hq].reshape(b_s, 2, N_KV_HEADS, HEAD_DIM).transpose(1, 0, 2, 3)
                     .reshape(2 * b_s * N_KV_HEADS, HEAD_DIM), ((0, 0), (0, (SEL_BLOCK - 1) * HEAD_DIM)))
    raw_tail = _compress_tail(z_tail, cmp_pe.reshape(2, 1, SEL_BLOCK * HEAD_DIM), cmp_w1.astype(BF16),
                              cmp_w2.astype(BF16)).reshape(2, b_s, 1, hq)
    nbp_s = -(-nb_s // LANES) * LANES
    raw_all = jnp.pad(jnp.concatenate([raw_s, raw_tail], axis=2), ((0, 0), (0, 0), (0, nbp_s - nb_s), (0, 0)))
    tabs_blk_s = _rope_tables((jnp.arange(nbp_s) + 1) * SEL_BLOCK - 1)
    kct_s, vcp_s = _cmp_finish(raw_all[0], raw_all[1], kn0, seg, tabs_blk_s, place)

    state_win3 = state_kv_win.reshape(b_s, w_buf, 2 * hq)
    head = jnp.arange(n_heads)
    for l in range(n_a, depth):
        j = l - n_a
        q_s, gates_s = _q_proj(xt, mod_s_tok(l, 0), norm_mix[l][None], w_q_b[j], w_g_b[j],
                               _tile_heads(q_norm[j], N_KV_HEADS), seg, tabs_s, b_s, sm, F32)
        q3 = q_s.reshape(b_s, n_heads, HEAD_DIM)
        q_cmp = jnp.pad(q3, ((0, 0), (0, 0), (0, LANES - HEAD_DIM)))
        odd = ((head // gq) % 2 == 1)[None, :, None]
        q_sel = jnp.where(odd, jnp.pad(q3, ((0, 0), (0, 0), (HEAD_DIM, 0))), q_cmp)
        q_win = (q3[:, :, None, :] * (jnp.arange(N_KV_HEADS)[None, None, :, None] == (head // gq)[None, :, None, None])
                 ).reshape(b_s, n_heads, hq)
        g3 = gates_s.reshape(b_s, N_KV_HEADS, LANES)[:, :, :3 * gq].reshape(b_s, n_heads, 3)
        g3 = jnp.pad(g3, ((0, 0), (0, 0), (0, LANES - 3)))
        o_c, idx = _decode_a(q_cmp, kct_s, vcp_s, pos_s, nb_s, gq)
        o_s = _decode_b(idx.reshape(b_s, N_KV_HEADS * N_SEL), page_table, q_sel, q_win, g3, o_c, cache3, rows_s,
                        state_win3, win_new, pos_s, gq)
        o_tok = o_s[:, :, :HEAD_DIM].reshape(1, b_s, d)
        xt = _o_proj(xt, o_tok, mod_s_tok(l, 0), w_o_b[j], b_s)
        xt = _ffn(xt, mod_s_tok(l, 1), norm_ffn[l][None], w_gu_b[l], w_dn_b[l], b_s, tf)
    y_sample = xt.reshape(b_s, 1, d)
    kv_rows_sample = rows_s.reshape(b_s, 1, 4, N_KV_HEADS, HEAD_DIM)
    win_full_s = jnp.concatenate([state_win3, win_new], axis=1)
    win_sample = win_full_s[:, win_full_s.shape[1] - w_buf:].reshape(b_s, w_buf, 2, N_KV_HEADS, HEAD_DIM)
    pool_sample = jnp.stack(pool_s)
    return (y_prompt, y_sample, kv_rows_prompt, kv_rows_sample, win_prompt, win_sample, pool_prompt, pool_sample)
```

```python
import functools
import math

import jax
import jax.numpy as jnp
from jax import lax
from jax.experimental import pallas as pl
from jax.experimental.pallas import tpu as pltpu

F32 = jnp.float32
BF16 = jnp.bfloat16

POOL_WINDOWS = (2, 4, 8, 16)
POOL_HALO = 16
HEAD_DIM = 64
N_KV_HEADS = 4
ROT_DIM = 16
ROPE_THETA = 500000.0
SEL_BLOCK = 64
N_SEL = 16
N_FORCED = 3
WINDOW = 512
PAGE_SIZE = 128
FORCE_SCORE = 1.0e4
EPS = 1e-6
NEG = -float(2 ** 30)
LOG2E = 1.4426950408889634
LANES = 128
TILE_ROW_STRIDE = 72
VMEM_LIMIT = 56 * 1024 * 1024
NT_DIMS = (((1,), (1,)), ((), ()))


def _cparams(sem):
    return pltpu.CompilerParams(dimension_semantics=sem, vmem_limit_bytes=VMEM_LIMIT)


def _dot(a, b):
    return jnp.dot(a, b, preferred_element_type=F32)


def _dot_nt(a, b):
    return lax.dot_general(a, b, NT_DIMS, preferred_element_type=F32)


def _split(a):
    hi = a.astype(BF16)
    lo = (a - hi.astype(F32)).astype(BF16)
    return hi, lo


def _adaln(x, g, shift, scale):
    ms = jnp.mean(x * x, axis=-1, keepdims=True)
    return x * lax.rsqrt(ms + EPS) * g * (1.0 + scale) + shift


def _head_rms(x, gain, seg):
    hi, lo = _split(x * x)
    ss = _dot(hi, seg) + _dot(lo, seg)
    return x * lax.rsqrt(ss * (1.0 / HEAD_DIM) + EPS) * gain


def _rope(x, cos, sin_up, sin_dn):
    outs = []
    for a in range(x.shape[1] // LANES):
        xa = x[:, a * LANES:(a + 1) * LANES]
        up = pltpu.roll(xa, LANES - ROT_DIM // 2, 1)
        dn = pltpu.roll(xa, ROT_DIM // 2, 1)
        outs.append(xa * cos + up * sin_up + dn * sin_dn)
    return outs[0] if len(outs) == 1 else jnp.concatenate(outs, axis=1)


def _gelu_tanh(x):
    return x * (0.5 * (1.0 + jnp.tanh(math.sqrt(2.0 / math.pi) * (x + 0.044715 * (x * x * x)))))


def _mods_kernel(c_ref, w_ref, b_ref, o_ref):
    ch, cl = _split(c_ref[...])
    wh, wl = _split(w_ref[0])
    o_ref[0] = _dot(ch, wh) + _dot(ch, wl) + _dot(cl, wh) + b_ref[0]


def _mods(c_all, w, b, tn=1024):
    n_l, d, n = w.shape
    m = c_all.shape[0]
    return pl.pallas_call(
        _mods_kernel,
        out_shape=jax.ShapeDtypeStruct((n_l, m, n), F32),
        grid=(n_l, n // tn),
        in_specs=[pl.BlockSpec((m, d), lambda l, j: (0, 0)),
                  pl.BlockSpec((1, d, tn), lambda l, j: (l, 0, j)),
                  pl.BlockSpec((1, 1, tn), lambda l, j: (l, 0, j))],
        out_specs=pl.BlockSpec((1, m, tn), lambda l, j: (l, 0, j)),
        compiler_params=_cparams(("parallel", "parallel")),
        name="mods",
    )(c_all, w, b)


def _pool_kernel(x_ref, pre_ref, mod_ref, g_ref, pw_ref, ps_ref, o_ref, np_ref, hb_ref, *, tt, pos0, d):
    i = pl.program_id(1)
    x = x_ref[0]
    mod = mod_ref[0]
    h = _adaln(x, g_ref[...], mod[:, :d], mod[:, d:2 * d])

    @pl.when(i == 0)
    def _():
        hb_ref[0:POOL_HALO, :] = pre_ref[0]

    hb_ref[POOL_HALO:POOL_HALO + tt, :] = h
    pos = pos0 + i * tt + lax.broadcasted_iota(jnp.int32, (tt, 1), 0)
    pg = d // len(POOL_WINDOWS)
    ys = []
    for gi, w in enumerate(POOL_WINDOWS):
        c0 = gi * pg
        hg = h[:, c0:c0 + pg]
        s = hg
        for j in range(1, w):
            s = s + hb_ref[POOL_HALO - j:POOL_HALO - j + tt, c0:c0 + pg]
        cnt = jnp.minimum(w, pos + 1).astype(F32)
        pooled = s / cnt - hg
        ys.append(_dot(pooled.astype(BF16), pw_ref[gi]))
    y = jnp.concatenate(ys, axis=1) * ps_ref[...]
    o_ref[0] = x + mod[:, 2 * d:] * y
    last = hb_ref[tt:tt + POOL_HALO, :]
    np_ref[0] = last
    hb_ref[0:POOL_HALO, :] = last


def _pool_layer(x, prefix16, mod, g, pw, ps, pos0, tt):
    b, t, d = x.shape
    kern = functools.partial(_pool_kernel, tt=tt, pos0=pos0, d=d)
    pg = d // len(POOL_WINDOWS)
    return pl.pallas_call(
        kern,
        out_shape=(jax.ShapeDtypeStruct((b, t, d), F32), jax.ShapeDtypeStruct((b, POOL_HALO, d), F32)),
        grid=(b, t // tt),
        in_specs=[pl.BlockSpec((1, tt, d), lambda bi, i: (bi, i, 0)),
                  pl.BlockSpec((1, POOL_HALO, d), lambda bi, i: (bi, 0, 0)),
                  pl.BlockSpec((1, 1, 3 * d), lambda bi, i: (bi, 0, 0)),
                  pl.BlockSpec((1, d), lambda bi, i: (0, 0)),
                  pl.BlockSpec((len(POOL_WINDOWS), pg, pg), lambda bi, i: (0, 0, 0)),
                  pl.BlockSpec((1, d), lambda bi, i: (0, 0))],
        out_specs=(pl.BlockSpec((1, tt, d), lambda bi, i: (bi, i, 0)),
                   pl.BlockSpec((1, POOL_HALO, d), lambda bi, i: (bi, 0, 0))),
        scratch_shapes=[pltpu.VMEM((POOL_HALO + max(tt, 8), d), F32)],
        compiler_params=_cparams(("parallel", "arbitrary")),
        name="pool_layer",
    )(x, prefix16, mod, g, pw, ps)


def _ffn_kernel(x_ref, mod_ref, g_ref, wg_ref, wu_ref, wd_ref, o_ref, h_ref, acc_ref, *, d):
    f = pl.program_id(2)

    @pl.when(f == 0)
    def _():
        mod = mod_ref[0]
        h_ref[...] = _adaln(x_ref[0], g_ref[...], mod[:, :d], mod[:, d:2 * d]).astype(BF16)
        acc_ref[...] = jnp.zeros_like(acc_ref)

    hb = h_ref[...]
    g = _dot(hb, wg_ref[...])
    u = _dot(hb, wu_ref[...])
    a = (g * (1.0 / (1.0 + jnp.exp(-g)))) * u
    acc_ref[...] += _dot(a.astype(BF16), wd_ref[...])

    @pl.when(f == pl.num_programs(2) - 1)
    def _():
        o_ref[0] = x_ref[0] + mod_ref[0][:, 2 * d:] * acc_ref[...]


def _ffn(x, mod, g, w_gu, w_dn, tm, tf):
    b, t, d = x.shape
    rm = mod.shape[1]
    ff = w_dn.shape[0]
    nf = ff // tf
    kern = functools.partial(_ffn_kernel, d=d)
    return pl.pallas_call(
        kern,
        out_shape=jax.ShapeDtypeStruct((b, t, d), F32),
        grid=(b, t // tm, nf),
        in_specs=[pl.BlockSpec((1, tm, d), lambda bi, i, f: (bi, i, 0)),
                  pl.BlockSpec((1, rm, 3 * d), lambda bi, i, f: (bi, 0, 0)),
                  pl.BlockSpec((1, d), lambda bi, i, f: (0, 0)),
                  pl.BlockSpec((d, tf), lambda bi, i, f: (0, f)),
                  pl.BlockSpec((d, tf), lambda bi, i, f: (0, nf + f)),
                  pl.BlockSpec((tf, d), lambda bi, i, f: (f, 0))],
        out_specs=pl.BlockSpec((1, tm, d), lambda bi, i, f: (bi, i, 0)),
        scratch_shapes=[pltpu.VMEM((tm, d), BF16), pltpu.VMEM((tm, d), F32)],
        compiler_params=_cparams(("parallel", "parallel", "arbitrary")),
        name="ffn",
    )(x, mod, g, w_gu, w_gu, w_dn)


def _kv_kernel(x_ref, mod_ref, g_ref, w_ref, kn_ref, seg_ref, cos_ref, su_ref, sd_ref, place_ref, one_ref,
               *out_refs, d, transposed):
    hq = N_KV_HEADS * HEAD_DIM
    mod = mod_ref[0]
    h = _adaln(x_ref[0], g_ref[...], mod[:, :d], mod[:, d:2 * d])
    proj = _dot(h.astype(BF16), w_ref[...])
    seg = seg_ref[...]
    cos, su, sd = cos_ref[...], su_ref[...], sd_ref[...]
    k_sel = _rope(_head_rms(proj[:, 2 * hq:3 * hq], kn_ref[0], seg), cos, su, sd)
    k_win = _rope(_head_rms(proj[:, 4 * hq:5 * hq], kn_ref[1], seg), cos, su, sd)
    v_sel = proj[:, 3 * hq:4 * hq]
    v_win = proj[:, 5 * hq:6 * hq]
    if not transposed:
        rows_ref, win_ref = out_refs
        rows_ref[0] = jnp.concatenate([proj[:, :2 * hq], k_sel, v_sel], axis=1)
        win_ref[0] = jnp.concatenate([k_win, v_win], axis=1)
        return
    rows_t_ref, win_t_ref, kst_ref, vs_ref, kwt_ref, vw_ref = out_refs
    tm = k_sel.shape[0]
    zeros = jnp.zeros((HEAD_DIM, tm), BF16)
    k_sel_t = k_sel.T
    k_win_t = k_win.T
    for t_ref, slabs in ((rows_t_ref, (proj[:, :hq].T, proj[:, hq:2 * hq].T, k_sel_t, v_sel.T)),
                         (win_t_ref, (k_win_t, v_win.T))):
        for sl, slab in enumerate(slabs):
            for k in range(N_KV_HEADS):
                t_ref[0, sl, k] = slab[k * HEAD_DIM:(k + 1) * HEAD_DIM, :]
    for kt_ref, kt32 in ((kst_ref, k_sel_t), (kwt_ref, k_win_t)):
        kt = kt32.astype(BF16)
        for k in range(N_KV_HEADS):
            kt_ref[0, k, 0:HEAD_DIM, :] = kt[k * HEAD_DIM:(k + 1) * HEAD_DIM, :]
            kt_ref[0, k, HEAD_DIM:2 * HEAD_DIM, :] = zeros
    for va_ref, vv in ((vs_ref, v_sel), (vw_ref, v_win)):
        va = (_dot(vv.astype(BF16), place_ref[...]) + one_ref[...]).astype(BF16)
        for k in range(N_KV_HEADS):
            va_ref[0, k] = va[:, k * LANES:(k + 1) * LANES]


def _kv_proj(x, mod, g, w_kv, kn, seg, tabs, place, one_row, tm, transposed):
    b, t, d = x.shape
    rm = mod.shape[1]
    hq = N_KV_HEADS * HEAD_DIM
    kern = functools.partial(_kv_kernel, d=d, transposed=transposed)
    const2 = lambda bi, i: (0, 0)
    tab_spec = pl.BlockSpec((tm, LANES), lambda bi, i: (i, 0))
    if transposed:
        kt_shape = jax.ShapeDtypeStruct((b, N_KV_HEADS, 2 * HEAD_DIM, t), BF16)
        va_shape = jax.ShapeDtypeStruct((b, N_KV_HEADS, t, LANES), BF16)
        kt_spec = pl.BlockSpec((1, N_KV_HEADS, 2 * HEAD_DIM, tm), lambda bi, i: (bi, 0, 0, i))
        va_spec = pl.BlockSpec((1, N_KV_HEADS, tm, LANES), lambda bi, i: (bi, 0, i, 0))
        out_shape = [jax.ShapeDtypeStruct((b, 4, N_KV_HEADS, HEAD_DIM, t), F32),
                     jax.ShapeDtypeStruct((b, 2, N_KV_HEADS, HEAD_DIM, t), F32), kt_shape, va_shape, kt_shape, va_shape]
        out_specs = [pl.BlockSpec((1, 4, N_KV_HEADS, HEAD_DIM, tm), lambda bi, i: (bi, 0, 0, 0, i)),
                     pl.BlockSpec((1, 2, N_KV_HEADS, HEAD_DIM, tm), lambda bi, i: (bi, 0, 0, 0, i)),
                     kt_spec, va_spec, kt_spec, va_spec]
    else:
        out_shape = [jax.ShapeDtypeStruct((b, t, 4 * hq), F32), jax.ShapeDtypeStruct((b, t, 2 * hq), F32)]
        out_specs = [pl.BlockSpec((1, tm, 4 * hq), lambda bi, i: (bi, i, 0)),
                     pl.BlockSpec((1, tm, 2 * hq), lambda bi, i: (bi, i, 0))]
    return pl.pallas_call(
        kern,
        out_shape=tuple(out_shape),
        grid=(b, t // tm),
        in_specs=[pl.BlockSpec((1, tm, d), lambda bi, i: (bi, i, 0)),
                  pl.BlockSpec((1, rm, 2 * d), lambda bi, i: (bi, 0, 0)),
                  pl.BlockSpec((1, d), const2),
                  pl.BlockSpec((d, 6 * hq), const2),
                  pl.BlockSpec((2, 1, hq), lambda bi, i: (0, 0, 0)),
                  pl.BlockSpec((hq, hq), const2),
                  tab_spec, tab_spec, tab_spec,
                  pl.BlockSpec((hq, N_KV_HEADS * LANES), const2),
                  pl.BlockSpec((1, N_KV_HEADS * LANES), const2)],
        out_specs=tuple(out_specs),
        compiler_params=_cparams(("parallel", "parallel")),
        name="kv_proj",
    )(x, mod, g, w_kv, kn, seg, *tabs, place, one_row)


def _compress_kernel(pt_ref, src_ref, pe_ref, w1_ref, w2_ref, o_ref, buf_ref, sem_ref, *, n_pages, n_b, paged):
    n_tiles = n_pages * N_KV_HEADS
    sl = pl.program_id(0)
    b = pl.program_id(1)
    step = sl * n_b + b
    cur = step % 2

    def tile_copy(sl_, b_, slot_, p, h):
        if paged:
            src = src_ref.at[pt_ref[b_, p], sl_, h]
        else:
            src = src_ref.at[b_, sl_, h, :, pl.ds(p * PAGE_SIZE, PAGE_SIZE)]
        ti = p * N_KV_HEADS + h
        return pltpu.make_async_copy(src, buf_ref.at[slot_, pl.ds(ti * TILE_ROW_STRIDE, HEAD_DIM), :],
                                     sem_ref.at[slot_])

    def issue(sl_, b_, slot_):
        for p in range(n_pages):
            for h in range(N_KV_HEADS):
                tile_copy(sl_, b_, slot_, p, h).start()

    @pl.when(step == 0)
    def _():
        issue(sl, b, 0)

    @pl.when(step + 1 < 2 * n_b)
    def _():
        nxt = step + 1
        issue(nxt // n_b, nxt % n_b, 1 - cur)

    for p in range(n_pages):
        for h in range(N_KV_HEADS):
            tile_copy(sl, b, cur, p, h).wait()

    pe = pe_ref[0]
    hid = jnp.zeros((n_tiles, 2 * LANES), F32)
    for dp in range(HEAD_DIM // 2):
        xs = [buf_ref[cur, pl.ds(dd, n_tiles, stride=TILE_ROW_STRIDE), :] + pe[dd:dd + 1, :]
              for dd in (2 * dp, 2 * dp + 1)]
        hid = hid + _dot(jnp.concatenate(xs, axis=1).astype(BF16), w1_ref[0, dp])
    o_ref[0, 0] = _dot(_gelu_tanh(hid).astype(BF16), w2_ref[0])


def _compress(page_table, src, pe_t, w1cat, w2bd, paged):
    n_b, n_pages = page_table.shape
    n_tiles = n_pages * N_KV_HEADS
    kern = functools.partial(_compress_kernel, n_pages=n_pages, n_b=n_b, paged=paged)
    grid_spec = pltpu.PrefetchScalarGridSpec(
        num_scalar_prefetch=1,
        grid=(2, n_b),
        in_specs=[pl.BlockSpec(memory_space=pl.ANY),
                  pl.BlockSpec((1, HEAD_DIM, LANES), lambda s, b, pt: (s, 0, 0)),
                  pl.BlockSpec((1, HEAD_DIM // 2, 2 * LANES, 2 * LANES), lambda s, b, pt: (s, 0, 0, 0)),
                  pl.BlockSpec((1, 2 * LANES, LANES), lambda s, b, pt: (s, 0, 0))],
        out_specs=pl.BlockSpec((1, 1, n_tiles, LANES), lambda s, b, pt: (s, b, 0, 0)),
        scratch_shapes=[pltpu.VMEM((2, n_tiles * TILE_ROW_STRIDE, LANES), F32),
                        pltpu.SemaphoreType.DMA((2,))])
    raw = pl.pallas_call(
        kern,
        out_shape=jax.ShapeDtypeStruct((2, n_b, n_tiles, LANES), F32),
        grid_spec=grid_spec,
        compiler_params=_cparams(("arbitrary", "arbitrary")),
        name="compress",
    )(page_table, src, pe_t, w1cat, w2bd)
    raw = raw.reshape(2, n_b, n_pages, N_KV_HEADS, 2, HEAD_DIM).transpose(0, 1, 2, 4, 3, 5)
    return raw.reshape(2, n_b, 2 * n_pages, N_KV_HEADS * HEAD_DIM)


def _tail_kernel(z_ref, pe_ref, w1_ref, w2_ref, o_ref):
    z = z_ref[...] + pe_ref[0]
    hid = _gelu_tanh(_dot(z.astype(BF16), w1_ref[0]))
    o_ref[0] = _dot(hid.astype(BF16), w2_ref[0])


def _compress_tail(z, pe_flat, w1, w2):
    m, kdim = z.shape[0] // 2, z.shape[1]
    hid = w1.shape[2]
    return pl.pallas_call(
        _tail_kernel,
        out_shape=jax.ShapeDtypeStruct((2, m, HEAD_DIM), F32),
        grid=(2,),
        in_specs=[pl.BlockSpec((m, kdim), lambda s: (s, 0)),
                  pl.BlockSpec((1, 1, kdim), lambda s: (s, 0, 0)),
                  pl.BlockSpec((1, kdim, hid), lambda s: (s, 0, 0)),
                  pl.BlockSpec((1, hid, HEAD_DIM), lambda s: (s, 0, 0))],
        out_specs=pl.BlockSpec((1, m, HEAD_DIM), lambda s: (s, 0, 0)),
        compiler_params=_cparams(("parallel",)),
        name="compress_tail",
    )(z, pe_flat, w1, w2)


def _cmp_finish_kernel(kc_ref, vc_ref, kn_ref, seg_ref, cos_ref, su_ref, sd_ref, place_ref, kt_ref, vp_ref):
    kc = _rope(_head_rms(kc_ref[0], kn_ref[...], seg_ref[...]), cos_ref[...], su_ref[...], sd_ref[...])
    kt = kc.T.astype(BF16)
    nbp = kc.shape[0]
    zeros = jnp.zeros((HEAD_DIM, nbp), BF16)
    vp = _dot(vc_ref[0].astype(BF16), place_ref[...]).astype(BF16)
    for k in range(N_KV_HEADS):
        kt_ref[0, k, 0:HEAD_DIM, :] = kt[k * HEAD_DIM:(k + 1) * HEAD_DIM, :]
        kt_ref[0, k, HEAD_DIM:2 * HEAD_DIM, :] = zeros
        vp_ref[0, k] = vp[:, k * LANES:(k + 1) * LANES]


def _cmp_finish(kc_raw, vc_raw, kn0, seg, tabs, place):
    b, nbp, hq = kc_raw.shape
    const2 = lambda bi: (0, 0)
    tab_spec = pl.BlockSpec((nbp, LANES), const2)
    return pl.pallas_call(
        _cmp_finish_kernel,
        out_shape=(jax.ShapeDtypeStruct((b, N_KV_HEADS, 2 * HEAD_DIM, nbp), BF16),
                   jax.ShapeDtypeStruct((b, N_KV_HEADS, nbp, LANES), BF16)),
        grid=(b,),
        in_specs=[pl.BlockSpec((1, nbp, hq), lambda bi: (bi, 0, 0)),
                  pl.BlockSpec((1, nbp, hq), lambda bi: (bi, 0, 0)),
                  pl.BlockSpec((1, hq), const2),
                  pl.BlockSpec((hq, hq), const2),
                  tab_spec, tab_spec, tab_spec,
                  pl.BlockSpec((hq, N_KV_HEADS * LANES), const2)],
        out_specs=(pl.BlockSpec((1, N_KV_HEADS, 2 * HEAD_DIM, nbp), lambda bi: (bi, 0, 0, 0)),
                   pl.BlockSpec((1, N_KV_HEADS, nbp, LANES), lambda bi: (bi, 0, 0, 0))),
        compiler_params=_cparams(("parallel",)),
        name="cmp_finish",
    )(kc_raw, vc_raw, kn0, seg, *tabs, place)


def _q_kernel(x_ref, mod_ref, g_ref, wq_ref, wg_ref, qn_ref, seg_ref, cos_ref, su_ref, sd_ref,
              q_ref, gate_ref, *, d, q_scale):
    hq = N_KV_HEADS * HEAD_DIM
    mod = mod_ref[0]
    hb = _adaln(x_ref[0], g_ref[...], mod[:, :d], mod[:, d:2 * d]).astype(BF16)
    q = _dot(hb, wq_ref[...])
    seg = seg_ref[...]
    cos, su, sd = cos_ref[...], su_ref[...], sd_ref[...]
    for c in range(d // hq):
        qc = _rope(_head_rms(q[:, c * hq:(c + 1) * hq], qn_ref[...], seg), cos, su, sd) * q_scale
        q_ref[0, :, c * hq:(c + 1) * hq] = qc.astype(q_ref.dtype)
    gates = 1.0 / (1.0 + jnp.exp(-_dot(hb, wg_ref[...])))
    for k in range(N_KV_HEADS):
        gate_ref[0, k] = gates[:, k * LANES:(k + 1) * LANES]


def _q_proj(x, mod, g, w_q, w_g, qn, seg, tabs, tm, q_scale, q_dtype):
    b, t, d = x.shape
    rm = mod.shape[1]
    hq = N_KV_HEADS * HEAD_DIM
    ng = w_g.shape[1]
    kern = functools.partial(_q_kernel, d=d, q_scale=q_scale)
    const2 = lambda bi, i: (0, 0)
    tab_spec = pl.BlockSpec((tm, LANES), lambda bi, i: (i, 0))
    return pl.pallas_call(
        kern,
        out_shape=(jax.ShapeDtypeStruct((b, t, d), q_dtype),
                   jax.ShapeDtypeStruct((b, N_KV_HEADS, t, LANES), F32)),
        grid=(b, t // tm),
        in_specs=[pl.BlockSpec((1, tm, d), lambda bi, i: (bi, i, 0)),
                  pl.BlockSpec((1, rm, 3 * d), lambda bi, i: (bi, 0, 0)),
                  pl.BlockSpec((1, d), const2),
                  pl.BlockSpec((d, d), const2),
                  pl.BlockSpec((d, ng), const2),
                  pl.BlockSpec((1, hq), const2),
                  pl.BlockSpec((hq, hq), const2),
                  tab_spec, tab_spec, tab_spec],
        out_specs=(pl.BlockSpec((1, tm, d), lambda bi, i: (bi, i, 0)),
                   pl.BlockSpec((1, N_KV_HEADS, tm, LANES), lambda bi, i: (bi, 0, i, 0))),
        compiler_params=_cparams(("parallel", "parallel")),
        name="q_proj",
    )(x, mod, g, w_q, w_g, qn, seg, *tabs)


def _oproj_kernel(x_ref, o_ref, mod_ref, w_ref, y_ref, *, d):
    y_ref[0] = x_ref[0] + mod_ref[0][:, 2 * d:] * _dot(o_ref[0].astype(BF16), w_ref[...])


def _o_proj(x, o, mod, w_o, tm):
    b, t, d = x.shape
    rm = mod.shape[1]
    kern = functools.partial(_oproj_kernel, d=d)
    return pl.pallas_call(
        kern,
        out_shape=jax.ShapeDtypeStruct((b, t, d), F32),
        grid=(b, t // tm),
        in_specs=[pl.BlockSpec((1, tm, d), lambda bi, i: (bi, i, 0)),
                  pl.BlockSpec((1, tm, d), lambda bi, i: (bi, i, 0)),
                  pl.BlockSpec((1, rm, 3 * d), lambda bi, i: (bi, 0, 0)),
                  pl.BlockSpec((d, d), lambda bi, i: (0, 0))],
        out_specs=pl.BlockSpec((1, tm, d), lambda bi, i: (bi, i, 0)),
        compiler_params=_cparams(("parallel", "parallel")),
        name="o_proj",
    )(x, o, mod, w_o)


def _attn_kernel(q_ref, gate_ref, kct_ref, vcp_ref, kst_ref, vs_ref, kwt_ref, vw_ref, e_ref, pin_ref, pout_ref,
                 dmask_ref, wmask_ref, o_ref, qa_ref, m_ref, acc_ref, accw_ref, s_ref, *, tq, ck, ck_big, nb):
    gq = q_ref.shape[2] // HEAD_DIM
    rows = gq * tq
    i = pl.program_id(2)
    t0 = i * tq

    qslab = q_ref[0]
    qp = jnp.concatenate([_dot(qslab, pin_ref[g]).astype(BF16) for g in range(gq)], axis=0)
    pos_r = t0 + lax.broadcasted_iota(jnp.int32, (rows, 1), 0) % tq

    wl = WINDOW + tq
    w0 = pl.multiple_of(jnp.maximum(t0 - WINDOW, 0), LANES)
    s_w = _dot(qp, kwt_ref[0, 0, :, pl.ds(w0, wl)]) + jnp.concatenate([wmask_ref[0]] * gq, axis=0)
    p_w = jnp.exp2(s_w - jnp.max(s_w, axis=1, keepdims=True))
    accw_ref[...] = _dot(p_w.astype(BF16), vw_ref[0, 0, pl.ds(w0, wl), :])

    s_c = _dot(qp, kct_ref[0, 0])
    blk = lax.broadcasted_iota(jnp.int32, (rows, nb), 1)
    ok_c = (blk + 1) * SEL_BLOCK - 1 <= pos_r
    m_c = jnp.max(jnp.where(ok_c, s_c, -jnp.inf), axis=1, keepdims=True)
    m_c = jnp.where(m_c == -jnp.inf, 0.0, m_c)
    p_c = jnp.where(ok_c, jnp.exp2(s_c - m_c), 0.0)
    p_c = p_c / jnp.maximum(jnp.sum(p_c, axis=1, keepdims=True), 1e-30)
    o_c = _dot(p_c.astype(BF16), vcp_ref[0, 0])

    imp = p_c[0:tq]
    for g in range(1, gq):
        imp = imp + p_c[g * tq:(g + 1) * tq]
    imp_t = imp.T
    n_io = lax.broadcasted_iota(jnp.int32, (nb, tq), 0)
    cur = (t0 + lax.broadcasted_iota(jnp.int32, (nb, tq), 1)) // SEL_BLOCK
    forced = (n_io == 0) | (n_io == cur) | (n_io == cur - 1)
    v = jnp.where(forced, -2.0, jnp.where(n_io > cur, -1.0, imp_t))
    sel = jnp.where(forced, 1.0, 0.0)
    for _ in range(N_SEL - N_FORCED):
        mx = jnp.max(v, axis=0, keepdims=True)
        first = jnp.min(jnp.where(v == mx, n_io, nb), axis=0, keepdims=True)
        pick = n_io == first
        sel = jnp.where(pick, 1.0, sel)
        v = jnp.where(pick, -2.0, v)
    bias_t = jnp.where((sel > 0.0) & (n_io <= cur), 0.0, NEG)
    bias = bias_t.T.astype(BF16)
    qa_ref[...] = jnp.concatenate([jnp.concatenate([bias] * gq, axis=0), qp], axis=1)

    m_ref[...] = jnp.full(m_ref.shape, NEG * 4.0, F32)

    def score_chunk(k0, width, causal):
        kaug = jnp.concatenate([e_ref[:, pl.ds(k0, width)], kst_ref[0, 0, :, pl.ds(k0, width)]], axis=0)
        s = _dot(qa_ref[...], kaug)
        if causal:
            s = s + jnp.concatenate([dmask_ref[0]] * gq, axis=0)
        s_ref[:, pl.ds(k0, width)] = s
        m = m_ref[...]
        for a in range(width // LANES):
            m = jnp.maximum(m, s[:, a * LANES:(a + 1) * LANES])
        m_ref[...] = m

    def value_chunk(k0, width):
        mb = m_ref[...]
        p = jnp.exp2(s_ref[:, pl.ds(k0, width)] - jnp.concatenate([mb] * (width // LANES), axis=1))
        acc_ref[...] += _dot(p.astype(BF16), vs_ref[0, 0, pl.ds(k0, width), :])

    c_last = t0 // ck
    per_big = ck_big // ck
    n_big = c_last // per_big
    n_small = c_last - n_big * per_big
    small0 = n_big * ck_big

    def loop(n, fn):
        def body(c, carry):
            fn(c)
            return carry
        lax.fori_loop(0, n, body, 0)

    loop(n_big, lambda c: score_chunk(pl.multiple_of(c * ck_big, ck_big), ck_big, False))
    loop(n_small, lambda c: score_chunk(pl.multiple_of(small0 + c * ck, ck), ck, False))
    score_chunk(pl.multiple_of(c_last * ck, ck), ck, True)

    m_row = jnp.max(m_ref[...], axis=1, keepdims=True)
    m_ref[...] = jnp.broadcast_to(m_row, m_ref.shape)
    acc_ref[...] = jnp.zeros_like(acc_ref)
    loop(n_big, lambda c: value_chunk(pl.multiple_of(c * ck_big, ck_big), ck_big))
    loop(n_small + 1, lambda c: value_chunk(pl.multiple_of(small0 + c * ck, ck), ck))
    acc = acc_ref[...]
    o_s = acc / acc[:, HEAD_DIM:HEAD_DIM + 1]
    acc_w = accw_ref[...]
    o_w = acc_w / acc_w[:, HEAD_DIM:HEAD_DIM + 1]

    gates = gate_ref[0, 0]
    out = jnp.zeros((tq, gq * HEAD_DIM), F32)
    for g in range(gq):
        r = slice(g * tq, (g + 1) * tq)
        mix = (gates[:, 3 * g:3 * g + 1] * o_c[r] + gates[:, 3 * g + 1:3 * g + 2] * o_s[r]
               + gates[:, 3 * g + 2:3 * g + 3] * o_w[r])
        out = out + _dot(mix.astype(BF16), pout_ref[g])
    o_ref[0] = out.astype(o_ref.dtype)


def _attention(q, gates, kct, vcp, kst, vs, kwt, vw, e_mat, pin, pout, tq, ck, ck_big):
    b, t, d = q.shape
    nb = kct.shape[3]
    gq = d // (N_KV_HEADS * HEAD_DIM)
    sw = gq * HEAD_DIM
    rows = gq * tq
    kern = functools.partial(_attn_kernel, tq=tq, ck=ck, ck_big=ck_big, nb=nb)
    per_head4 = lambda bi, k, i: (bi, k, 0, 0)
    tt = jnp.arange(tq)[None, :, None]
    n_phase = ck // tq
    jd = jnp.arange(ck)[None, None, :]
    dmask = jnp.where(jd <= jnp.arange(n_phase)[:, None, None] * tq + tt, 0.0, NEG).astype(F32)
    wl = WINDOW + tq
    n_early = WINDOW // tq
    jw = jnp.arange(wl)[None, None, :]
    pos_e = jnp.arange(n_early)[:, None, None] * tq + tt
    early = (jw <= pos_e) & (pos_e - jw < WINDOW)
    steady = (jw > tt) & (jw <= tt + WINDOW)
    wmask = jnp.where(jnp.concatenate([early, steady], axis=0), 0.0, NEG).astype(F32)
    return pl.pallas_call(
        kern,
        out_shape=jax.ShapeDtypeStruct((b, t, d), BF16),
        grid=(b, N_KV_HEADS, t // tq),
        in_specs=[pl.BlockSpec((1, tq, sw), lambda bi, k, i: (bi, i, k)),
                  pl.BlockSpec((1, 1, tq, LANES), lambda bi, k, i: (bi, k, i, 0)),
                  pl.BlockSpec((1, 1, 2 * HEAD_DIM, nb), per_head4),
                  pl.BlockSpec((1, 1, nb, LANES), per_head4),
                  pl.BlockSpec((1, 1, 2 * HEAD_DIM, t), per_head4),
                  pl.BlockSpec((1, 1, t, LANES), per_head4),
                  pl.BlockSpec((1, 1, 2 * HEAD_DIM, t), per_head4),
                  pl.BlockSpec((1, 1, t, LANES), per_head4),
                  pl.BlockSpec((nb, t), lambda bi, k, i: (0, 0)),
                  pl.BlockSpec((gq, sw, LANES), lambda bi, k, i: (0, 0, 0)),
                  pl.BlockSpec((gq, LANES, sw), lambda bi, k, i: (0, 0, 0)),
                  pl.BlockSpec((1, tq, ck), lambda bi, k, i: (i % n_phase, 0, 0)),
                  pl.BlockSpec((1, tq, wl), lambda bi, k, i: (jnp.minimum(i, n_early), 0, 0))],
        out_specs=pl.BlockSpec((1, tq, sw), lambda bi, k, i: (bi, i, k)),
        scratch_shapes=[pltpu.VMEM((rows, nb + LANES), BF16),
                        pltpu.VMEM((rows, LANES), F32),
                        pltpu.VMEM((rows, LANES), F32),
                        pltpu.VMEM((rows, LANES), F32),
                        pltpu.VMEM((rows, t), F32)],
        compiler_params=_cparams(("parallel", "parallel", "arbitrary")),
        name="nsa_attention",
    )(q, gates, kct, vcp, kst, vs, kwt, vw, e_mat, pin, pout, dmask, wmask)


def _row_kvh(shape, gq):
    return lax.broadcasted_iota(jnp.int32, shape, 0) // gq


def _dec_a_kernel(q_ref, kct_ref, vcp_ref, oc_ref, idx_ref, *, pos, nb, gq, bb):
    nh = q_ref.shape[1]
    nbp = kct_ref.shape[3]
    rk = _row_kvh((nh, nbp), gq)
    rk_o = _row_kvh((nh, LANES), gq)
    blk = lax.broadcasted_iota(jnp.int32, (nh, nbp), 1)
    ok = ((blk + 1) * SEL_BLOCK - 1 <= pos) & (blk < nb)
    imps = []
    for lb in range(bb):
        qb = q_ref[lb].astype(BF16)
        s = jnp.zeros((nh, nbp), F32)
        for k in range(N_KV_HEADS):
            s = jnp.where(rk == k, _dot(qb, kct_ref[lb, k]), s)
        m = jnp.max(jnp.where(ok, s, -jnp.inf), axis=1, keepdims=True)
        m = jnp.where(m == -jnp.inf, 0.0, m)
        p = jnp.where(ok, jnp.exp(s - m), 0.0)
        p = p / jnp.maximum(jnp.sum(p, axis=1, keepdims=True), 1e-30)
        pb = p.astype(BF16)
        o_c = jnp.zeros((nh, LANES), F32)
        for k in range(N_KV_HEADS):
            o_c = jnp.where(rk_o == k, _dot(pb, vcp_ref[lb, k]), o_c)
        oc_ref[lb] = o_c
        imps += [jnp.sum(jnp.where(rk == k, p, 0.0), axis=0, keepdims=True) for k in range(N_KV_HEADS)]
    imp = jnp.concatenate(imps, axis=0)
    nr = bb * N_KV_HEADS
    n_io = lax.broadcasted_iota(jnp.int32, (nr, nbp), 1)
    cur = pos // SEL_BLOCK
    forced = (n_io == 0) | (n_io == cur) | (n_io == cur - 1)
    v = jnp.where(forced, FORCE_SCORE, jnp.where(n_io > cur, -1.0, imp))
    v = jnp.where(n_io < nb, v, -3.0)
    col = lax.broadcasted_iota(jnp.int32, (nr, N_SEL), 1)
    idx = jnp.zeros((nr, N_SEL), jnp.int32)
    for r in range(N_SEL):
        mx = jnp.max(v, axis=1, keepdims=True)
        first = jnp.min(jnp.where(v == mx, n_io, nbp), axis=1, keepdims=True)
        idx = jnp.where(col == r, first, idx)
        v = jnp.where(n_io == first, -4.0, v)
    idx_ref[...] = idx


def _decode_a(q_pad, kct, vcp, pos, nb, gq, bb):
    b, nh, _ = q_pad.shape
    nbp = kct.shape[3]
    kern = functools.partial(_dec_a_kernel, pos=pos, nb=nb, gq=gq, bb=bb)
    return pl.pallas_call(
        kern,
        out_shape=(jax.ShapeDtypeStruct((b, nh, LANES), F32),
                   jax.ShapeDtypeStruct((b * N_KV_HEADS, N_SEL), jnp.int32)),
        grid=(b // bb,),
        in_specs=[pl.BlockSpec((bb, nh, LANES), lambda bi: (bi, 0, 0)),
                  pl.BlockSpec((bb, N_KV_HEADS, 2 * HEAD_DIM, nbp), lambda bi: (bi, 0, 0, 0)),
                  pl.BlockSpec((bb, N_KV_HEADS, nbp, LANES), lambda bi: (bi, 0, 0, 0))],
        out_specs=(pl.BlockSpec((bb, nh, LANES), lambda bi: (bi, 0, 0)),
                   pl.BlockSpec((bb * N_KV_HEADS, N_SEL), lambda bi: (bi, 0))),
        compiler_params=_cparams(("parallel",)),
        name="decode_cmp_topk",
    )(q_pad, kct, vcp)


def _dec_b_kernel(idx_ref, pt_ref, q_ref, gate_ref, oc_ref, new_ref, cache_ref, swin_ref,
                  o_ref, kbuf_ref, vbuf_ref, sem_ref, *, pos, nb_past, gq, n_b):
    nh = q_ref.shape[1]
    b = pl.program_id(0)
    cur = b % 2
    nkeys = N_SEL * PAGE_SIZE

    def copies(b_, slot_, k, j):
        n = idx_ref[b_, k * N_SEL + j]
        page = pt_ref[b_, lax.shift_right_logical(jnp.minimum(n, nb_past - 1), 1)]
        ck = pltpu.make_async_copy(cache_ref.at[page, 2, k], kbuf_ref.at[slot_, k, j], sem_ref.at[slot_])
        cv = pltpu.make_async_copy(cache_ref.at[page, 3, k], vbuf_ref.at[slot_, k, j], sem_ref.at[slot_])
        return n, ck, cv

    def issue(b_, slot_):
        for k in range(N_KV_HEADS):
            for j in range(N_SEL):
                n, ck, cv = copies(b_, slot_, k, j)

                @pl.when(n < nb_past)
                def _():
                    ck.start()
                    cv.start()

    @pl.when(b == 0)
    def _():
        issue(b, 0)

    @pl.when(b + 1 < n_b)
    def _():
        issue(b + 1, 1 - cur)

    q = q_ref[0]
    qb = q.astype(BF16)
    new = new_ref[0]
    eye = (lax.broadcasted_iota(jnp.int32, (HEAD_DIM, HEAD_DIM), 0)
           == lax.broadcasted_iota(jnp.int32, (HEAD_DIM, HEAD_DIM), 1))
    col0 = (lax.broadcasted_iota(jnp.int32, (HEAD_DIM, PAGE_SIZE), 1) == 0).astype(BF16)
    lane = lax.broadcasted_iota(jnp.int32, (1, nkeys), 1)
    r_in = lane % PAGE_SIZE
    rk = _row_kvh((nh, nkeys), gq)
    rk_h = _row_kvh((nh, HEAD_DIM), gq)

    kpos = jnp.zeros((nh, nkeys), jnp.int32)
    half = jnp.zeros((nh, nkeys), jnp.int32)
    for k in range(N_KV_HEADS):
        tail_k = _dot(jnp.where(eye, new[0, k * gq:k * gq + 1, :], 0.0).astype(BF16), col0)
        tail_v = _dot(jnp.where(eye, new[1, k * gq:k * gq + 1, :], 0.0).astype(BF16), col0)
        kp = r_in
        hv = jnp.zeros((1, nkeys), jnp.int32)
        for j in range(N_SEL):
            n, ck, cv = copies(b, cur, k, j)

            @pl.when(n < nb_past)
            def _():
                ck.wait()
                cv.wait()

            @pl.when(n >= nb_past)
            def _():
                kbuf_ref[cur, k, j] = tail_k
                vbuf_ref[cur, k, j] = tail_v

            in_j = lane // PAGE_SIZE == j
            kp = kp + jnp.where(in_j, lax.shift_right_logical(n, 1) * PAGE_SIZE, 0)
            hv = hv + jnp.where(in_j, n & 1, 0)
        kpos = jnp.where(rk == k, kp, kpos)
        half = jnp.where(rk == k, hv, half)
    s = jnp.zeros((nh, nkeys), F32)
    for k in range(N_KV_HEADS):
        sk = jnp.concatenate([_dot(qb, kbuf_ref[cur, k, j].astype(BF16)) for j in range(N_SEL)], axis=1)
        s = jnp.where(rk == k, sk, s)
    ok = (r_in // SEL_BLOCK == half) & (kpos <= pos)
    m = jnp.max(jnp.where(ok, s, -jnp.inf), axis=1, keepdims=True)
    p = jnp.where(ok, jnp.exp(s - m), 0.0)
    p = p / jnp.maximum(jnp.sum(p, axis=1, keepdims=True), 1e-30)
    pb = p.astype(BF16)
    o_s = jnp.zeros((nh, HEAD_DIM), F32)
    for k in range(N_KV_HEADS):
        o_k = jnp.zeros((nh, HEAD_DIM), F32)
        for j in range(N_SEL):
            o_k = o_k + _dot_nt(pb[:, j * PAGE_SIZE:(j + 1) * PAGE_SIZE], vbuf_ref[cur, k, j].astype(BF16))
        o_s = jnp.where(rk_h == k, o_k, o_s)

    w_buf = swin_ref.shape[4]
    rk_w = _row_kvh((nh, w_buf), gq)
    s_w = jnp.zeros((nh, w_buf), F32)
    for k in range(N_KV_HEADS):
        s_w = jnp.where(rk_w == k, _dot(qb, swin_ref[0, 0, k].astype(BF16)), s_w)
    s_n = jnp.sum(q * new[2], axis=1, keepdims=True)
    ridx = lax.broadcasted_iota(jnp.int32, (nh, w_buf), 1)
    ok_w = w_buf - ridx < WINDOW
    m_w = jnp.maximum(jnp.max(jnp.where(ok_w, s_w, -jnp.inf), axis=1, keepdims=True), s_n)
    p_w = jnp.where(ok_w, jnp.exp(s_w - m_w), 0.0)
    p_n = jnp.exp(s_n - m_w)
    l_w = jnp.sum(p_w, axis=1, keepdims=True) + p_n
    pwb = (p_w / l_w).astype(BF16)
    o_w = jnp.zeros((nh, HEAD_DIM), F32)
    for k in range(N_KV_HEADS):
        o_w = jnp.where(rk_h == k, _dot_nt(pwb, swin_ref[0, 1, k].astype(BF16)), o_w)
    o_w = o_w + (p_n / l_w) * new[3]

    gates = gate_ref[0]
    o_ref[0] = gates[:, 0:1] * oc_ref[0][:, :HEAD_DIM] + gates[:, 1:2] * o_s + gates[:, 2:3] * o_w


def _decode_b(idx_flat, page_table, q3, gates, o_c, new_h, cache_t, state_t, pos, gq):
    b, nh, _ = q3.shape
    w_buf = state_t.shape[4]
    nb_past = page_table.shape[1] * (PAGE_SIZE // SEL_BLOCK)
    kern = functools.partial(_dec_b_kernel, pos=pos, nb_past=nb_past, gq=gq, n_b=b)
    grid_spec = pltpu.PrefetchScalarGridSpec(
        num_scalar_prefetch=2,
        grid=(b,),
        in_specs=[pl.BlockSpec((1, nh, HEAD_DIM), lambda bi, ix, pt: (bi, 0, 0)),
                  pl.BlockSpec((1, nh, LANES), lambda bi, ix, pt: (bi, 0, 0)),
                  pl.BlockSpec((1, nh, LANES), lambda bi, ix, pt: (bi, 0, 0)),
                  pl.BlockSpec((1, 4, nh, HEAD_DIM), lambda bi, ix, pt: (bi, 0, 0, 0)),
                  pl.BlockSpec(memory_space=pl.ANY),
                  pl.BlockSpec((1, 2, N_KV_HEADS, HEAD_DIM, w_buf), lambda bi, ix, pt: (bi, 0, 0, 0, 0))],
        out_specs=pl.BlockSpec((1, nh, HEAD_DIM), lambda bi, ix, pt: (bi, 0, 0)),
        scratch_shapes=[pltpu.VMEM((2, N_KV_HEADS, N_SEL, HEAD_DIM, PAGE_SIZE), F32),
                        pltpu.VMEM((2, N_KV_HEADS, N_SEL, HEAD_DIM, PAGE_SIZE), F32),
                        pltpu.SemaphoreType.DMA((2,))])
    return pl.pallas_call(
        kern,
        out_shape=jax.ShapeDtypeStruct((b, nh, HEAD_DIM), F32),
        grid_spec=grid_spec,
        compiler_params=_cparams(("arbitrary",)),
        name="decode_sel_win",
    )(idx_flat, page_table, q3, gates, o_c, new_h, cache_t, state_t)


def _rope_tables(pos):
    half = ROT_DIM // 2
    inv = jnp.power(jnp.float32(ROPE_THETA), -jnp.arange(half, dtype=F32) * 2.0 / ROT_DIM)
    ang = pos.astype(F32)[:, None] * inv[None, :]
    cos, sin = jnp.cos(ang), jnp.sin(ang)
    r = jnp.arange(LANES) % HEAD_DIM
    f = r % half
    cos_t = jnp.where(r < ROT_DIM, cos[:, f], 1.0)
    sin_up = jnp.where(r < half, -sin[:, f], 0.0)
    sin_dn = jnp.where((r >= half) & (r < ROT_DIM), sin[:, f], 0.0)
    return cos_t, sin_up, sin_dn


def _tile_heads(v, n):
    return jnp.tile(v.astype(F32), n)[None, :]


def kernel(x_prompt, x_sample, cache_kv, state_kv_win, state_pool, page_table, c_prompt, c_sample, ada_w, ada_b, norm_mix, norm_ffn, pool_w, pool_scale, ada_kv_w, ada_kv_b, norm_kv, w_kv, k_norm, cmp_pe, cmp_w1, cmp_w2, w_qg, q_norm, w_o, w_gate_up, w_down):
    b_p, seq, d = x_prompt.shape
    b_s, dec_seq, _ = x_sample.shape
    depth = ada_w.shape[0]
    n_a = pool_w.shape[0]
    n_heads = d // HEAD_DIM
    gq = n_heads // N_KV_HEADS
    hq = N_KV_HEADS * HEAD_DIM
    n_pages = page_table.shape[1]
    past_len = n_pages * PAGE_SIZE
    w_buf = state_kv_win.shape[1]
    d_ff = w_down.shape[1]
    assert dec_seq == 1 and hq == 2 * LANES and seq % PAGE_SIZE == 0
    sm = HEAD_DIM ** -0.5

    w_gu_b = w_gate_up.astype(BF16)
    w_dn_b = w_down.astype(BF16)
    w_kv_b = w_kv.astype(BF16)
    pool_w_b = pool_w.astype(BF16)
    w_q_b = w_qg[:, :, :d].astype(BF16)
    wg_cols = w_qg[:, :, d:].reshape(-1, d, N_KV_HEADS, 3 * gq)
    w_g_b = jnp.pad(wg_cols, ((0, 0), (0, 0), (0, 0), (0, LANES - 3 * gq))).reshape(-1, d, N_KV_HEADS * LANES).astype(BF16)
    w_o_b = w_o.astype(BF16)

    head_of = jnp.arange(hq) // HEAD_DIM
    seg = (head_of[:, None] == head_of[None, :]).astype(BF16)
    lane_in = jnp.arange(hq)
    lane_out = jnp.arange(N_KV_HEADS * LANES)
    place = ((lane_out[None, :] // LANES == lane_in[:, None] // HEAD_DIM)
             & (lane_out[None, :] % LANES == lane_in[:, None] % HEAD_DIM)).astype(BF16)
    one_row = (lane_out % LANES == HEAD_DIM).astype(F32)[None, :]
    sw = gq * HEAD_DIM
    cin = jnp.arange(sw)
    pin = jnp.stack([((cin[:, None] // HEAD_DIM == g) & (jnp.arange(LANES)[None, :] == cin[:, None] % HEAD_DIM))
                     for g in range(gq)]).astype(BF16)
    pout = jnp.stack([((jnp.arange(LANES)[:, None] < HEAD_DIM)
                       & (cin[None, :] == g * HEAD_DIM + jnp.arange(LANES)[:, None]))
                      for g in range(gq)]).astype(BF16)

    kn = jnp.stack([_tile_heads(k_norm[1], N_KV_HEADS), _tile_heads(k_norm[2], N_KV_HEADS)])
    kn0 = _tile_heads(k_norm[0], N_KV_HEADS)
    pe_t = jnp.tile(cmp_pe.transpose(0, 2, 1), (1, 1, 2))
    eye2 = jnp.eye(2, dtype=F32)
    w1_dsj = cmp_w1.reshape(2, SEL_BLOCK, HEAD_DIM, -1).transpose(0, 2, 1, 3)
    n_hid = w1_dsj.shape[-1]
    w1cat = jnp.einsum('ab,zdsj->zdasbj', eye2, w1_dsj).reshape(2, HEAD_DIM // 2, 2 * 2 * SEL_BLOCK, 2 * n_hid).astype(BF16)
    w2bd = jnp.einsum('ab,zje->zajbe', eye2, cmp_w2).reshape(2, 2 * n_hid, 2 * HEAD_DIM).astype(BF16)

    c_all = jnp.concatenate([c_prompt, c_sample], axis=0)
    m_all = c_all.shape[0]
    m_pad = -(-m_all // 8) * 8
    c_all = jnp.pad(c_all, ((0, m_pad - m_all), (0, 0)))
    mods = _mods(c_all, ada_w.reshape(depth * 2, d, 3 * d), ada_b.reshape(depth * 2, 1, 3 * d)).reshape(depth, 2, m_pad, 3 * d)
    mod_kv = _mods(c_all, ada_kv_w[None], ada_kv_b[None, None, :])[0]

    def mod_p(l, j):
        return mods[l, j, :b_p][:, None, :]

    def mod_s_tok(l, j):
        return mods[l, j, b_p:m_all][None]

    def mod_s_seq(l, j):
        return mods[l, j, b_p:m_all][:, None, :]

    tm = min(512, seq)
    tf = d_ff // 2 if (d_ff // 2) % LANES == 0 else d_ff
    tq = 128
    ck = 512
    ck_big = min(2048, seq // 2)

    x = x_prompt
    pool_p = []
    for l in range(n_a):
        x, npool = _pool_layer(x, jnp.zeros((b_p, POOL_HALO, d), F32), mod_p(l, 0), norm_mix[l][None], pool_w_b[l],
                               pool_scale[l][None], 0, tm)
        pool_p.append(npool[:, 1:])
        x = _ffn(x, mod_p(l, 1), norm_ffn[l][None], w_gu_b[l], w_dn_b[l], tm, tf)

    tabs_p = _rope_tables(jnp.arange(seq))
    rows_t, win_t, kst, vs, kwt, vw = _kv_proj(x, mod_kv[:b_p][:, None, :], norm_kv[None], w_kv_b, kn, seg, tabs_p,
                                               place, one_row, tm, True)
    nb_p = seq // SEL_BLOCK
    pt_p = jnp.zeros((b_p, seq // PAGE_SIZE), jnp.int32)
    raw_p = _compress(pt_p, rows_t, pe_t, w1cat, w2bd, False)
    tabs_blk_p = _rope_tables((jnp.arange(nb_p) + 1) * SEL_BLOCK - 1)
    kct_p, vcp_p = _cmp_finish(raw_p[0], raw_p[1], kn0, seg, tabs_blk_p, place)
    e_mat = (jnp.arange(seq)[None, :] // SEL_BLOCK == jnp.arange(nb_p)[:, None]).astype(BF16)
    for l in range(n_a, depth):
        j = l - n_a
        q, gates = _q_proj(x, mod_p(l, 0), norm_mix[l][None], w_q_b[j], w_g_b[j], _tile_heads(q_norm[j], N_KV_HEADS),
                           seg, tabs_p, tm, sm * LOG2E, BF16)
        o = _attention(q, gates, kct_p, vcp_p, kst, vs, kwt, vw, e_mat, pin, pout, tq, ck, ck_big)
        x = _o_proj(x, o, mod_p(l, 0), w_o_b[j], tm)
        x = _ffn(x, mod_p(l, 1), norm_ffn[l][None], w_gu_b[l], w_dn_b[l], tm, tf)
    y_prompt = x
    kv_rows_prompt = rows_t.transpose(0, 4, 1, 2, 3)
    win_keep_p = min(WINDOW, seq)
    win_prompt = win_t[..., seq - win_keep_p:].transpose(0, 4, 1, 2, 3)
    pool_prompt = jnp.stack(pool_p)

    pos_s = past_len
    xs = x_sample
    pool_s = []
    for l in range(n_a):
        pre = jnp.pad(state_pool[l], ((0, 0), (POOL_HALO - state_pool.shape[2], 0), (0, 0)))
        xs, npool = _pool_layer(xs, pre, mod_s_seq(l, 0), norm_mix[l][None], pool_w_b[l], pool_scale[l][None], pos_s, 1)
        pool_s.append(npool[:, 1:])
        xs = _ffn(xs.reshape(1, b_s, d), mod_s_tok(l, 1), norm_ffn[l][None], w_gu_b[l], w_dn_b[l], b_s, tf).reshape(b_s, 1, d)
    xt = xs.reshape(1, b_s, d)
    tabs_s = _rope_tables(jnp.full((b_s,), pos_s))
    rows_s, win_s = _kv_proj(xt, mod_kv[b_p:m_all][None], norm_kv[None], w_kv_b, kn, seg, tabs_s, place, one_row,
                             b_s, False)
    rows_s = rows_s.reshape(b_s, 4, N_KV_HEADS, HEAD_DIM)
    win_s = win_s.reshape(b_s, 2, N_KV_HEADS, HEAD_DIM)

    cache_t = cache_kv.transpose(0, 2, 3, 4, 1)
    state_t = state_kv_win.transpose(0, 2, 3, 4, 1)
    raw_s = _compress(page_table, cache_t, pe_t, w1cat, w2bd, True)
    nb_s = -(-(past_len + 1) // SEL_BLOCK)
    z_tail = jnp.pad(rows_s[:, :2].transpose(1, 0, 2, 3).reshape(2 * b_s * N_KV_HEADS, HEAD_DIM),
                     ((0, 0), (0, (SEL_BLOCK - 1) * HEAD_DIM)))
    raw_tail = _compress_tail(z_tail, cmp_pe.reshape(2, 1, SEL_BLOCK * HEAD_DIM), cmp_w1.astype(BF16),
                              cmp_w2.astype(BF16)).reshape(2, b_s, 1, hq)
    nbp_s = -(-nb_s // LANES) * LANES
    raw_all = jnp.pad(jnp.concatenate([raw_s, raw_tail], axis=2), ((0, 0), (0, 0), (0, nbp_s - nb_s), (0, 0)))
    tabs_blk_s = _rope_tables((jnp.arange(nbp_s) + 1) * SEL_BLOCK - 1)
    kct_s, vcp_s = _cmp_finish(raw_all[0], raw_all[1], kn0, seg, tabs_blk_s, place)

    new_h = jnp.repeat(jnp.concatenate([rows_s[:, 2:4], win_s], axis=1), gq, axis=2)
    bb = math.gcd(b_s, 8)
    for l in range(n_a, depth):
        j = l - n_a
        q_s, gates_s = _q_proj(xt, mod_s_tok(l, 0), norm_mix[l][None], w_q_b[j], w_g_b[j],
                               _tile_heads(q_norm[j], N_KV_HEADS), seg, tabs_s, b_s, sm, F32)
        q3 = q_s.reshape(b_s, n_heads, HEAD_DIM)
        q_cmp = jnp.pad(q3, ((0, 0), (0, 0), (0, LANES - HEAD_DIM)))
        g3 = gates_s[0, :, :, :3 * gq].transpose(1, 0, 2).reshape(b_s, n_heads, 3)
        g3 = jnp.pad(g3, ((0, 0), (0, 0), (0, LANES - 3)))
        o_c, idx = _decode_a(q_cmp, kct_s, vcp_s, pos_s, nb_s, gq, bb)
        o_s = _decode_b(idx.reshape(b_s, N_KV_HEADS * N_SEL), page_table, q3, g3, o_c, new_h, cache_t, state_t,
                        pos_s, gq)
        xt = _o_proj(xt, o_s.reshape(1, b_s, d), mod_s_tok(l, 0), w_o_b[j], b_s)
        xt = _ffn(xt, mod_s_tok(l, 1), norm_ffn[l][None], w_gu_b[l], w_dn_b[l], b_s, tf)
    y_sample = xt.reshape(b_s, 1, d)
    kv_rows_sample = rows_s.reshape(b_s, 1, 4, N_KV_HEADS, HEAD_DIM)
    win_sample = jnp.concatenate([state_t[..., 1:], win_s[..., None]], axis=-1).transpose(0, 4, 1, 2, 3)
    pool_sample = jnp.stack(pool_s)
    return (y_prompt, y_sample, kv_rows_prompt, kv_rows_sample, win_prompt, win_sample, pool_prompt, pool_sample)
```

```python
import functools
import math

import jax
import jax.numpy as jnp
from jax import lax
from jax.experimental import pallas as pl
from jax.experimental.pallas import tpu as pltpu

F32 = jnp.float32
BF16 = jnp.bfloat16

POOL_WINDOWS = (2, 4, 8, 16)
POOL_HALO = 16
HEAD_DIM = 64
N_KV_HEADS = 4
ROT_DIM = 16
ROPE_THETA = 500000.0
SEL_BLOCK = 64
N_SEL = 16
N_FORCED = 3
WINDOW = 512
PAGE_SIZE = 128
FORCE_SCORE = 1.0e4
EPS = 1e-6
NEG = -float(2 ** 30)
LOG2E = 1.4426950408889634
LANES = 128
TILE_ROW_STRIDE = 72
VMEM_LIMIT = 56 * 1024 * 1024
NT_DIMS = (((1,), (1,)), ((), ()))


def _cparams(sem):
    return pltpu.CompilerParams(dimension_semantics=sem, vmem_limit_bytes=VMEM_LIMIT)


def _dot(a, b):
    return jnp.dot(a, b, preferred_element_type=F32)


def _dot_nt(a, b):
    return lax.dot_general(a, b, NT_DIMS, preferred_element_type=F32)


def _split(a):
    hi = a.astype(BF16)
    lo = (a - hi.astype(F32)).astype(BF16)
    return hi, lo


def _adaln(x, g, shift, scale):
    ms = jnp.mean(x * x, axis=-1, keepdims=True)
    return x * lax.rsqrt(ms + EPS) * g * (1.0 + scale) + shift


def _head_rms(x, gain, seg):
    hi, lo = _split(x * x)
    ss = _dot(hi, seg) + _dot(lo, seg)
    return x * lax.rsqrt(ss * (1.0 / HEAD_DIM) + EPS) * gain


def _rope(x, cos, sin_up, sin_dn):
    outs = []
    for a in range(x.shape[1] // LANES):
        xa = x[:, a * LANES:(a + 1) * LANES]
        up = pltpu.roll(xa, LANES - ROT_DIM // 2, 1)
        dn = pltpu.roll(xa, ROT_DIM // 2, 1)
        outs.append(xa * cos + up * sin_up + dn * sin_dn)
    return outs[0] if len(outs) == 1 else jnp.concatenate(outs, axis=1)


def _gelu_tanh(x):
    return x * (0.5 * (1.0 + jnp.tanh(math.sqrt(2.0 / math.pi) * (x + 0.044715 * (x * x * x)))))


def _mods_kernel(c_ref, w_ref, b_ref, o_ref):
    ch, cl = _split(c_ref[...])
    wh, wl = _split(w_ref[0])
    o_ref[0] = _dot(ch, wh) + _dot(ch, wl) + _dot(cl, wh) + b_ref[0]


def _mods(c_all, w, b, tn=1024):
    n_l, d, n = w.shape
    m = c_all.shape[0]
    return pl.pallas_call(
        _mods_kernel,
        out_shape=jax.ShapeDtypeStruct((n_l, m, n), F32),
        grid=(n_l, n // tn),
        in_specs=[pl.BlockSpec((m, d), lambda l, j: (0, 0)),
                  pl.BlockSpec((1, d, tn), lambda l, j: (l, 0, j)),
                  pl.BlockSpec((1, 1, tn), lambda l, j: (l, 0, j))],
        out_specs=pl.BlockSpec((1, m, tn), lambda l, j: (l, 0, j)),
        compiler_params=_cparams(("parallel", "parallel")),
        name="mods",
    )(c_all, w, b)


def _pool_kernel(x_ref, pre_ref, mod_ref, g_ref, pw_ref, ps_ref, o_ref, np_ref, hb_ref, *, tt, pos0, d):
    i = pl.program_id(1)
    x = x_ref[0]
    mod = mod_ref[0]
    h = _adaln(x, g_ref[...], mod[:, :d], mod[:, d:2 * d])

    @pl.when(i == 0)
    def _():
        hb_ref[0:POOL_HALO, :] = pre_ref[0]

    hb_ref[POOL_HALO:POOL_HALO + tt, :] = h
    pos = pos0 + i * tt + lax.broadcasted_iota(jnp.int32, (tt, 1), 0)
    pg = d // len(POOL_WINDOWS)
    ys = []
    for gi, w in enumerate(POOL_WINDOWS):
        c0 = gi * pg
        hg = h[:, c0:c0 + pg]
        s = hg
        for j in range(1, w):
            s = s + hb_ref[POOL_HALO - j:POOL_HALO - j + tt, c0:c0 + pg]
        cnt = jnp.minimum(w, pos + 1).astype(F32)
        pooled = s / cnt - hg
        ys.append(_dot(pooled.astype(BF16), pw_ref[gi]))
    y = jnp.concatenate(ys, axis=1) * ps_ref[...]
    o_ref[0] = x + mod[:, 2 * d:] * y
    last = hb_ref[tt:tt + POOL_HALO, :]
    np_ref[0] = last
    hb_ref[0:POOL_HALO, :] = last


def _pool_layer(x, prefix16, mod, g, pw, ps, pos0, tt):
    b, t, d = x.shape
    kern = functools.partial(_pool_kernel, tt=tt, pos0=pos0, d=d)
    pg = d // len(POOL_WINDOWS)
    return pl.pallas_call(
        kern,
        out_shape=(jax.ShapeDtypeStruct((b, t, d), F32), jax.ShapeDtypeStruct((b, POOL_HALO, d), F32)),
        grid=(b, t // tt),
        in_specs=[pl.BlockSpec((1, tt, d), lambda bi, i: (bi, i, 0)),
                  pl.BlockSpec((1, POOL_HALO, d), lambda bi, i: (bi, 0, 0)),
                  pl.BlockSpec((1, 1, 3 * d), lambda bi, i: (bi, 0, 0)),
                  pl.BlockSpec((1, d), lambda bi, i: (0, 0)),
                  pl.BlockSpec((len(POOL_WINDOWS), pg, pg), lambda bi, i: (0, 0, 0)),
                  pl.BlockSpec((1, d), lambda bi, i: (0, 0))],
        out_specs=(pl.BlockSpec((1, tt, d), lambda bi, i: (bi, i, 0)),
                   pl.BlockSpec((1, POOL_HALO, d), lambda bi, i: (bi, 0, 0))),
        scratch_shapes=[pltpu.VMEM((POOL_HALO + max(tt, 8), d), F32)],
        compiler_params=_cparams(("parallel", "arbitrary")),
        name="pool_layer",
    )(x, prefix16, mod, g, pw, ps)


def _ffn_kernel(x_ref, mod_ref, g_ref, wg_ref, wu_ref, wd_ref, o_ref, h_ref, acc_ref, *, d):
    f = pl.program_id(2)

    @pl.when(f == 0)
    def _():
        mod = mod_ref[0]
        h_ref[...] = _adaln(x_ref[0], g_ref[...], mod[:, :d], mod[:, d:2 * d]).astype(BF16)
        acc_ref[...] = jnp.zeros_like(acc_ref)

    hb = h_ref[...]
    g = _dot(hb, wg_ref[...])
    u = _dot(hb, wu_ref[...])
    a = (g * (1.0 / (1.0 + jnp.exp(-g)))) * u
    acc_ref[...] += _dot(a.astype(BF16), wd_ref[...])

    @pl.when(f == pl.num_programs(2) - 1)
    def _():
        o_ref[0] = x_ref[0] + mod_ref[0][:, 2 * d:] * acc_ref[...]


def _ffn(x, mod, g, w_gu, w_dn, tm, tf):
    b, t, d = x.shape
    rm = mod.shape[1]
    ff = w_dn.shape[0]
    nf = ff // tf
    kern = functools.partial(_ffn_kernel, d=d)
    return pl.pallas_call(
        kern,
        out_shape=jax.ShapeDtypeStruct((b, t, d), F32),
        grid=(b, t // tm, nf),
        in_specs=[pl.BlockSpec((1, tm, d), lambda bi, i, f: (bi, i, 0)),
                  pl.BlockSpec((1, rm, 3 * d), lambda bi, i, f: (bi, 0, 0)),
                  pl.BlockSpec((1, d), lambda bi, i, f: (0, 0)),
                  pl.BlockSpec((d, tf), lambda bi, i, f: (0, f)),
                  pl.BlockSpec((d, tf), lambda bi, i, f: (0, nf + f)),
                  pl.BlockSpec((tf, d), lambda bi, i, f: (f, 0))],
        out_specs=pl.BlockSpec((1, tm, d), lambda bi, i, f: (bi, i, 0)),
        scratch_shapes=[pltpu.VMEM((tm, d), BF16), pltpu.VMEM((tm, d), F32)],
        compiler_params=_cparams(("parallel", "parallel", "arbitrary")),
        name="ffn",
    )(x, mod, g, w_gu, w_gu, w_dn)


def _kv_kernel(x_ref, mod_ref, g_ref, w_ref, kn_ref, seg_ref, cos_ref, su_ref, sd_ref, place_ref, one_ref,
               *out_refs, d, transposed):
    hq = N_KV_HEADS * HEAD_DIM
    mod = mod_ref[0]
    h = _adaln(x_ref[0], g_ref[...], mod[:, :d], mod[:, d:2 * d])
    proj = _dot(h.astype(BF16), w_ref[...])
    seg = seg_ref[...]
    cos, su, sd = cos_ref[...], su_ref[...], sd_ref[...]
    k_sel = _rope(_head_rms(proj[:, 2 * hq:3 * hq], kn_ref[0], seg), cos, su, sd)
    k_win = _rope(_head_rms(proj[:, 4 * hq:5 * hq], kn_ref[1], seg), cos, su, sd)
    v_sel = proj[:, 3 * hq:4 * hq]
    v_win = proj[:, 5 * hq:6 * hq]
    if not transposed:
        rows_ref, win_ref = out_refs
        rows_ref[0] = jnp.concatenate([proj[:, :2 * hq], k_sel, v_sel], axis=1)
        win_ref[0] = jnp.concatenate([k_win, v_win], axis=1)
        return
    rows_t_ref, win_t_ref, kst_ref, vs_ref, kwt_ref, vw_ref = out_refs
    tm = k_sel.shape[0]
    zeros = jnp.zeros((HEAD_DIM, tm), BF16)
    k_sel_t = k_sel.T
    k_win_t = k_win.T
    for t_ref, slabs in ((rows_t_ref, (proj[:, :hq].T, proj[:, hq:2 * hq].T, k_sel_t, v_sel.T)),
                         (win_t_ref, (k_win_t, v_win.T))):
        for sl, slab in enumerate(slabs):
            for k in range(N_KV_HEADS):
                t_ref[0, sl, k] = slab[k * HEAD_DIM:(k + 1) * HEAD_DIM, :]
    for kt_ref, kt32 in ((kst_ref, k_sel_t), (kwt_ref, k_win_t)):
        kt = kt32.astype(BF16)
        for k in range(N_KV_HEADS):
            kt_ref[0, k, 0:HEAD_DIM, :] = kt[k * HEAD_DIM:(k + 1) * HEAD_DIM, :]
            kt_ref[0, k, HEAD_DIM:2 * HEAD_DIM, :] = zeros
    for va_ref, vv in ((vs_ref, v_sel), (vw_ref, v_win)):
        va = (_dot(vv.astype(BF16), place_ref[...]) + one_ref[...]).astype(BF16)
        for k in range(N_KV_HEADS):
            va_ref[0, k] = va[:, k * LANES:(k + 1) * LANES]


def _kv_proj(x, mod, g, w_kv, kn, seg, tabs, place, one_row, tm, transposed):
    b, t, d = x.shape
    rm = mod.shape[1]
    hq = N_KV_HEADS * HEAD_DIM
    kern = functools.partial(_kv_kernel, d=d, transposed=transposed)
    const2 = lambda bi, i: (0, 0)
    tab_spec = pl.BlockSpec((tm, LANES), lambda bi, i: (i, 0))
    if transposed:
        kt_shape = jax.ShapeDtypeStruct((b, N_KV_HEADS, 2 * HEAD_DIM, t), BF16)
        va_shape = jax.ShapeDtypeStruct((b, N_KV_HEADS, t, LANES), BF16)
        kt_spec = pl.BlockSpec((1, N_KV_HEADS, 2 * HEAD_DIM, tm), lambda bi, i: (bi, 0, 0, i))
        va_spec = pl.BlockSpec((1, N_KV_HEADS, tm, LANES), lambda bi, i: (bi, 0, i, 0))
        out_shape = [jax.ShapeDtypeStruct((b, 4, N_KV_HEADS, HEAD_DIM, t), F32),
                     jax.ShapeDtypeStruct((b, 2, N_KV_HEADS, HEAD_DIM, t), F32), kt_shape, va_shape, kt_shape, va_shape]
        out_specs = [pl.BlockSpec((1, 4, N_KV_HEADS, HEAD_DIM, tm), lambda bi, i: (bi, 0, 0, 0, i)),
                     pl.BlockSpec((1, 2, N_KV_HEADS, HEAD_DIM, tm), lambda bi, i: (bi, 0, 0, 0, i)),
                     kt_spec, va_spec, kt_spec, va_spec]
    else:
        out_shape = [jax.ShapeDtypeStruct((b, t, 4 * hq), F32), jax.ShapeDtypeStruct((b, t, 2 * hq), F32)]
        out_specs = [pl.BlockSpec((1, tm, 4 * hq), lambda bi, i: (bi, i, 0)),
                     pl.BlockSpec((1, tm, 2 * hq), lambda bi, i: (bi, i, 0))]
    return pl.pallas_call(
        kern,
        out_shape=tuple(out_shape),
        grid=(b, t // tm),
        in_specs=[pl.BlockSpec((1, tm, d), lambda bi, i: (bi, i, 0)),
                  pl.BlockSpec((1, rm, 2 * d), lambda bi, i: (bi, 0, 0)),
                  pl.BlockSpec((1, d), const2),
                  pl.BlockSpec((d, 6 * hq), const2),
                  pl.BlockSpec((2, 1, hq), lambda bi, i: (0, 0, 0)),
                  pl.BlockSpec((hq, hq), const2),
                  tab_spec, tab_spec, tab_spec,
                  pl.BlockSpec((hq, N_KV_HEADS * LANES), const2),
                  pl.BlockSpec((1, N_KV_HEADS * LANES), const2)],
        out_specs=tuple(out_specs),
        compiler_params=_cparams(("parallel", "parallel")),
        name="kv_proj",
    )(x, mod, g, w_kv, kn, seg, *tabs, place, one_row)


def _compress_kernel(pt_ref, src_ref, pe_ref, w1_ref, w2_ref, o_ref, buf_ref, sem_ref, *, n_pages, n_b, paged):
    n_tiles = n_pages * N_KV_HEADS
    sl = pl.program_id(0)
    b = pl.program_id(1)
    step = sl * n_b + b
    cur = step % 2

    def tile_copy(sl_, b_, slot_, p, h):
        if paged:
            src = src_ref.at[pt_ref[b_, p], sl_, h]
        else:
            src = src_ref.at[b_, sl_, h, :, pl.ds(p * PAGE_SIZE, PAGE_SIZE)]
        ti = p * N_KV_HEADS + h
        return pltpu.make_async_copy(src, buf_ref.at[slot_, pl.ds(ti * TILE_ROW_STRIDE, HEAD_DIM), :],
                                     sem_ref.at[slot_])

    def issue(sl_, b_, slot_):
        for p in range(n_pages):
            for h in range(N_KV_HEADS):
                tile_copy(sl_, b_, slot_, p, h).start()

    @pl.when(step == 0)
    def _():
        issue(sl, b, 0)

    @pl.when(step + 1 < 2 * n_b)
    def _():
        nxt = step + 1
        issue(nxt // n_b, nxt % n_b, 1 - cur)

    for p in range(n_pages):
        for h in range(N_KV_HEADS):
            tile_copy(sl, b, cur, p, h).wait()

    pe = pe_ref[0]
    hid = jnp.zeros((n_tiles, 2 * LANES), F32)
    for dp in range(HEAD_DIM // 2):
        xs = [buf_ref[cur, pl.ds(dd, n_tiles, stride=TILE_ROW_STRIDE), :] + pe[dd:dd + 1, :]
              for dd in (2 * dp, 2 * dp + 1)]
        hid = hid + _dot(jnp.concatenate(xs, axis=1).astype(BF16), w1_ref[0, dp])
    o_ref[0, 0] = _dot(_gelu_tanh(hid).astype(BF16), w2_ref[0])


def _compress(page_table, src, pe_t, w1cat, w2bd, paged):
    n_b, n_pages = page_table.shape
    n_tiles = n_pages * N_KV_HEADS
    kern = functools.partial(_compress_kernel, n_pages=n_pages, n_b=n_b, paged=paged)
    grid_spec = pltpu.PrefetchScalarGridSpec(
        num_scalar_prefetch=1,
        grid=(2, n_b),
        in_specs=[pl.BlockSpec(memory_space=pl.ANY),
                  pl.BlockSpec((1, HEAD_DIM, LANES), lambda s, b, pt: (s, 0, 0)),
                  pl.BlockSpec((1, HEAD_DIM // 2, 2 * LANES, 2 * LANES), lambda s, b, pt: (s, 0, 0, 0)),
                  pl.BlockSpec((1, 2 * LANES, LANES), lambda s, b, pt: (s, 0, 0))],
        out_specs=pl.BlockSpec((1, 1, n_tiles, LANES), lambda s, b, pt: (s, b, 0, 0)),
        scratch_shapes=[pltpu.VMEM((2, n_tiles * TILE_ROW_STRIDE, LANES), F32),
                        pltpu.SemaphoreType.DMA((2,))])
    raw = pl.pallas_call(
        kern,
        out_shape=jax.ShapeDtypeStruct((2, n_b, n_tiles, LANES), F32),
        grid_spec=grid_spec,
        compiler_params=_cparams(("arbitrary", "arbitrary")),
        name="compress",
    )(page_table, src, pe_t, w1cat, w2bd)
    raw = raw.reshape(2, n_b, n_pages, N_KV_HEADS, 2, HEAD_DIM).transpose(0, 1, 2, 4, 3, 5)
    return raw.reshape(2, n_b, 2 * n_pages, N_KV_HEADS * HEAD_DIM)


def _tail_kernel(z_ref, pe_ref, w1_ref, w2_ref, o_ref):
    z = z_ref[...] + pe_ref[0]
    hid = _gelu_tanh(_dot(z.astype(BF16), w1_ref[0]))
    o_ref[0] = _dot(hid.astype(BF16), w2_ref[0])


def _compress_tail(z, pe_flat, w1, w2):
    m, kdim = z.shape[0] // 2, z.shape[1]
    hid = w1.shape[2]
    return pl.pallas_call(
        _tail_kernel,
        out_shape=jax.ShapeDtypeStruct((2, m, HEAD_DIM), F32),
        grid=(2,),
        in_specs=[pl.BlockSpec((m, kdim), lambda s: (s, 0)),
                  pl.BlockSpec((1, 1, kdim), lambda s: (s, 0, 0)),
                  pl.BlockSpec((1, kdim, hid), lambda s: (s, 0, 0)),
                  pl.BlockSpec((1, hid, HEAD_DIM), lambda s: (s, 0, 0))],
        out_specs=pl.BlockSpec((1, m, HEAD_DIM), lambda s: (s, 0, 0)),
        compiler_params=_cparams(("parallel",)),
        name="compress_tail",
    )(z, pe_flat, w1, w2)


def _cmp_finish_kernel(kc_ref, vc_ref, kn_ref, seg_ref, cos_ref, su_ref, sd_ref, place_ref, kt_ref, vp_ref):
    kc = _rope(_head_rms(kc_ref[0], kn_ref[...], seg_ref[...]), cos_ref[...], su_ref[...], sd_ref[...])
    kt = kc.T.astype(BF16)
    nbp = kc.shape[0]
    zeros = jnp.zeros((HEAD_DIM, nbp), BF16)
    vp = _dot(vc_ref[0].astype(BF16), place_ref[...]).astype(BF16)
    for k in range(N_KV_HEADS):
        kt_ref[0, k, 0:HEAD_DIM, :] = kt[k * HEAD_DIM:(k + 1) * HEAD_DIM, :]
        kt_ref[0, k, HEAD_DIM:2 * HEAD_DIM, :] = zeros
        vp_ref[0, k] = vp[:, k * LANES:(k + 1) * LANES]


def _cmp_finish(kc_raw, vc_raw, kn0, seg, tabs, place):
    b, nbp, hq = kc_raw.shape
    const2 = lambda bi: (0, 0)
    tab_spec = pl.BlockSpec((nbp, LANES), const2)
    return pl.pallas_call(
        _cmp_finish_kernel,
        out_shape=(jax.ShapeDtypeStruct((b, N_KV_HEADS, 2 * HEAD_DIM, nbp), BF16),
                   jax.ShapeDtypeStruct((b, N_KV_HEADS, nbp, LANES), BF16)),
        grid=(b,),
        in_specs=[pl.BlockSpec((1, nbp, hq), lambda bi: (bi, 0, 0)),
                  pl.BlockSpec((1, nbp, hq), lambda bi: (bi, 0, 0)),
                  pl.BlockSpec((1, hq), const2),
                  pl.BlockSpec((hq, hq), const2),
                  tab_spec, tab_spec, tab_spec,
                  pl.BlockSpec((hq, N_KV_HEADS * LANES), const2)],
        out_specs=(pl.BlockSpec((1, N_KV_HEADS, 2 * HEAD_DIM, nbp), lambda bi: (bi, 0, 0, 0)),
                   pl.BlockSpec((1, N_KV_HEADS, nbp, LANES), lambda bi: (bi, 0, 0, 0))),
        compiler_params=_cparams(("parallel",)),
        name="cmp_finish",
    )(kc_raw, vc_raw, kn0, seg, *tabs, place)


def _q_kernel(x_ref, mod_ref, g_ref, wq_ref, wg_ref, qn_ref, seg_ref, cos_ref, su_ref, sd_ref,
              q_ref, gate_ref, *, d, q_scale):
    hq = N_KV_HEADS * HEAD_DIM
    mod = mod_ref[0]
    hb = _adaln(x_ref[0], g_ref[...], mod[:, :d], mod[:, d:2 * d]).astype(BF16)
    q = _dot(hb, wq_ref[...])
    seg = seg_ref[...]
    cos, su, sd = cos_ref[...], su_ref[...], sd_ref[...]
    for c in range(d // hq):
        qc = _rope(_head_rms(q[:, c * hq:(c + 1) * hq], qn_ref[...], seg), cos, su, sd) * q_scale
        q_ref[0, :, c * hq:(c + 1) * hq] = qc.astype(q_ref.dtype)
    gates = 1.0 / (1.0 + jnp.exp(-_dot(hb, wg_ref[...])))
    for k in range(N_KV_HEADS):
        gate_ref[0, k] = gates[:, k * LANES:(k + 1) * LANES]


def _q_proj(x, mod, g, w_q, w_g, qn, seg, tabs, tm, q_scale, q_dtype):
    b, t, d = x.shape
    rm = mod.shape[1]
    hq = N_KV_HEADS * HEAD_DIM
    ng = w_g.shape[1]
    kern = functools.partial(_q_kernel, d=d, q_scale=q_scale)
    const2 = lambda bi, i: (0, 0)
    tab_spec = pl.BlockSpec((tm, LANES), lambda bi, i: (i, 0))
    return pl.pallas_call(
        kern,
        out_shape=(jax.ShapeDtypeStruct((b, t, d), q_dtype),
                   jax.ShapeDtypeStruct((b, N_KV_HEADS, t, LANES), F32)),
        grid=(b, t // tm),
        in_specs=[pl.BlockSpec((1, tm, d), lambda bi, i: (bi, i, 0)),
                  pl.BlockSpec((1, rm, 3 * d), lambda bi, i: (bi, 0, 0)),
                  pl.BlockSpec((1, d), const2),
                  pl.BlockSpec((d, d), const2),
                  pl.BlockSpec((d, ng), const2),
                  pl.BlockSpec((1, hq), const2),
                  pl.BlockSpec((hq, hq), const2),
                  tab_spec, tab_spec, tab_spec],
        out_specs=(pl.BlockSpec((1, tm, d), lambda bi, i: (bi, i, 0)),
                   pl.BlockSpec((1, N_KV_HEADS, tm, LANES), lambda bi, i: (bi, 0, i, 0))),
        compiler_params=_cparams(("parallel", "parallel")),
        name="q_proj",
    )(x, mod, g, w_q, w_g, qn, seg, *tabs)


def _oproj_kernel(x_ref, o_ref, mod_ref, w_ref, y_ref, *, d):
    y_ref[0] = x_ref[0] + mod_ref[0][:, 2 * d:] * _dot(o_ref[0].astype(BF16), w_ref[...])


def _o_proj(x, o, mod, w_o, tm):
    b, t, d = x.shape
    rm = mod.shape[1]
    kern = functools.partial(_oproj_kernel, d=d)
    return pl.pallas_call(
        kern,
        out_shape=jax.ShapeDtypeStruct((b, t, d), F32),
        grid=(b, t // tm),
        in_specs=[pl.BlockSpec((1, tm, d), lambda bi, i: (bi, i, 0)),
                  pl.BlockSpec((1, tm, d), lambda bi, i: (bi, i, 0)),
                  pl.BlockSpec((1, rm, 3 * d), lambda bi, i: (bi, 0, 0)),
                  pl.BlockSpec((d, d), lambda bi, i: (0, 0))],
        out_specs=pl.BlockSpec((1, tm, d), lambda bi, i: (bi, i, 0)),
        compiler_params=_cparams(("parallel", "parallel")),
        name="o_proj",
    )(x, o, mod, w_o)


def _select_kernel(q_ref, kct_ref, vcp_ref, pin_ref, bias_ref, oc_ref, *, tq, n_sub, nb):
    gq = q_ref.shape[2] // HEAD_DIM
    rows = gq * tq
    i = pl.program_id(2)
    for u in range(n_sub):
        t0 = (i * n_sub + u) * tq
        qslab = q_ref[0, u * tq:(u + 1) * tq, :]
        qp = jnp.concatenate([_dot(qslab, pin_ref[g]).astype(BF16) for g in range(gq)], axis=0)
        pos_r = t0 + lax.broadcasted_iota(jnp.int32, (rows, 1), 0) % tq
        s_c = _dot(qp, kct_ref[0, 0])
        blk = lax.broadcasted_iota(jnp.int32, (rows, nb), 1)
        ok_c = (blk + 1) * SEL_BLOCK - 1 <= pos_r
        m_c = jnp.max(jnp.where(ok_c, s_c, -jnp.inf), axis=1, keepdims=True)
        m_c = jnp.where(m_c == -jnp.inf, 0.0, m_c)
        p_c = jnp.where(ok_c, jnp.exp2(s_c - m_c), 0.0)
        p_c = p_c / jnp.maximum(jnp.sum(p_c, axis=1, keepdims=True), 1e-30)
        o_c = _dot(p_c.astype(BF16), vcp_ref[0, 0]).astype(BF16)
        for g in range(gq):
            oc_ref[0, 0, g, u * tq:(u + 1) * tq, :] = o_c[g * tq:(g + 1) * tq]

        imp = p_c[0:tq]
        for g in range(1, gq):
            imp = imp + p_c[g * tq:(g + 1) * tq]
        imp_t = imp.T
        n_io = lax.broadcasted_iota(jnp.int32, (nb, tq), 0)
        cur = (t0 + lax.broadcasted_iota(jnp.int32, (nb, tq), 1)) // SEL_BLOCK
        forced = (n_io == 0) | (n_io == cur) | (n_io == cur - 1)
        v = jnp.where(forced, -2.0, jnp.where(n_io > cur, -1.0, imp_t))
        for _ in range(N_SEL - N_FORCED):
            mx = jnp.max(v, axis=0, keepdims=True)
            first = jnp.min(jnp.where(v == mx, n_io, nb), axis=0, keepdims=True)
            v = jnp.where(n_io == first, -2.0, v)
        bias_t = jnp.where((v == -2.0) & (n_io <= cur), 0.0, NEG)
        bias_ref[0, 0, u * tq:(u + 1) * tq, :] = bias_t.T.astype(BF16)


def _select(q, kct, vcp, pin, tq, n_sub):
    b, t, d = q.shape
    nb = kct.shape[3]
    gq = d // (N_KV_HEADS * HEAD_DIM)
    sw = gq * HEAD_DIM
    ts = tq * n_sub
    kern = functools.partial(_select_kernel, tq=tq, n_sub=n_sub, nb=nb)
    per_head4 = lambda bi, k, i: (bi, k, 0, 0)
    return pl.pallas_call(
        kern,
        out_shape=(jax.ShapeDtypeStruct((b, N_KV_HEADS, t, nb), BF16),
                   jax.ShapeDtypeStruct((b, N_KV_HEADS, gq, t, LANES), BF16)),
        grid=(b, N_KV_HEADS, t // ts),
        in_specs=[pl.BlockSpec((1, ts, sw), lambda bi, k, i: (bi, i, k)),
                  pl.BlockSpec((1, 1, 2 * HEAD_DIM, nb), per_head4),
                  pl.BlockSpec((1, 1, nb, LANES), per_head4),
                  pl.BlockSpec((gq, sw, LANES), lambda bi, k, i: (0, 0, 0))],
        out_specs=(pl.BlockSpec((1, 1, ts, nb), lambda bi, k, i: (bi, k, i, 0)),
                   pl.BlockSpec((1, 1, gq, ts, LANES), lambda bi, k, i: (bi, k, 0, i, 0))),
        compiler_params=_cparams(("parallel", "parallel", "parallel")),
        name="cmp_select",
    )(q, kct, vcp, pin)


def _attn_kernel(q_ref, gate_ref, bias_ref, oc_ref, kst_ref, vs_ref, kwt_ref, vw_ref, e_ref, pin_ref, pout_ref,
                 dmask_ref, wmask_ref, o_ref, qa_ref, m_ref, acc_ref, s_ref, *, tq, ck, ck_big, nb):
    gq = q_ref.shape[2] // HEAD_DIM
    rows = gq * tq
    i = pl.program_id(2)
    t0 = i * tq

    qslab = q_ref[0]
    qp = jnp.concatenate([_dot(qslab, pin_ref[g]).astype(BF16) for g in range(gq)], axis=0)
    qa_ref[...] = jnp.concatenate([jnp.concatenate([bias_ref[0, 0]] * gq, axis=0), qp], axis=1)

    m_ref[...] = jnp.full(m_ref.shape, NEG * 4.0, F32)

    def score_chunk(k0, width, causal):
        kaug = jnp.concatenate([e_ref[:, pl.ds(k0, width)], kst_ref[0, 0, :, pl.ds(k0, width)]], axis=0)
        s = _dot(qa_ref[...], kaug)
        if causal:
            s = s + jnp.concatenate([dmask_ref[0]] * gq, axis=0)
        s_ref[:, pl.ds(k0, width)] = s
        m = m_ref[...]
        for a in range(width // LANES):
            m = jnp.maximum(m, s[:, a * LANES:(a + 1) * LANES])
        m_ref[...] = m

    def value_chunk(k0, width):
        mb = m_ref[...]
        p = jnp.exp2(s_ref[:, pl.ds(k0, width)] - jnp.concatenate([mb] * (width // LANES), axis=1))
        acc_ref[...] += _dot(p.astype(BF16), vs_ref[0, 0, pl.ds(k0, width), :])

    c_last = t0 // ck
    per_big = ck_big // ck
    n_big = c_last // per_big
    n_small = c_last - n_big * per_big
    small0 = n_big * ck_big

    def loop(n, fn):
        def body(c, carry):
            fn(c)
            return carry
        lax.fori_loop(0, n, body, 0)

    loop(n_big, lambda c: score_chunk(pl.multiple_of(c * ck_big, ck_big), ck_big, False))
    loop(n_small, lambda c: score_chunk(pl.multiple_of(small0 + c * ck, ck), ck, False))
    score_chunk(pl.multiple_of(c_last * ck, ck), ck, True)

    m_row = jnp.max(m_ref[...], axis=1, keepdims=True)
    m_ref[...] = jnp.broadcast_to(m_row, m_ref.shape)
    acc_ref[...] = jnp.zeros_like(acc_ref)
    loop(n_big, lambda c: value_chunk(pl.multiple_of(c * ck_big, ck_big), ck_big))
    loop(n_small + 1, lambda c: value_chunk(pl.multiple_of(small0 + c * ck, ck), ck))
    acc = acc_ref[...]
    o_s = acc / acc[:, HEAD_DIM:HEAD_DIM + 1]

    wl = WINDOW + tq
    w0 = pl.multiple_of(jnp.maximum(t0 - WINDOW, 0), LANES)
    s_w = _dot(qp, kwt_ref[0, 0, :, pl.ds(w0, wl)]) + jnp.concatenate([wmask_ref[0]] * gq, axis=0)
    p_w = jnp.exp2(s_w - jnp.max(s_w, axis=1, keepdims=True))
    acc_w = _dot(p_w.astype(BF16), vw_ref[0, 0, pl.ds(w0, wl), :])
    o_w = acc_w / acc_w[:, HEAD_DIM:HEAD_DIM + 1]

    o_c = oc_ref[0, 0].reshape(rows, LANES).astype(F32)
    gates = gate_ref[0, 0]
    out = jnp.zeros((tq, gq * HEAD_DIM), F32)
    for g in range(gq):
        r = slice(g * tq, (g + 1) * tq)
        mix = (gates[:, 3 * g:3 * g + 1] * o_c[r] + gates[:, 3 * g + 1:3 * g + 2] * o_s[r]
               + gates[:, 3 * g + 2:3 * g + 3] * o_w[r])
        out = out + _dot(mix.astype(BF16), pout_ref[g])
    o_ref[0] = out.astype(o_ref.dtype)


def _attention(q, gates, bias, oc, kst, vs, kwt, vw, e_mat, pin, pout, tq, ck, ck_big):
    b, t, d = q.shape
    nb = bias.shape[3]
    gq = d // (N_KV_HEADS * HEAD_DIM)
    sw = gq * HEAD_DIM
    rows = gq * tq
    kern = functools.partial(_attn_kernel, tq=tq, ck=ck, ck_big=ck_big, nb=nb)
    per_head4 = lambda bi, k, i: (bi, k, 0, 0)
    once = pl.Buffered(1)
    tt = jnp.arange(tq)[None, :, None]
    n_phase = ck // tq
    jd = jnp.arange(ck)[None, None, :]
    dmask = jnp.where(jd <= jnp.arange(n_phase)[:, None, None] * tq + tt, 0.0, NEG).astype(F32)
    wl = WINDOW + tq
    n_early = WINDOW // tq
    jw = jnp.arange(wl)[None, None, :]
    pos_e = jnp.arange(n_early)[:, None, None] * tq + tt
    early = (jw <= pos_e) & (pos_e - jw < WINDOW)
    steady = (jw > tt) & (jw <= tt + WINDOW)
    wmask = jnp.where(jnp.concatenate([early, steady], axis=0), 0.0, NEG).astype(F32)
    return pl.pallas_call(
        kern,
        out_shape=jax.ShapeDtypeStruct((b, t, d), BF16),
        grid=(b, N_KV_HEADS, t // tq),
        in_specs=[pl.BlockSpec((1, tq, sw), lambda bi, k, i: (bi, i, k)),
                  pl.BlockSpec((1, 1, tq, LANES), lambda bi, k, i: (bi, k, i, 0)),
                  pl.BlockSpec((1, 1, tq, nb), lambda bi, k, i: (bi, k, i, 0)),
                  pl.BlockSpec((1, 1, gq, tq, LANES), lambda bi, k, i: (bi, k, 0, i, 0)),
                  pl.BlockSpec((1, 1, 2 * HEAD_DIM, t), per_head4, pipeline_mode=once),
                  pl.BlockSpec((1, 1, t, LANES), per_head4, pipeline_mode=once),
                  pl.BlockSpec((1, 1, 2 * HEAD_DIM, t), per_head4, pipeline_mode=once),
                  pl.BlockSpec((1, 1, t, LANES), per_head4, pipeline_mode=once),
                  pl.BlockSpec((nb, t), lambda bi, k, i: (0, 0), pipeline_mode=once),
                  pl.BlockSpec((gq, sw, LANES), lambda bi, k, i: (0, 0, 0)),
                  pl.BlockSpec((gq, LANES, sw), lambda bi, k, i: (0, 0, 0)),
                  pl.BlockSpec((1, tq, ck), lambda bi, k, i: (i % n_phase, 0, 0)),
                  pl.BlockSpec((1, tq, wl), lambda bi, k, i: (jnp.minimum(i, n_early), 0, 0))],
        out_specs=pl.BlockSpec((1, tq, sw), lambda bi, k, i: (bi, i, k)),
        scratch_shapes=[pltpu.VMEM((rows, nb + LANES), BF16),
                        pltpu.VMEM((rows, LANES), F32),
                        pltpu.VMEM((rows, LANES), F32),
                        pltpu.VMEM((rows, t), F32)],
        compiler_params=_cparams(("parallel", "parallel", "arbitrary")),
        name="nsa_attention",
    )(q, gates, bias, oc, kst, vs, kwt, vw, e_mat, pin, pout, dmask, wmask)


def _row_kvh(shape, gq):
    return lax.broadcasted_iota(jnp.int32, shape, 0) // gq


def _dec_a_kernel(q_ref, kct_ref, vcp_ref, oc_ref, idx_ref, *, pos, nb, gq, bb):
    nh = q_ref.shape[1]
    nbp = kct_ref.shape[3]
    rk = _row_kvh((nh, nbp), gq)
    rk_o = _row_kvh((nh, LANES), gq)
    blk = lax.broadcasted_iota(jnp.int32, (nh, nbp), 1)
    ok = ((blk + 1) * SEL_BLOCK - 1 <= pos) & (blk < nb)
    imps = []
    for lb in range(bb):
        qb = q_ref[lb].astype(BF16)
        s = jnp.zeros((nh, nbp), F32)
        for k in range(N_KV_HEADS):
            s = jnp.where(rk == k, _dot(qb, kct_ref[lb, k]), s)
        m = jnp.max(jnp.where(ok, s, -jnp.inf), axis=1, keepdims=True)
        m = jnp.where(m == -jnp.inf, 0.0, m)
        p = jnp.where(ok, jnp.exp(s - m), 0.0)
        p = p / jnp.maximum(jnp.sum(p, axis=1, keepdims=True), 1e-30)
        pb = p.astype(BF16)
        o_c = jnp.zeros((nh, LANES), F32)
        for k in range(N_KV_HEADS):
            o_c = jnp.where(rk_o == k, _dot(pb, vcp_ref[lb, k]), o_c)
        oc_ref[lb] = o_c
        imps += [jnp.sum(jnp.where(rk == k, p, 0.0), axis=0, keepdims=True) for k in range(N_KV_HEADS)]
    imp = jnp.concatenate(imps, axis=0)
    nr = bb * N_KV_HEADS
    n_io = lax.broadcasted_iota(jnp.int32, (nr, nbp), 1)
    cur = pos // SEL_BLOCK
    forced = (n_io == 0) | (n_io == cur) | (n_io == cur - 1)
    v = jnp.where(forced, FORCE_SCORE, jnp.where(n_io > cur, -1.0, imp))
    v = jnp.where(n_io < nb, v, -3.0)
    col = lax.broadcasted_iota(jnp.int32, (nr, N_SEL), 1)
    idx = jnp.zeros((nr, N_SEL), jnp.int32)
    for r in range(N_SEL):
        mx = jnp.max(v, axis=1, keepdims=True)
        first = jnp.min(jnp.where(v == mx, n_io, nbp), axis=1, keepdims=True)
        idx = jnp.where(col == r, first, idx)
        v = jnp.where(n_io == first, -4.0, v)
    idx_ref[...] = idx


def _decode_a(q_pad, kct, vcp, pos, nb, gq, bb):
    b, nh, _ = q_pad.shape
    nbp = kct.shape[3]
    kern = functools.partial(_dec_a_kernel, pos=pos, nb=nb, gq=gq, bb=bb)
    return pl.pallas_call(
        kern,
        out_shape=(jax.ShapeDtypeStruct((b, nh, LANES), F32),
                   jax.ShapeDtypeStruct((b * N_KV_HEADS, N_SEL), jnp.int32)),
        grid=(b // bb,),
        in_specs=[pl.BlockSpec((bb, nh, LANES), lambda bi: (bi, 0, 0)),
                  pl.BlockSpec((bb, N_KV_HEADS, 2 * HEAD_DIM, nbp), lambda bi: (bi, 0, 0, 0)),
                  pl.BlockSpec((bb, N_KV_HEADS, nbp, LANES), lambda bi: (bi, 0, 0, 0))],
        out_specs=(pl.BlockSpec((bb, nh, LANES), lambda bi: (bi, 0, 0)),
                   pl.BlockSpec((bb * N_KV_HEADS, N_SEL), lambda bi: (bi, 0))),
        compiler_params=_cparams(("parallel",)),
        name="decode_cmp_topk",
    )(q_pad, kct, vcp)


def _dec_b_kernel(idx_ref, pt_ref, q_ref, gate_ref, oc_ref, new_ref, cache_ref, swin_ref,
                  o_ref, kbuf_ref, vbuf_ref, sem_ref, *, pos, nb_past, gq, n_b):
    nh = q_ref.shape[1]
    b = pl.program_id(0)
    cur = b % 2
    nkeys = N_SEL * PAGE_SIZE

    def copies(b_, slot_, k, j):
        n = idx_ref[b_, k * N_SEL + j]
        page = pt_ref[b_, lax.shift_right_logical(jnp.minimum(n, nb_past - 1), 1)]
        ck = pltpu.make_async_copy(cache_ref.at[page, 2, k], kbuf_ref.at[slot_, k, j], sem_ref.at[slot_])
        cv = pltpu.make_async_copy(cache_ref.at[page, 3, k], vbuf_ref.at[slot_, k, j], sem_ref.at[slot_])
        return n, ck, cv

    def issue(b_, slot_):
        for k in range(N_KV_HEADS):
            for j in range(N_SEL):
                n, ck, cv = copies(b_, slot_, k, j)

                @pl.when(n < nb_past)
                def _():
                    ck.start()
                    cv.start()

    @pl.when(b == 0)
    def _():
        issue(b, 0)

    @pl.when(b + 1 < n_b)
    def _():
        issue(b + 1, 1 - cur)

    q = q_ref[0]
    qb = q.astype(BF16)
    new = new_ref[0]
    eye = (lax.broadcasted_iota(jnp.int32, (HEAD_DIM, HEAD_DIM), 0)
           == lax.broadcasted_iota(jnp.int32, (HEAD_DIM, HEAD_DIM), 1))
    col0 = (lax.broadcasted_iota(jnp.int32, (HEAD_DIM, PAGE_SIZE), 1) == 0).astype(BF16)
    lane = lax.broadcasted_iota(jnp.int32, (1, nkeys), 1)
    r_in = lane % PAGE_SIZE
    rk = _row_kvh((nh, nkeys), gq)
    rk_h = _row_kvh((nh, HEAD_DIM), gq)

    kpos = jnp.zeros((nh, nkeys), jnp.int32)
    half = jnp.zeros((nh, nkeys), jnp.int32)
    for k in range(N_KV_HEADS):
        tail_k = _dot(jnp.where(eye, new[0, k * gq:k * gq + 1, :], 0.0).astype(BF16), col0)
        tail_v = _dot(jnp.where(eye, new[1, k * gq:k * gq + 1, :], 0.0).astype(BF16), col0)
        kp = r_in
        hv = jnp.zeros((1, nkeys), jnp.int32)
        for j in range(N_SEL):
            n, ck, cv = copies(b, cur, k, j)

            @pl.when(n < nb_past)
            def _():
                ck.wait()
                cv.wait()

            @pl.when(n >= nb_past)
            def _():
                kbuf_ref[cur, k, j] = tail_k
                vbuf_ref[cur, k, j] = tail_v

            in_j = lane // PAGE_SIZE == j
            kp = kp + jnp.where(in_j, lax.shift_right_logical(n, 1) * PAGE_SIZE, 0)
            hv = hv + jnp.where(in_j, n & 1, 0)
        kpos = jnp.where(rk == k, kp, kpos)
        half = jnp.where(rk == k, hv, half)
    s = jnp.zeros((nh, nkeys), F32)
    for k in range(N_KV_HEADS):
        sk = jnp.concatenate([_dot(qb, kbuf_ref[cur, k, j].astype(BF16)) for j in range(N_SEL)], axis=1)
        s = jnp.where(rk == k, sk, s)
    ok = (r_in // SEL_BLOCK == half) & (kpos <= pos)
    m = jnp.max(jnp.where(ok, s, -jnp.inf), axis=1, keepdims=True)
    p = jnp.where(ok, jnp.exp(s - m), 0.0)
    p = p / jnp.maximum(jnp.sum(p, axis=1, keepdims=True), 1e-30)
    pb = p.astype(BF16)
    o_s = jnp.zeros((nh, HEAD_DIM), F32)
    for k in range(N_KV_HEADS):
        o_k = jnp.zeros((nh, HEAD_DIM), F32)
        for j in range(N_SEL):
            o_k = o_k + _dot_nt(pb[:, j * PAGE_SIZE:(j + 1) * PAGE_SIZE], vbuf_ref[cur, k, j].astype(BF16))
        o_s = jnp.where(rk_h == k, o_k, o_s)

    w_buf = swin_ref.shape[4]
    rk_w = _row_kvh((nh, w_buf), gq)
    s_w = jnp.zeros((nh, w_buf), F32)
    for k in range(N_KV_HEADS):
        s_w = jnp.where(rk_w == k, _dot(qb, swin_ref[0, 0, k].astype(BF16)), s_w)
    s_n = jnp.sum(q * new[2], axis=1, keepdims=True)
    ridx = lax.broadcasted_iota(jnp.int32, (nh, w_buf), 1)
    ok_w = w_buf - ridx < WINDOW
    m_w = jnp.maximum(jnp.max(jnp.where(ok_w, s_w, -jnp.inf), axis=1, keepdims=True), s_n)
    p_w = jnp.where(ok_w, jnp.exp(s_w - m_w), 0.0)
    p_n = jnp.exp(s_n - m_w)
    l_w = jnp.sum(p_w, axis=1, keepdims=True) + p_n
    pwb = (p_w / l_w).astype(BF16)
    o_w = jnp.zeros((nh, HEAD_DIM), F32)
    for k in range(N_KV_HEADS):
        o_w = jnp.where(rk_h == k, _dot_nt(pwb, swin_ref[0, 1, k].astype(BF16)), o_w)
    o_w = o_w + (p_n / l_w) * new[3]

    gates = gate_ref[0]
    o_ref[0] = gates[:, 0:1] * oc_ref[0][:, :HEAD_DIM] + gates[:, 1:2] * o_s + gates[:, 2:3] * o_w


def _decode_b(idx_flat, page_table, q3, gates, o_c, new_h, cache_t, state_t, pos, gq):
    b, nh, _ = q3.shape
    w_buf = state_t.shape[4]
    nb_past = page_table.shape[1] * (PAGE_SIZE // SEL_BLOCK)
    kern = functools.partial(_dec_b_kernel, pos=pos, nb_past=nb_past, gq=gq, n_b=b)
    grid_spec = pltpu.PrefetchScalarGridSpec(
        num_scalar_prefetch=2,
        grid=(b,),
        in_specs=[pl.BlockSpec((1, nh, HEAD_DIM), lambda bi, ix, pt: (bi, 0, 0)),
                  pl.BlockSpec((1, nh, LANES), lambda bi, ix, pt: (bi, 0, 0)),
                  pl.BlockSpec((1, nh, LANES), lambda bi, ix, pt: (bi, 0, 0)),
                  pl.BlockSpec((1, 4, nh, HEAD_DIM), lambda bi, ix, pt: (bi, 0, 0, 0)),
                  pl.BlockSpec(memory_space=pl.ANY),
                  pl.BlockSpec((1, 2, N_KV_HEADS, HEAD_DIM, w_buf), lambda bi, ix, pt: (bi, 0, 0, 0, 0))],
        out_specs=pl.BlockSpec((1, nh, HEAD_DIM), lambda bi, ix, pt: (bi, 0, 0)),
        scratch_shapes=[pltpu.VMEM((2, N_KV_HEADS, N_SEL, HEAD_DIM, PAGE_SIZE), F32),
                        pltpu.VMEM((2, N_KV_HEADS, N_SEL, HEAD_DIM, PAGE_SIZE), F32),
                        pltpu.SemaphoreType.DMA((2,))])
    return pl.pallas_call(
        kern,
        out_shape=jax.ShapeDtypeStruct((b, nh, HEAD_DIM), F32),
        grid_spec=grid_spec,
        compiler_params=_cparams(("arbitrary",)),
        name="decode_sel_win",
    )(idx_flat, page_table, q3, gates, o_c, new_h, cache_t, state_t)


def _rope_tables(pos):
    half = ROT_DIM // 2
    inv = jnp.power(jnp.float32(ROPE_THETA), -jnp.arange(half, dtype=F32) * 2.0 / ROT_DIM)
    ang = pos.astype(F32)[:, None] * inv[None, :]
    cos, sin = jnp.cos(ang), jnp.sin(ang)
    r = jnp.arange(LANES) % HEAD_DIM
    f = r % half
    cos_t = jnp.where(r < ROT_DIM, cos[:, f], 1.0)
    sin_up = jnp.where(r < half, -sin[:, f], 0.0)
    sin_dn = jnp.where((r >= half) & (r < ROT_DIM), sin[:, f], 0.0)
    return cos_t, sin_up, sin_dn


def _tile_heads(v, n):
    return jnp.tile(v.astype(F32), n)[None, :]


def kernel(x_prompt, x_sample, cache_kv, state_kv_win, state_pool, page_table, c_prompt, c_sample, ada_w, ada_b, norm_mix, norm_ffn, pool_w, pool_scale, ada_kv_w, ada_kv_b, norm_kv, w_kv, k_norm, cmp_pe, cmp_w1, cmp_w2, w_qg, q_norm, w_o, w_gate_up, w_down):
    b_p, seq, d = x_prompt.shape
    b_s, dec_seq, _ = x_sample.shape
    depth = ada_w.shape[0]
    n_a = pool_w.shape[0]
    n_heads = d // HEAD_DIM
    gq = n_heads // N_KV_HEADS
    hq = N_KV_HEADS * HEAD_DIM
    n_pages = page_table.shape[1]
    past_len = n_pages * PAGE_SIZE
    w_buf = state_kv_win.shape[1]
    d_ff = w_down.shape[1]
    assert dec_seq == 1 and hq == 2 * LANES and seq % PAGE_SIZE == 0
    sm = HEAD_DIM ** -0.5

    w_gu_b = w_gate_up.astype(BF16)
    w_dn_b = w_down.astype(BF16)
    w_kv_b = w_kv.astype(BF16)
    pool_w_b = pool_w.astype(BF16)
    w_q_b = w_qg[:, :, :d].astype(BF16)
    wg_cols = w_qg[:, :, d:].reshape(-1, d, N_KV_HEADS, 3 * gq)
    w_g_b = jnp.pad(wg_cols, ((0, 0), (0, 0), (0, 0), (0, LANES - 3 * gq))).reshape(-1, d, N_KV_HEADS * LANES).astype(BF16)
    w_o_b = w_o.astype(BF16)

    head_of = jnp.arange(hq) // HEAD_DIM
    seg = (head_of[:, None] == head_of[None, :]).astype(BF16)
    lane_in = jnp.arange(hq)
    lane_out = jnp.arange(N_KV_HEADS * LANES)
    place = ((lane_out[None, :] // LANES == lane_in[:, None] // HEAD_DIM)
             & (lane_out[None, :] % LANES == lane_in[:, None] % HEAD_DIM)).astype(BF16)
    one_row = (lane_out % LANES == HEAD_DIM).astype(F32)[None, :]
    sw = gq * HEAD_DIM
    cin = jnp.arange(sw)
    pin = jnp.stack([((cin[:, None] // HEAD_DIM == g) & (jnp.arange(LANES)[None, :] == cin[:, None] % HEAD_DIM))
                     for g in range(gq)]).astype(BF16)
    pout = jnp.stack([((jnp.arange(LANES)[:, None] < HEAD_DIM)
                       & (cin[None, :] == g * HEAD_DIM + jnp.arange(LANES)[:, None]))
                      for g in range(gq)]).astype(BF16)

    kn = jnp.stack([_tile_heads(k_norm[1], N_KV_HEADS), _tile_heads(k_norm[2], N_KV_HEADS)])
    kn0 = _tile_heads(k_norm[0], N_KV_HEADS)
    pe_t = jnp.tile(cmp_pe.transpose(0, 2, 1), (1, 1, 2))
    eye2 = jnp.eye(2, dtype=F32)
    w1_dsj = cmp_w1.reshape(2, SEL_BLOCK, HEAD_DIM, -1).transpose(0, 2, 1, 3)
    n_hid = w1_dsj.shape[-1]
    w1cat = jnp.einsum('ab,zdsj->zdasbj', eye2, w1_dsj).reshape(2, HEAD_DIM // 2, 2 * 2 * SEL_BLOCK, 2 * n_hid).astype(BF16)
    w2bd = jnp.einsum('ab,zje->zajbe', eye2, cmp_w2).reshape(2, 2 * n_hid, 2 * HEAD_DIM).astype(BF16)

    c_all = jnp.concatenate([c_prompt, c_sample], axis=0)
    m_all = c_all.shape[0]
    m_pad = -(-m_all // 8) * 8
    c_all = jnp.pad(c_all, ((0, m_pad - m_all), (0, 0)))
    mods = _mods(c_all, ada_w.reshape(depth * 2, d, 3 * d), ada_b.reshape(depth * 2, 1, 3 * d)).reshape(depth, 2, m_pad, 3 * d)
    mod_kv = _mods(c_all, ada_kv_w[None], ada_kv_b[None, None, :])[0]

    def mod_p(l, j):
        return mods[l, j, :b_p][:, None, :]

    def mod_s_tok(l, j):
        return mods[l, j, b_p:m_all][None]

    def mod_s_seq(l, j):
        return mods[l, j, b_p:m_all][:, None, :]

    tm = min(512, seq)
    tf = d_ff // 2 if (d_ff // 2) % LANES == 0 else d_ff
    tq = 256
    tq_sel = min(1024, seq)
    ck = 512
    ck_big = min(2048, seq // 2)

    x = x_prompt
    pool_p = []
    for l in range(n_a):
        x, npool = _pool_layer(x, jnp.zeros((b_p, POOL_HALO, d), F32), mod_p(l, 0), norm_mix[l][None], pool_w_b[l],
                               pool_scale[l][None], 0, tm)
        pool_p.append(npool[:, 1:])
        x = _ffn(x, mod_p(l, 1), norm_ffn[l][None], w_gu_b[l], w_dn_b[l], tm, tf)

    tabs_p = _rope_tables(jnp.arange(seq))
    rows_t, win_t, kst, vs, kwt, vw = _kv_proj(x, mod_kv[:b_p][:, None, :], norm_kv[None], w_kv_b, kn, seg, tabs_p,
                                               place, one_row, tm, True)
    nb_p = seq // SEL_BLOCK
    pt_p = jnp.zeros((b_p, seq // PAGE_SIZE), jnp.int32)
    raw_p = _compress(pt_p, rows_t, pe_t, w1cat, w2bd, False)
    tabs_blk_p = _rope_tables((jnp.arange(nb_p) + 1) * SEL_BLOCK - 1)
    kct_p, vcp_p = _cmp_finish(raw_p[0], raw_p[1], kn0, seg, tabs_blk_p, place)
    e_mat = (jnp.arange(seq)[None, :] // SEL_BLOCK == jnp.arange(nb_p)[:, None]).astype(BF16)
    for l in range(n_a, depth):
        j = l - n_a
        q, gates = _q_proj(x, mod_p(l, 0), norm_mix[l][None], w_q_b[j], w_g_b[j], _tile_heads(q_norm[j], N_KV_HEADS),
                           seg, tabs_p, tm, sm * LOG2E, BF16)
        bias, o_cmp = _select(q, kct_p, vcp_p, pin, tq_sel, 1)
        o = _attention(q, gates, bias, o_cmp, kst, vs, kwt, vw, e_mat, pin, pout, tq, ck, ck_big)
        x = _o_proj(x, o, mod_p(l, 0), w_o_b[j], tm)
        x = _ffn(x, mod_p(l, 1), norm_ffn[l][None], w_gu_b[l], w_dn_b[l], tm, tf)
    y_prompt = x
    kv_rows_prompt = rows_t.transpose(0, 4, 1, 2, 3)
    win_keep_p = min(WINDOW, seq)
    win_prompt = win_t[..., seq - win_keep_p:].transpose(0, 4, 1, 2, 3)
    pool_prompt = jnp.stack(pool_p)

    pos_s = past_len
    xs = x_sample
    pool_s = []
    for l in range(n_a):
        pre = jnp.pad(state_pool[l], ((0, 0), (POOL_HALO - state_pool.shape[2], 0), (0, 0)))
        xs, npool = _pool_layer(xs, pre, mod_s_seq(l, 0), norm_mix[l][None], pool_w_b[l], pool_scale[l][None], pos_s, 1)
        pool_s.append(npool[:, 1:])
        xs = _ffn(xs.reshape(1, b_s, d), mod_s_tok(l, 1), norm_ffn[l][None], w_gu_b[l], w_dn_b[l], b_s, tf).reshape(b_s, 1, d)
    xt = xs.reshape(1, b_s, d)
    tabs_s = _rope_tables(jnp.full((b_s,), pos_s))
    rows_s, win_s = _kv_proj(xt, mod_kv[b_p:m_all][None], norm_kv[None], w_kv_b, kn, seg, tabs_s, place, one_row,
                             b_s, False)
    rows_s = rows_s.reshape(b_s, 4, N_KV_HEADS, HEAD_DIM)
    win_s = win_s.reshape(b_s, 2, N_KV_HEADS, HEAD_DIM)

    cache_t = cache_kv.transpose(0, 2, 3, 4, 1)
    state_t = state_kv_win.transpose(0, 2, 3, 4, 1)
    raw_s = _compress(page_table, cache_t, pe_t, w1cat, w2bd, True)
    nb_s = -(-(past_len + 1) // SEL_BLOCK)
    z_tail = jnp.pad(rows_s[:, :2].transpose(1, 0, 2, 3).reshape(2 * b_s * N_KV_HEADS, HEAD_DIM),
                     ((0, 0), (0, (SEL_BLOCK - 1) * HEAD_DIM)))
    raw_tail = _compress_tail(z_tail, cmp_pe.reshape(2, 1, SEL_BLOCK * HEAD_DIM), cmp_w1.astype(BF16),
                              cmp_w2.astype(BF16)).reshape(2, b_s, 1, hq)
    nbp_s = -(-nb_s // LANES) * LANES
    raw_all = jnp.pad(jnp.concatenate([raw_s, raw_tail], axis=2), ((0, 0), (0, 0), (0, nbp_s - nb_s), (0, 0)))
    tabs_blk_s = _rope_tables((jnp.arange(nbp_s) + 1) * SEL_BLOCK - 1)
    kct_s, vcp_s = _cmp_finish(raw_all[0], raw_all[1], kn0, seg, tabs_blk_s, place)

    new_h = jnp.repeat(jnp.concatenate([rows_s[:, 2:4], win_s], axis=1), gq, axis=2)
    bb = math.gcd(b_s, 8)
    for l in range(n_a, depth):
        j = l - n_a
        q_s, gates_s = _q_proj(xt, mod_s_tok(l, 0), norm_mix[l][None], w_q_b[j], w_g_b[j],
                               _tile_heads(q_norm[j], N_KV_HEADS), seg, tabs_s, b_s, sm, F32)
        q3 = q_s.reshape(b_s, n_heads, HEAD_DIM)
        q_cmp = jnp.pad(q3, ((0, 0), (0, 0), (0, LANES - HEAD_DIM)))
        g3 = gates_s[0, :, :, :3 * gq].transpose(1, 0, 2).reshape(b_s, n_heads, 3)
        g3 = jnp.pad(g3, ((0, 0), (0, 0), (0, LANES - 3)))
        o_c, idx = _decode_a(q_cmp, kct_s, vcp_s, pos_s, nb_s, gq, bb)
        o_s = _decode_b(idx.reshape(b_s, N_KV_HEADS * N_SEL), page_table, q3, g3, o_c, new_h, cache_t, state_t,
                        pos_s, gq)
        xt = _o_proj(xt, o_s.reshape(1, b_s, d), mod_s_tok(l, 0), w_o_b[j], b_s)
        xt = _ffn(xt, mod_s_tok(l, 1), norm_ffn[l][None], w_gu_b[l], w_dn_b[l], b_s, tf)
    y_sample = xt.reshape(b_s, 1, d)
    kv_rows_sample = rows_s.reshape(b_s, 1, 4, N_KV_HEADS, HEAD_DIM)
    win_sample = jnp.concatenate([state_t[..., 1:], win_s[..., None]], axis=-1).transpose(0, 4, 1, 2, 3)
    pool_sample = jnp.stack(pool_s)
    return (y_prompt, y_sample, kv_rows_prompt, kv_rows_sample, win_prompt, win_sample, pool_prompt, pool_sample)
```

```python
import functools
import math

import jax
import jax.numpy as jnp
from jax import lax
from jax.experimental import pallas as pl
from jax.experimental.pallas import tpu as pltpu

F32 = jnp.float32
BF16 = jnp.bfloat16

POOL_WINDOWS = (2, 4, 8, 16)
POOL_HALO = 16
HEAD_DIM = 64
N_KV_HEADS = 4
ROT_DIM = 16
ROPE_THETA = 500000.0
SEL_BLOCK = 64
N_SEL = 16
N_FORCED = 3
WINDOW = 512
PAGE_SIZE = 128
FORCE_SCORE = 1.0e4
EPS = 1e-6
NEG = -float(2 ** 30)
LOG2E = 1.4426950408889634
LANES = 128
TILE_ROW_STRIDE = 72
VMEM_LIMIT = 56 * 1024 * 1024
NT_DIMS = (((1,), (1,)), ((), ()))


def _cparams(sem):
    return pltpu.CompilerParams(dimension_semantics=sem, vmem_limit_bytes=VMEM_LIMIT)


def _dot(a, b):
    return jnp.dot(a, b, preferred_element_type=F32)


def _dot_nt(a, b):
    return lax.dot_general(a, b, NT_DIMS, preferred_element_type=F32)


def _split(a):
    hi = a.astype(BF16)
    lo = (a - hi.astype(F32)).astype(BF16)
    return hi, lo


def _adaln(x, g, shift, scale):
    ms = jnp.mean(x * x, axis=-1, keepdims=True)
    return x * lax.rsqrt(ms + EPS) * g * (1.0 + scale) + shift


def _head_rms(x, gain, seg):
    hi, lo = _split(x * x)
    ss = _dot(hi, seg) + _dot(lo, seg)
    return x * lax.rsqrt(ss * (1.0 / HEAD_DIM) + EPS) * gain


def _rope(x, cos, sin_up, sin_dn):
    outs = []
    for a in range(x.shape[1] // LANES):
        xa = x[:, a * LANES:(a + 1) * LANES]
        up = pltpu.roll(xa, LANES - ROT_DIM // 2, 1)
        dn = pltpu.roll(xa, ROT_DIM // 2, 1)
        outs.append(xa * cos + up * sin_up + dn * sin_dn)
    return outs[0] if len(outs) == 1 else jnp.concatenate(outs, axis=1)


def _gelu_tanh(x):
    return x * (0.5 * (1.0 + jnp.tanh(math.sqrt(2.0 / math.pi) * (x + 0.044715 * (x * x * x)))))


def _mods_kernel(c_ref, w_ref, b_ref, o_ref):
    ch, cl = _split(c_ref[...])
    wh, wl = _split(w_ref[0])
    o_ref[0] = _dot(ch, wh) + _dot(ch, wl) + _dot(cl, wh) + b_ref[0]


def _mods(c_all, w, b, tn=1024):
    n_l, d, n = w.shape
    m = c_all.shape[0]
    return pl.pallas_call(
        _mods_kernel,
        out_shape=jax.ShapeDtypeStruct((n_l, m, n), F32),
        grid=(n_l, n // tn),
        in_specs=[pl.BlockSpec((m, d), lambda l, j: (0, 0)),
                  pl.BlockSpec((1, d, tn), lambda l, j: (l, 0, j)),
                  pl.BlockSpec((1, 1, tn), lambda l, j: (l, 0, j))],
        out_specs=pl.BlockSpec((1, m, tn), lambda l, j: (l, 0, j)),
        compiler_params=_cparams(("parallel", "parallel")),
        name="mods",
    )(c_all, w, b)


def _pool_kernel(x_ref, pre_ref, mod_ref, g_ref, pw_ref, ps_ref, o_ref, np_ref, hb_ref, *, tt, pos0, d):
    i = pl.program_id(1)
    x = x_ref[0]
    mod = mod_ref[0]
    h = _adaln(x, g_ref[...], mod[:, :d], mod[:, d:2 * d])

    @pl.when(i == 0)
    def _():
        hb_ref[0:POOL_HALO, :] = pre_ref[0]

    hb_ref[POOL_HALO:POOL_HALO + tt, :] = h
    pos = pos0 + i * tt + lax.broadcasted_iota(jnp.int32, (tt, 1), 0)
    pg = d // len(POOL_WINDOWS)
    ys = []
    for gi, w in enumerate(POOL_WINDOWS):
        c0 = gi * pg
        hg = h[:, c0:c0 + pg]
        s = hg
        for j in range(1, w):
            s = s + hb_ref[POOL_HALO - j:POOL_HALO - j + tt, c0:c0 + pg]
        cnt = jnp.minimum(w, pos + 1).astype(F32)
        pooled = s / cnt - hg
        ys.append(_dot(pooled.astype(BF16), pw_ref[gi]))
    y = jnp.concatenate(ys, axis=1) * ps_ref[...]
    o_ref[0] = x + mod[:, 2 * d:] * y
    last = hb_ref[tt:tt + POOL_HALO, :]
    np_ref[0] = last
    hb_ref[0:POOL_HALO, :] = last


def _pool_layer(x, prefix16, mod, g, pw, ps, pos0, tt):
    b, t, d = x.shape
    kern = functools.partial(_pool_kernel, tt=tt, pos0=pos0, d=d)
    pg = d // len(POOL_WINDOWS)
    return pl.pallas_call(
        kern,
        out_shape=(jax.ShapeDtypeStruct((b, t, d), F32), jax.ShapeDtypeStruct((b, POOL_HALO, d), F32)),
        grid=(b, t // tt),
        in_specs=[pl.BlockSpec((1, tt, d), lambda bi, i: (bi, i, 0)),
                  pl.BlockSpec((1, POOL_HALO, d), lambda bi, i: (bi, 0, 0)),
                  pl.BlockSpec((1, 1, 3 * d), lambda bi, i: (bi, 0, 0)),
                  pl.BlockSpec((1, d), lambda bi, i: (0, 0)),
                  pl.BlockSpec((len(POOL_WINDOWS), pg, pg), lambda bi, i: (0, 0, 0)),
                  pl.BlockSpec((1, d), lambda bi, i: (0, 0))],
        out_specs=(pl.BlockSpec((1, tt, d), lambda bi, i: (bi, i, 0)),
                   pl.BlockSpec((1, POOL_HALO, d), lambda bi, i: (bi, 0, 0))),
        scratch_shapes=[pltpu.VMEM((POOL_HALO + max(tt, 8), d), F32)],
        compiler_params=_cparams(("parallel", "arbitrary")),
        name="pool_layer",
    )(x, prefix16, mod, g, pw, ps)


def _ffn_kernel(x_ref, mod_ref, g_ref, wgu_ref, wd_ref, *rest, d, tf, with_attn):
    ff = wd_ref.shape[0]
    x = x_ref[0]
    if with_attn:
        a_ref, amod_ref, wo_ref, o_ref = rest
        x = x + amod_ref[0][:, 2 * d:] * _dot(a_ref[0].astype(BF16), wo_ref[...])
    else:
        (o_ref,) = rest
    mod = mod_ref[0]
    hb = _adaln(x, g_ref[...], mod[:, :d], mod[:, d:2 * d]).astype(BF16)
    acc = jnp.zeros(x.shape, F32)
    for f in range(ff // tf):
        g = _dot(hb, wgu_ref[:, f * tf:(f + 1) * tf])
        u = _dot(hb, wgu_ref[:, ff + f * tf:ff + (f + 1) * tf])
        a = (g * (1.0 / (1.0 + jnp.exp(-g)))) * u
        acc = acc + _dot(a.astype(BF16), wd_ref[f * tf:(f + 1) * tf, :])
    o_ref[0] = x + mod[:, 2 * d:] * acc


def _ffn(x, mod, g, w_gu, w_dn, tm, tf, attn=None):
    b, t, d = x.shape
    rm = mod.shape[1]
    ff = w_dn.shape[0]
    kern = functools.partial(_ffn_kernel, d=d, tf=tf, with_attn=attn is not None)
    once = pl.Buffered(1)
    operands = [x, mod, g, w_gu, w_dn]
    in_specs = [pl.BlockSpec((1, tm, d), lambda bi, i: (bi, i, 0)),
                pl.BlockSpec((1, rm, 3 * d), lambda bi, i: (bi, 0, 0)),
                pl.BlockSpec((1, d), lambda bi, i: (0, 0)),
                pl.BlockSpec((d, 2 * ff), lambda bi, i: (0, 0), pipeline_mode=once),
                pl.BlockSpec((ff, d), lambda bi, i: (0, 0), pipeline_mode=once)]
    if attn is not None:
        operands += list(attn)
        in_specs += [pl.BlockSpec((1, tm, d), lambda bi, i: (bi, i, 0)),
                     pl.BlockSpec((1, rm, 3 * d), lambda bi, i: (bi, 0, 0)),
                     pl.BlockSpec((d, d), lambda bi, i: (0, 0), pipeline_mode=once)]
    return pl.pallas_call(
        kern,
        out_shape=jax.ShapeDtypeStruct((b, t, d), F32),
        grid=(b, t // tm),
        in_specs=in_specs,
        out_specs=pl.BlockSpec((1, tm, d), lambda bi, i: (bi, i, 0)),
        compiler_params=_cparams(("parallel", "parallel")),
        name="ffn",
    )(*operands)


def _kv_kernel(x_ref, mod_ref, g_ref, w_ref, kn_ref, seg_ref, cos_ref, su_ref, sd_ref, place_ref, one_ref,
               *out_refs, d, transposed):
    hq = N_KV_HEADS * HEAD_DIM
    mod = mod_ref[0]
    h = _adaln(x_ref[0], g_ref[...], mod[:, :d], mod[:, d:2 * d])
    proj = _dot(h.astype(BF16), w_ref[...])
    seg = seg_ref[...]
    cos, su, sd = cos_ref[...], su_ref[...], sd_ref[...]
    k_sel = _rope(_head_rms(proj[:, 2 * hq:3 * hq], kn_ref[0], seg), cos, su, sd)
    k_win = _rope(_head_rms(proj[:, 4 * hq:5 * hq], kn_ref[1], seg), cos, su, sd)
    v_sel = proj[:, 3 * hq:4 * hq]
    v_win = proj[:, 5 * hq:6 * hq]
    if not transposed:
        rows_ref, win_ref = out_refs
        rows_ref[0] = jnp.concatenate([proj[:, :2 * hq], k_sel, v_sel], axis=1)
        win_ref[0] = jnp.concatenate([k_win, v_win], axis=1)
        return
    rows_t_ref, win_t_ref, kst_ref, vs_ref, kwt_ref, vw_ref = out_refs
    tm = k_sel.shape[0]
    zeros = jnp.zeros((HEAD_DIM, tm), BF16)
    k_sel_t = k_sel.T
    k_win_t = k_win.T
    for t_ref, slabs in ((rows_t_ref, (proj[:, :hq].T, proj[:, hq:2 * hq].T, k_sel_t, v_sel.T)),
                         (win_t_ref, (k_win_t, v_win.T))):
        for sl, slab in enumerate(slabs):
            for k in range(N_KV_HEADS):
                t_ref[0, sl, k] = slab[k * HEAD_DIM:(k + 1) * HEAD_DIM, :]
    for kt_ref, kt32 in ((kst_ref, k_sel_t), (kwt_ref, k_win_t)):
        kt = kt32.astype(BF16)
        for k in range(N_KV_HEADS):
            kt_ref[0, k, 0:HEAD_DIM, :] = kt[k * HEAD_DIM:(k + 1) * HEAD_DIM, :]
            kt_ref[0, k, HEAD_DIM:2 * HEAD_DIM, :] = zeros
    for va_ref, vv in ((vs_ref, v_sel), (vw_ref, v_win)):
        va = (_dot(vv.astype(BF16), place_ref[...]) + one_ref[...]).astype(BF16)
        for k in range(N_KV_HEADS):
            va_ref[0, k] = va[:, k * LANES:(k + 1) * LANES]


def _kv_proj(x, mod, g, w_kv, kn, seg, tabs, place, one_row, tm, transposed):
    b, t, d = x.shape
    rm = mod.shape[1]
    hq = N_KV_HEADS * HEAD_DIM
    kern = functools.partial(_kv_kernel, d=d, transposed=transposed)
    const2 = lambda bi, i: (0, 0)
    tab_spec = pl.BlockSpec((tm, LANES), lambda bi, i: (i, 0))
    if transposed:
        kt_shape = jax.ShapeDtypeStruct((b, N_KV_HEADS, 2 * HEAD_DIM, t), BF16)
        va_shape = jax.ShapeDtypeStruct((b, N_KV_HEADS, t, LANES), BF16)
        kt_spec = pl.BlockSpec((1, N_KV_HEADS, 2 * HEAD_DIM, tm), lambda bi, i: (bi, 0, 0, i))
        va_spec = pl.BlockSpec((1, N_KV_HEADS, tm, LANES), lambda bi, i: (bi, 0, i, 0))
        out_shape = [jax.ShapeDtypeStruct((b, 4, N_KV_HEADS, HEAD_DIM, t), F32),
                     jax.ShapeDtypeStruct((b, 2, N_KV_HEADS, HEAD_DIM, t), F32), kt_shape, va_shape, kt_shape, va_shape]
        out_specs = [pl.BlockSpec((1, 4, N_KV_HEADS, HEAD_DIM, tm), lambda bi, i: (bi, 0, 0, 0, i)),
                     pl.BlockSpec((1, 2, N_KV_HEADS, HEAD_DIM, tm), lambda bi, i: (bi, 0, 0, 0, i)),
                     kt_spec, va_spec, kt_spec, va_spec]
    else:
        out_shape = [jax.ShapeDtypeStruct((b, t, 4 * hq), F32), jax.ShapeDtypeStruct((b, t, 2 * hq), F32)]
        out_specs = [pl.BlockSpec((1, tm, 4 * hq), lambda bi, i: (bi, i, 0)),
                     pl.BlockSpec((1, tm, 2 * hq), lambda bi, i: (bi, i, 0))]
    return pl.pallas_call(
        kern,
        out_shape=tuple(out_shape),
        grid=(b, t // tm),
        in_specs=[pl.BlockSpec((1, tm, d), lambda bi, i: (bi, i, 0)),
                  pl.BlockSpec((1, rm, 2 * d), lambda bi, i: (bi, 0, 0)),
                  pl.BlockSpec((1, d), const2),
                  pl.BlockSpec((d, 6 * hq), const2),
                  pl.BlockSpec((2, 1, hq), lambda bi, i: (0, 0, 0)),
                  pl.BlockSpec((hq, hq), const2),
                  tab_spec, tab_spec, tab_spec,
                  pl.BlockSpec((hq, N_KV_HEADS * LANES), const2),
                  pl.BlockSpec((1, N_KV_HEADS * LANES), const2)],
        out_specs=tuple(out_specs),
        compiler_params=_cparams(("parallel", "parallel")),
        name="kv_proj",
    )(x, mod, g, w_kv, kn, seg, *tabs, place, one_row)


def _compress_kernel(pt_ref, src_ref, pe_ref, w1_ref, w2_ref, o_ref, buf_ref, sem_ref, *, n_pages, n_b, paged):
    n_tiles = n_pages * N_KV_HEADS
    sl = pl.program_id(0)
    b = pl.program_id(1)
    step = sl * n_b + b
    cur = step % 2

    def tile_copy(sl_, b_, slot_, p, h):
        if paged:
            src = src_ref.at[pt_ref[b_, p], sl_, h]
        else:
            src = src_ref.at[b_, sl_, h, :, pl.ds(p * PAGE_SIZE, PAGE_SIZE)]
        ti = p * N_KV_HEADS + h
        return pltpu.make_async_copy(src, buf_ref.at[slot_, pl.ds(ti * TILE_ROW_STRIDE, HEAD_DIM), :],
                                     sem_ref.at[slot_])

    def issue(sl_, b_, slot_):
        for p in range(n_pages):
            for h in range(N_KV_HEADS):
                tile_copy(sl_, b_, slot_, p, h).start()

    @pl.when(step == 0)
    def _():
        issue(sl, b, 0)

    @pl.when(step + 1 < 2 * n_b)
    def _():
        nxt = step + 1
        issue(nxt // n_b, nxt % n_b, 1 - cur)

    for p in range(n_pages):
        for h in range(N_KV_HEADS):
            tile_copy(sl, b, cur, p, h).wait()

    pe = pe_ref[0]
    hid = jnp.zeros((n_tiles, 2 * LANES), F32)
    for dp in range(HEAD_DIM // 2):
        xs = [buf_ref[cur, pl.ds(dd, n_tiles, stride=TILE_ROW_STRIDE), :] + pe[dd:dd + 1, :]
              for dd in (2 * dp, 2 * dp + 1)]
        hid = hid + _dot(jnp.concatenate(xs, axis=1).astype(BF16), w1_ref[0, dp])
    o_ref[0, 0] = _dot(_gelu_tanh(hid).astype(BF16), w2_ref[0])


def _compress(page_table, src, pe_t, w1cat, w2bd, paged):
    n_b, n_pages = page_table.shape
    n_tiles = n_pages * N_KV_HEADS
    kern = functools.partial(_compress_kernel, n_pages=n_pages, n_b=n_b, paged=paged)
    grid_spec = pltpu.PrefetchScalarGridSpec(
        num_scalar_prefetch=1,
        grid=(2, n_b),
        in_specs=[pl.BlockSpec(memory_space=pl.ANY),
                  pl.BlockSpec((1, HEAD_DIM, LANES), lambda s, b, pt: (s, 0, 0)),
                  pl.BlockSpec((1, HEAD_DIM // 2, 2 * LANES, 2 * LANES), lambda s, b, pt: (s, 0, 0, 0)),
                  pl.BlockSpec((1, 2 * LANES, LANES), lambda s, b, pt: (s, 0, 0))],
        out_specs=pl.BlockSpec((1, 1, n_tiles, LANES), lambda s, b, pt: (s, b, 0, 0)),
        scratch_shapes=[pltpu.VMEM((2, n_tiles * TILE_ROW_STRIDE, LANES), F32),
                        pltpu.SemaphoreType.DMA((2,))])
    raw = pl.pallas_call(
        kern,
        out_shape=jax.ShapeDtypeStruct((2, n_b, n_tiles, LANES), F32),
        grid_spec=grid_spec,
        compiler_params=_cparams(("arbitrary", "arbitrary")),
        name="compress",
    )(page_table, src, pe_t, w1cat, w2bd)
    raw = raw.reshape(2, n_b, n_pages, N_KV_HEADS, 2, HEAD_DIM).transpose(0, 1, 2, 4, 3, 5)
    return raw.reshape(2, n_b, 2 * n_pages, N_KV_HEADS * HEAD_DIM)


def _tail_kernel(z_ref, pe_ref, w1_ref, w2_ref, o_ref):
    z = z_ref[...] + pe_ref[0]
    hid = _gelu_tanh(_dot(z.astype(BF16), w1_ref[0]))
    o_ref[0] = _dot(hid.astype(BF16), w2_ref[0])


def _compress_tail(z, pe_flat, w1, w2):
    m, kdim = z.shape[0] // 2, z.shape[1]
    hid = w1.shape[2]
    return pl.pallas_call(
        _tail_kernel,
        out_shape=jax.ShapeDtypeStruct((2, m, HEAD_DIM), F32),
        grid=(2,),
        in_specs=[pl.BlockSpec((m, kdim), lambda s: (s, 0)),
                  pl.BlockSpec((1, 1, kdim), lambda s: (s, 0, 0)),
                  pl.BlockSpec((1, kdim, hid), lambda s: (s, 0, 0)),
                  pl.BlockSpec((1, hid, HEAD_DIM), lambda s: (s, 0, 0))],
        out_specs=pl.BlockSpec((1, m, HEAD_DIM), lambda s: (s, 0, 0)),
        compiler_params=_cparams(("parallel",)),
        name="compress_tail",
    )(z, pe_flat, w1, w2)


def _cmp_finish_kernel(kc_ref, vc_ref, kn_ref, seg_ref, cos_ref, su_ref, sd_ref, place_ref, kt_ref, vp_ref):
    kc = _rope(_head_rms(kc_ref[0], kn_ref[...], seg_ref[...]), cos_ref[...], su_ref[...], sd_ref[...])
    kt = kc.T.astype(BF16)
    nbp = kc.shape[0]
    zeros = jnp.zeros((HEAD_DIM, nbp), BF16)
    vp = _dot(vc_ref[0].astype(BF16), place_ref[...]).astype(BF16)
    for k in range(N_KV_HEADS):
        kt_ref[0, k, 0:HEAD_DIM, :] = kt[k * HEAD_DIM:(k + 1) * HEAD_DIM, :]
        kt_ref[0, k, HEAD_DIM:2 * HEAD_DIM, :] = zeros
        vp_ref[0, k] = vp[:, k * LANES:(k + 1) * LANES]


def _cmp_finish(kc_raw, vc_raw, kn0, seg, tabs, place):
    b, nbp, hq = kc_raw.shape
    const2 = lambda bi: (0, 0)
    tab_spec = pl.BlockSpec((nbp, LANES), const2)
    return pl.pallas_call(
        _cmp_finish_kernel,
        out_shape=(jax.ShapeDtypeStruct((b, N_KV_HEADS, 2 * HEAD_DIM, nbp), BF16),
                   jax.ShapeDtypeStruct((b, N_KV_HEADS, nbp, LANES), BF16)),
        grid=(b,),
        in_specs=[pl.BlockSpec((1, nbp, hq), lambda bi: (bi, 0, 0)),
                  pl.BlockSpec((1, nbp, hq), lambda bi: (bi, 0, 0)),
                  pl.BlockSpec((1, hq), const2),
                  pl.BlockSpec((hq, hq), const2),
                  tab_spec, tab_spec, tab_spec,
                  pl.BlockSpec((hq, N_KV_HEADS * LANES), const2)],
        out_specs=(pl.BlockSpec((1, N_KV_HEADS, 2 * HEAD_DIM, nbp), lambda bi: (bi, 0, 0, 0)),
                   pl.BlockSpec((1, N_KV_HEADS, nbp, LANES), lambda bi: (bi, 0, 0, 0))),
        compiler_params=_cparams(("parallel",)),
        name="cmp_finish",
    )(kc_raw, vc_raw, kn0, seg, *tabs, place)


def _q_kernel(x_ref, mod_ref, g_ref, wq_ref, wg_ref, qn_ref, seg_ref, cos_ref, su_ref, sd_ref,
              q_ref, gate_ref, *, d, q_scale):
    hq = N_KV_HEADS * HEAD_DIM
    mod = mod_ref[0]
    hb = _adaln(x_ref[0], g_ref[...], mod[:, :d], mod[:, d:2 * d]).astype(BF16)
    q = _dot(hb, wq_ref[...])
    seg = seg_ref[...]
    cos, su, sd = cos_ref[...], su_ref[...], sd_ref[...]
    for c in range(d // hq):
        qc = _rope(_head_rms(q[:, c * hq:(c + 1) * hq], qn_ref[...], seg), cos, su, sd) * q_scale
        q_ref[0, :, c * hq:(c + 1) * hq] = qc.astype(q_ref.dtype)
    gates = 1.0 / (1.0 + jnp.exp(-_dot(hb, wg_ref[...])))
    for k in range(N_KV_HEADS):
        gate_ref[0, k] = gates[:, k * LANES:(k + 1) * LANES]


def _q_proj(x, mod, g, w_q, w_g, qn, seg, tabs, tm, q_scale, q_dtype):
    b, t, d = x.shape
    rm = mod.shape[1]
    hq = N_KV_HEADS * HEAD_DIM
    ng = w_g.shape[1]
    kern = functools.partial(_q_kernel, d=d, q_scale=q_scale)
    const2 = lambda bi, i: (0, 0)
    tab_spec = pl.BlockSpec((tm, LANES), lambda bi, i: (i, 0))
    return pl.pallas_call(
        kern,
        out_shape=(jax.ShapeDtypeStruct((b, t, d), q_dtype),
                   jax.ShapeDtypeStruct((b, N_KV_HEADS, t, LANES), F32)),
        grid=(b, t // tm),
        in_specs=[pl.BlockSpec((1, tm, d), lambda bi, i: (bi, i, 0)),
                  pl.BlockSpec((1, rm, 3 * d), lambda bi, i: (bi, 0, 0)),
                  pl.BlockSpec((1, d), const2),
                  pl.BlockSpec((d, d), const2),
                  pl.BlockSpec((d, ng), const2),
                  pl.BlockSpec((1, hq), const2),
                  pl.BlockSpec((hq, hq), const2),
                  tab_spec, tab_spec, tab_spec],
        out_specs=(pl.BlockSpec((1, tm, d), lambda bi, i: (bi, i, 0)),
                   pl.BlockSpec((1, N_KV_HEADS, tm, LANES), lambda bi, i: (bi, 0, i, 0))),
        compiler_params=_cparams(("parallel", "parallel")),
        name="q_proj",
    )(x, mod, g, w_q, w_g, qn, seg, *tabs)


def _select_kernel(q_ref, kct_ref, vcp_ref, pin_ref, bias_ref, oc_ref, *, tq, n_sub, nb):
    gq = q_ref.shape[2] // HEAD_DIM
    rows = gq * tq
    i = pl.program_id(2)
    for u in range(n_sub):
        t0 = (i * n_sub + u) * tq
        qslab = q_ref[0, u * tq:(u + 1) * tq, :]
        qp = jnp.concatenate([_dot(qslab, pin_ref[g]).astype(BF16) for g in range(gq)], axis=0)
        pos_r = t0 + lax.broadcasted_iota(jnp.int32, (rows, 1), 0) % tq
        s_c = _dot(qp, kct_ref[0, 0])
        blk = lax.broadcasted_iota(jnp.int32, (rows, nb), 1)
        ok_c = (blk + 1) * SEL_BLOCK - 1 <= pos_r
        m_c = jnp.max(jnp.where(ok_c, s_c, -jnp.inf), axis=1, keepdims=True)
        m_c = jnp.where(m_c == -jnp.inf, 0.0, m_c)
        p_c = jnp.where(ok_c, jnp.exp2(s_c - m_c), 0.0)
        p_c = p_c / jnp.maximum(jnp.sum(p_c, axis=1, keepdims=True), 1e-30)
        o_c = _dot(p_c.astype(BF16), vcp_ref[0, 0]).astype(BF16)
        for g in range(gq):
            oc_ref[0, 0, g, u * tq:(u + 1) * tq, :] = o_c[g * tq:(g + 1) * tq]

        imp = p_c[0:tq]
        for g in range(1, gq):
            imp = imp + p_c[g * tq:(g + 1) * tq]
        imp_t = imp.T
        n_io = lax.broadcasted_iota(jnp.int32, (nb, tq), 0)
        cur = (t0 + lax.broadcasted_iota(jnp.int32, (nb, tq), 1)) // SEL_BLOCK
        forced = (n_io == 0) | (n_io == cur) | (n_io == cur - 1)
        v = jnp.where(forced, -2.0, jnp.where(n_io > cur, -1.0, imp_t))
        for _ in range(N_SEL - N_FORCED):
            mx = jnp.max(v, axis=0, keepdims=True)
            first = jnp.min(jnp.where(v == mx, n_io, nb), axis=0, keepdims=True)
            v = jnp.where(n_io == first, -2.0, v)
        bias_t = jnp.where((v == -2.0) & (n_io <= cur), 0.0, NEG)
        bias_ref[0, 0, u * tq:(u + 1) * tq, :] = bias_t.T.astype(BF16)


def _select(q, kct, vcp, pin, tq, n_sub):
    b, t, d = q.shape
    nb = kct.shape[3]
    gq = d // (N_KV_HEADS * HEAD_DIM)
    sw = gq * HEAD_DIM
    ts = tq * n_sub
    kern = functools.partial(_select_kernel, tq=tq, n_sub=n_sub, nb=nb)
    per_head4 = lambda bi, k, i: (bi, k, 0, 0)
    return pl.pallas_call(
        kern,
        out_shape=(jax.ShapeDtypeStruct((b, N_KV_HEADS, t, nb), BF16),
                   jax.ShapeDtypeStruct((b, N_KV_HEADS, gq, t, LANES), BF16)),
        grid=(b, N_KV_HEADS, t // ts),
        in_specs=[pl.BlockSpec((1, ts, sw), lambda bi, k, i: (bi, i, k)),
                  pl.BlockSpec((1, 1, 2 * HEAD_DIM, nb), per_head4),
                  pl.BlockSpec((1, 1, nb, LANES), per_head4),
                  pl.BlockSpec((gq, sw, LANES), lambda bi, k, i: (0, 0, 0))],
        out_specs=(pl.BlockSpec((1, 1, ts, nb), lambda bi, k, i: (bi, k, i, 0)),
                   pl.BlockSpec((1, 1, gq, ts, LANES), lambda bi, k, i: (bi, k, 0, i, 0))),
        compiler_params=_cparams(("parallel", "parallel", "parallel")),
        name="cmp_select",
    )(q, kct, vcp, pin)


def _attn_kernel(q_ref, gate_ref, bias_ref, oc_ref, kst_ref, vs_ref, kwt_ref, vw_ref, e_ref, pin_ref, pout_ref,
                 dmask_ref, wmask_ref, o_ref, qa_ref, m_ref, acc_ref, s_ref, *, tq, ck, ck_big, nb):
    gq = q_ref.shape[2] // HEAD_DIM
    rows = gq * tq
    i = pl.program_id(2)
    t0 = i * tq

    qslab = q_ref[0]
    qp = jnp.concatenate([_dot(qslab, pin_ref[g]).astype(BF16) for g in range(gq)], axis=0)
    qa_ref[...] = jnp.concatenate([jnp.concatenate([bias_ref[0, 0]] * gq, axis=0), qp], axis=1)

    m_ref[...] = jnp.full(m_ref.shape, NEG * 4.0, F32)

    def score_chunk(k0, width, causal):
        kaug = jnp.concatenate([e_ref[:, pl.ds(k0, width)], kst_ref[0, 0, :, pl.ds(k0, width)]], axis=0)
        s = _dot(qa_ref[...], kaug)
        if causal:
            s = s + jnp.concatenate([dmask_ref[0]] * gq, axis=0)
        s_ref[:, pl.ds(k0, width)] = s
        m = m_ref[...]
        for a in range(width // LANES):
            m = jnp.maximum(m, s[:, a * LANES:(a + 1) * LANES])
        m_ref[...] = m

    def value_chunk(k0, width):
        mb = m_ref[...]
        p = jnp.exp2(s_ref[:, pl.ds(k0, width)] - jnp.concatenate([mb] * (width // LANES), axis=1))
        acc_ref[...] += _dot(p.astype(BF16), vs_ref[0, 0, pl.ds(k0, width), :])

    c_last = t0 // ck
    per_big = ck_big // ck
    n_big = c_last // per_big
    n_small = c_last - n_big * per_big
    small0 = n_big * ck_big

    def loop(n, fn):
        def body(c, carry):
            fn(c)
            return carry
        lax.fori_loop(0, n, body, 0)

    loop(n_big, lambda c: score_chunk(pl.multiple_of(c * ck_big, ck_big), ck_big, False))
    loop(n_small, lambda c: score_chunk(pl.multiple_of(small0 + c * ck, ck), ck, False))
    score_chunk(pl.multiple_of(c_last * ck, ck), ck, True)

    m_row = jnp.max(m_ref[...], axis=1, keepdims=True)
    m_ref[...] = jnp.broadcast_to(m_row, m_ref.shape)
    acc_ref[...] = jnp.zeros_like(acc_ref)
    loop(n_big, lambda c: value_chunk(pl.multiple_of(c * ck_big, ck_big), ck_big))
    loop(n_small + 1, lambda c: value_chunk(pl.multiple_of(small0 + c * ck, ck), ck))
    acc = acc_ref[...]
    o_s = acc / acc[:, HEAD_DIM:HEAD_DIM + 1]

    wl = WINDOW + tq
    w0 = pl.multiple_of(jnp.maximum(t0 - WINDOW, 0), LANES)
    s_w = _dot(qp, kwt_ref[0, 0, :, pl.ds(w0, wl)]) + jnp.concatenate([wmask_ref[0]] * gq, axis=0)
    p_w = jnp.exp2(s_w - jnp.max(s_w, axis=1, keepdims=True))
    acc_w = _dot(p_w.astype(BF16), vw_ref[0, 0, pl.ds(w0, wl), :])
    o_w = acc_w / acc_w[:, HEAD_DIM:HEAD_DIM + 1]

    o_c = oc_ref[0, 0].reshape(rows, LANES).astype(F32)
    gates = gate_ref[0, 0]
    out = jnp.zeros((tq, gq * HEAD_DIM), F32)
    for g in range(gq):
        r = slice(g * tq, (g + 1) * tq)
        mix = (gates[:, 3 * g:3 * g + 1] * o_c[r] + gates[:, 3 * g + 1:3 * g + 2] * o_s[r]
               + gates[:, 3 * g + 2:3 * g + 3] * o_w[r])
        out = out + _dot(mix.astype(BF16), pout_ref[g])
    o_ref[0] = out.astype(o_ref.dtype)


def _attention(q, gates, bias, oc, kst, vs, kwt, vw, e_mat, pin, pout, tq, ck, ck_big):
    b, t, d = q.shape
    nb = bias.shape[3]
    gq = d // (N_KV_HEADS * HEAD_DIM)
    sw = gq * HEAD_DIM
    rows = gq * tq
    kern = functools.partial(_attn_kernel, tq=tq, ck=ck, ck_big=ck_big, nb=nb)
    per_head4 = lambda bi, k, i: (bi, k, 0, 0)
    once = pl.Buffered(1)
    tt = jnp.arange(tq)[None, :, None]
    n_phase = ck // tq
    jd = jnp.arange(ck)[None, None, :]
    dmask = jnp.where(jd <= jnp.arange(n_phase)[:, None, None] * tq + tt, 0.0, NEG).astype(F32)
    wl = WINDOW + tq
    n_early = WINDOW // tq
    jw = jnp.arange(wl)[None, None, :]
    pos_e = jnp.arange(n_early)[:, None, None] * tq + tt
    early = (jw <= pos_e) & (pos_e - jw < WINDOW)
    steady = (jw > tt) & (jw <= tt + WINDOW)
    wmask = jnp.where(jnp.concatenate([early, steady], axis=0), 0.0, NEG).astype(F32)
    return pl.pallas_call(
        kern,
        out_shape=jax.ShapeDtypeStruct((b, t, d), BF16),
        grid=(b, N_KV_HEADS, t // tq),
        in_specs=[pl.BlockSpec((1, tq, sw), lambda bi, k, i: (bi, i, k)),
                  pl.BlockSpec((1, 1, tq, LANES), lambda bi, k, i: (bi, k, i, 0)),
                  pl.BlockSpec((1, 1, tq, nb), lambda bi, k, i: (bi, k, i, 0)),
                  pl.BlockSpec((1, 1, gq, tq, LANES), lambda bi, k, i: (bi, k, 0, i, 0)),
                  pl.BlockSpec((1, 1, 2 * HEAD_DIM, t), per_head4, pipeline_mode=once),
                  pl.BlockSpec((1, 1, t, LANES), per_head4, pipeline_mode=once),
                  pl.BlockSpec((1, 1, 2 * HEAD_DIM, t), per_head4, pipeline_mode=once),
                  pl.BlockSpec((1, 1, t, LANES), per_head4, pipeline_mode=once),
                  pl.BlockSpec((nb, t), lambda bi, k, i: (0, 0), pipeline_mode=once),
                  pl.BlockSpec((gq, sw, LANES), lambda bi, k, i: (0, 0, 0)),
                  pl.BlockSpec((gq, LANES, sw), lambda bi, k, i: (0, 0, 0)),
                  pl.BlockSpec((1, tq, ck), lambda bi, k, i: (i % n_phase, 0, 0)),
                  pl.BlockSpec((1, tq, wl), lambda bi, k, i: (jnp.minimum(i, n_early), 0, 0))],
        out_specs=pl.BlockSpec((1, tq, sw), lambda bi, k, i: (bi, i, k)),
        scratch_shapes=[pltpu.VMEM((rows, nb + LANES), BF16),
                        pltpu.VMEM((rows, LANES), F32),
                        pltpu.VMEM((rows, LANES), F32),
                        pltpu.VMEM((rows, t), F32)],
        compiler_params=_cparams(("parallel", "parallel", "arbitrary")),
        name="nsa_attention",
    )(q, gates, bias, oc, kst, vs, kwt, vw, e_mat, pin, pout, dmask, wmask)


def _row_kvh(shape, gq):
    return lax.broadcasted_iota(jnp.int32, shape, 0) // gq


def _dec_a_kernel(q_ref, kct_ref, vcp_ref, oc_ref, idx_ref, *, pos, nb, gq, bb):
    nh = q_ref.shape[1]
    nbp = kct_ref.shape[3]
    rk = _row_kvh((nh, nbp), gq)
    rk_o = _row_kvh((nh, LANES), gq)
    blk = lax.broadcasted_iota(jnp.int32, (nh, nbp), 1)
    ok = ((blk + 1) * SEL_BLOCK - 1 <= pos) & (blk < nb)
    imps = []
    for lb in range(bb):
        qb = q_ref[lb].astype(BF16)
        s = jnp.zeros((nh, nbp), F32)
        for k in range(N_KV_HEADS):
            s = jnp.where(rk == k, _dot(qb, kct_ref[lb, k]), s)
        m = jnp.max(jnp.where(ok, s, -jnp.inf), axis=1, keepdims=True)
        m = jnp.where(m == -jnp.inf, 0.0, m)
        p = jnp.where(ok, jnp.exp(s - m), 0.0)
        p = p / jnp.maximum(jnp.sum(p, axis=1, keepdims=True), 1e-30)
        pb = p.astype(BF16)
        o_c = jnp.zeros((nh, LANES), F32)
        for k in range(N_KV_HEADS):
            o_c = jnp.where(rk_o == k, _dot(pb, vcp_ref[lb, k]), o_c)
        oc_ref[lb] = o_c
        imps += [jnp.sum(jnp.where(rk == k, p, 0.0), axis=0, keepdims=True) for k in range(N_KV_HEADS)]
    imp = jnp.concatenate(imps, axis=0)
    nr = bb * N_KV_HEADS
    n_io = lax.broadcasted_iota(jnp.int32, (nr, nbp), 1)
    cur = pos // SEL_BLOCK
    forced = (n_io == 0) | (n_io == cur) | (n_io == cur - 1)
    v = jnp.where(forced, FORCE_SCORE, jnp.where(n_io > cur, -1.0, imp))
    v = jnp.where(n_io < nb, v, -3.0)
    col = lax.broadcasted_iota(jnp.int32, (nr, N_SEL), 1)
    idx = jnp.zeros((nr, N_SEL), jnp.int32)
    for r in range(N_SEL):
        mx = jnp.max(v, axis=1, keepdims=True)
        first = jnp.min(jnp.where(v == mx, n_io, nbp), axis=1, keepdims=True)
        idx = jnp.where(col == r, first, idx)
        v = jnp.where(n_io == first, -4.0, v)
    idx_ref[...] = idx


def _decode_a(q_pad, kct, vcp, pos, nb, gq, bb):
    b, nh, _ = q_pad.shape
    nbp = kct.shape[3]
    kern = functools.partial(_dec_a_kernel, pos=pos, nb=nb, gq=gq, bb=bb)
    return pl.pallas_call(
        kern,
        out_shape=(jax.ShapeDtypeStruct((b, nh, LANES), F32),
                   jax.ShapeDtypeStruct((b * N_KV_HEADS, N_SEL), jnp.int32)),
        grid=(b // bb,),
        in_specs=[pl.BlockSpec((bb, nh, LANES), lambda bi: (bi, 0, 0)),
                  pl.BlockSpec((bb, N_KV_HEADS, 2 * HEAD_DIM, nbp), lambda bi: (bi, 0, 0, 0)),
                  pl.BlockSpec((bb, N_KV_HEADS, nbp, LANES), lambda bi: (bi, 0, 0, 0))],
        out_specs=(pl.BlockSpec((bb, nh, LANES), lambda bi: (bi, 0, 0)),
                   pl.BlockSpec((bb * N_KV_HEADS, N_SEL), lambda bi: (bi, 0))),
        compiler_params=_cparams(("parallel",)),
        name="decode_cmp_topk",
    )(q_pad, kct, vcp)


def _dec_b_kernel(idx_ref, pt_ref, q_ref, gate_ref, oc_ref, new_ref, cache_ref, swin_ref,
                  o_ref, kbuf_ref, vbuf_ref, sem_ref, *, pos, nb_past, gq, n_b):
    nh = q_ref.shape[1]
    b = pl.program_id(0)
    cur = b % 2
    nkeys = N_SEL * PAGE_SIZE

    def copies(b_, slot_, k, j):
        n = idx_ref[b_, k * N_SEL + j]
        page = pt_ref[b_, lax.shift_right_logical(jnp.minimum(n, nb_past - 1), 1)]
        ck = pltpu.make_async_copy(cache_ref.at[page, 2, k], kbuf_ref.at[slot_, k, j], sem_ref.at[slot_])
        cv = pltpu.make_async_copy(cache_ref.at[page, 3, k], vbuf_ref.at[slot_, k, j], sem_ref.at[slot_])
        return n, ck, cv

    def issue(b_, slot_):
        for k in range(N_KV_HEADS):
            for j in range(N_SEL):
                n, ck, cv = copies(b_, slot_, k, j)

                @pl.when(n < nb_past)
                def _():
                    ck.start()
                    cv.start()

    @pl.when(b == 0)
    def _():
        issue(b, 0)

    @pl.when(b + 1 < n_b)
    def _():
        issue(b + 1, 1 - cur)

    q = q_ref[0]
    qb = q.astype(BF16)
    new = new_ref[0]
    eye = (lax.broadcasted_iota(jnp.int32, (HEAD_DIM, HEAD_DIM), 0)
           == lax.broadcasted_iota(jnp.int32, (HEAD_DIM, HEAD_DIM), 1))
    col0 = (lax.broadcasted_iota(jnp.int32, (HEAD_DIM, PAGE_SIZE), 1) == 0).astype(BF16)
    lane = lax.broadcasted_iota(jnp.int32, (1, nkeys), 1)
    r_in = lane % PAGE_SIZE
    rk = _row_kvh((nh, nkeys), gq)
    rk_h = _row_kvh((nh, HEAD_DIM), gq)

    kpos = jnp.zeros((nh, nkeys), jnp.int32)
    half = jnp.zeros((nh, nkeys), jnp.int32)
    for k in range(N_KV_HEADS):
        tail_k = _dot(jnp.where(eye, new[0, k * gq:k * gq + 1, :], 0.0).astype(BF16), col0)
        tail_v = _dot(jnp.where(eye, new[1, k * gq:k * gq + 1, :], 0.0).astype(BF16), col0)
        kp = r_in
        hv = jnp.zeros((1, nkeys), jnp.int32)
        for j in range(N_SEL):
            n, ck, cv = copies(b, cur, k, j)

            @pl.when(n < nb_past)
            def _():
                ck.wait()
                cv.wait()

            @pl.when(n >= nb_past)
            def _():
                kbuf_ref[cur, k, j] = tail_k
                vbuf_ref[cur, k, j] = tail_v

            in_j = lane // PAGE_SIZE == j
            kp = kp + jnp.where(in_j, lax.shift_right_logical(n, 1) * PAGE_SIZE, 0)
            hv = hv + jnp.where(in_j, n & 1, 0)
        kpos = jnp.where(rk == k, kp, kpos)
        half = jnp.where(rk == k, hv, half)
    s = jnp.zeros((nh, nkeys), F32)
    for k in range(N_KV_HEADS):
        sk = jnp.concatenate([_dot(qb, kbuf_ref[cur, k, j].astype(BF16)) for j in range(N_SEL)], axis=1)
        s = jnp.where(rk == k, sk, s)
    ok = (r_in // SEL_BLOCK == half) & (kpos <= pos)
    m = jnp.max(jnp.where(ok, s, -jnp.inf), axis=1, keepdims=True)
    p = jnp.where(ok, jnp.exp(s - m), 0.0)
    p = p / jnp.maximum(jnp.sum(p, axis=1, keepdims=True), 1e-30)
    pb = p.astype(BF16)
    o_s = jnp.zeros((nh, HEAD_DIM), F32)
    for k in range(N_KV_HEADS):
        o_k = jnp.zeros((nh, HEAD_DIM), F32)
        for j in range(N_SEL):
            o_k = o_k + _dot_nt(pb[:, j * PAGE_SIZE:(j + 1) * PAGE_SIZE], vbuf_ref[cur, k, j].astype(BF16))
        o_s = jnp.where(rk_h == k, o_k, o_s)

    w_buf = swin_ref.shape[4]
    rk_w = _row_kvh((nh, w_buf), gq)
    s_w = jnp.zeros((nh, w_buf), F32)
    for k in range(N_KV_HEADS):
        s_w = jnp.where(rk_w == k, _dot(qb, swin_ref[0, 0, k].astype(BF16)), s_w)
    s_n = jnp.sum(q * new[2], axis=1, keepdims=True)
    ridx = lax.broadcasted_iota(jnp.int32, (nh, w_buf), 1)
    ok_w = w_buf - ridx < WINDOW
    m_w = jnp.maximum(jnp.max(jnp.where(ok_w, s_w, -jnp.inf), axis=1, keepdims=True), s_n)
    p_w = jnp.where(ok_w, jnp.exp(s_w - m_w), 0.0)
    p_n = jnp.exp(s_n - m_w)
    l_w = jnp.sum(p_w, axis=1, keepdims=True) + p_n
    pwb = (p_w / l_w).astype(BF16)
    o_w = jnp.zeros((nh, HEAD_DIM), F32)
    for k in range(N_KV_HEADS):
        o_w = jnp.where(rk_h == k, _dot_nt(pwb, swin_ref[0, 1, k].astype(BF16)), o_w)
    o_w = o_w + (p_n / l_w) * new[3]

    gates = gate_ref[0]
    o_ref[0] = gates[:, 0:1] * oc_ref[0][:, :HEAD_DIM] + gates[:, 1:2] * o_s + gates[:, 2:3] * o_w


def _decode_b(idx_flat, page_table, q3, gates, o_c, new_h, cache_t, state_t, pos, gq):
    b, nh, _ = q3.shape
    w_buf = state_t.shape[4]
    nb_past = page_table.shape[1] * (PAGE_SIZE // SEL_BLOCK)
    kern = functools.partial(_dec_b_kernel, pos=pos, nb_past=nb_past, gq=gq, n_b=b)
    grid_spec = pltpu.PrefetchScalarGridSpec(
        num_scalar_prefetch=2,
        grid=(b,),
        in_specs=[pl.BlockSpec((1, nh, HEAD_DIM), lambda bi, ix, pt: (bi, 0, 0)),
                  pl.BlockSpec((1, nh, LANES), lambda bi, ix, pt: (bi, 0, 0)),
                  pl.BlockSpec((1, nh, LANES), lambda bi, ix, pt: (bi, 0, 0)),
                  pl.BlockSpec((1, 4, nh, HEAD_DIM), lambda bi, ix, pt: (bi, 0, 0, 0)),
                  pl.BlockSpec(memory_space=pl.ANY),
                  pl.BlockSpec((1, 2, N_KV_HEADS, HEAD_DIM, w_buf), lambda bi, ix, pt: (bi, 0, 0, 0, 0))],
        out_specs=pl.BlockSpec((1, nh, HEAD_DIM), lambda bi, ix, pt: (bi, 0, 0)),
        scratch_shapes=[pltpu.VMEM((2, N_KV_HEADS, N_SEL, HEAD_DIM, PAGE_SIZE), F32),
                        pltpu.VMEM((2, N_KV_HEADS, N_SEL, HEAD_DIM, PAGE_SIZE), F32),
                        pltpu.SemaphoreType.DMA((2,))])
    return pl.pallas_call(
        kern,
        out_shape=jax.ShapeDtypeStruct((b, nh, HEAD_DIM), F32),
        grid_spec=grid_spec,
        compiler_params=_cparams(("arbitrary",)),
        name="decode_sel_win",
    )(idx_flat, page_table, q3, gates, o_c, new_h, cache_t, state_t)


def _rope_tables(pos):
    half = ROT_DIM // 2
    inv = jnp.power(jnp.float32(ROPE_THETA), -jnp.arange(half, dtype=F32) * 2.0 / ROT_DIM)
    ang = pos.astype(F32)[:, None] * inv[None, :]
    cos, sin = jnp.cos(ang), jnp.sin(ang)
    r = jnp.arange(LANES) % HEAD_DIM
    f = r % half
    cos_t = jnp.where(r < ROT_DIM, cos[:, f], 1.0)
    sin_up = jnp.where(r < half, -sin[:, f], 0.0)
    sin_dn = jnp.where((r >= half) & (r < ROT_DIM), sin[:, f], 0.0)
    return cos_t, sin_up, sin_dn


def _tile_heads(v, n):
    return jnp.tile(v.astype(F32), n)[None, :]


def kernel(x_prompt, x_sample, cache_kv, state_kv_win, state_pool, page_table, c_prompt, c_sample, ada_w, ada_b, norm_mix, norm_ffn, pool_w, pool_scale, ada_kv_w, ada_kv_b, norm_kv, w_kv, k_norm, cmp_pe, cmp_w1, cmp_w2, w_qg, q_norm, w_o, w_gate_up, w_down):
    b_p, seq, d = x_prompt.shape
    b_s, dec_seq, _ = x_sample.shape
    depth = ada_w.shape[0]
    n_a = pool_w.shape[0]
    n_heads = d // HEAD_DIM
    gq = n_heads // N_KV_HEADS
    hq = N_KV_HEADS * HEAD_DIM
    n_pages = page_table.shape[1]
    past_len = n_pages * PAGE_SIZE
    w_buf = state_kv_win.shape[1]
    d_ff = w_down.shape[1]
    assert dec_seq == 1 and hq == 2 * LANES and seq % PAGE_SIZE == 0
    sm = HEAD_DIM ** -0.5

    w_gu_b = w_gate_up.astype(BF16)
    w_dn_b = w_down.astype(BF16)
    w_kv_b = w_kv.astype(BF16)
    pool_w_b = pool_w.astype(BF16)
    w_q_b = w_qg[:, :, :d].astype(BF16)
    wg_cols = w_qg[:, :, d:].reshape(-1, d, N_KV_HEADS, 3 * gq)
    w_g_b = jnp.pad(wg_cols, ((0, 0), (0, 0), (0, 0), (0, LANES - 3 * gq))).reshape(-1, d, N_KV_HEADS * LANES).astype(BF16)
    w_o_b = w_o.astype(BF16)

    head_of = jnp.arange(hq) // HEAD_DIM
    seg = (head_of[:, None] == head_of[None, :]).astype(BF16)
    lane_in = jnp.arange(hq)
    lane_out = jnp.arange(N_KV_HEADS * LANES)
    place = ((lane_out[None, :] // LANES == lane_in[:, None] // HEAD_DIM)
             & (lane_out[None, :] % LANES == lane_in[:, None] % HEAD_DIM)).astype(BF16)
    one_row = (lane_out % LANES == HEAD_DIM).astype(F32)[None, :]
    sw = gq * HEAD_DIM
    cin = jnp.arange(sw)
    pin = jnp.stack([((cin[:, None] // HEAD_DIM == g) & (jnp.arange(LANES)[None, :] == cin[:, None] % HEAD_DIM))
                     for g in range(gq)]).astype(BF16)
    pout = jnp.stack([((jnp.arange(LANES)[:, None] < HEAD_DIM)
                       & (cin[None, :] == g * HEAD_DIM + jnp.arange(LANES)[:, None]))
                      for g in range(gq)]).astype(BF16)

    kn = jnp.stack([_tile_heads(k_norm[1], N_KV_HEADS), _tile_heads(k_norm[2], N_KV_HEADS)])
    kn0 = _tile_heads(k_norm[0], N_KV_HEADS)
    pe_t = jnp.tile(cmp_pe.transpose(0, 2, 1), (1, 1, 2))
    eye2 = jnp.eye(2, dtype=F32)
    w1_dsj = cmp_w1.reshape(2, SEL_BLOCK, HEAD_DIM, -1).transpose(0, 2, 1, 3)
    n_hid = w1_dsj.shape[-1]
    w1cat = jnp.einsum('ab,zdsj->zdasbj', eye2, w1_dsj).reshape(2, HEAD_DIM // 2, 2 * 2 * SEL_BLOCK, 2 * n_hid).astype(BF16)
    w2bd = jnp.einsum('ab,zje->zajbe', eye2, cmp_w2).reshape(2, 2 * n_hid, 2 * HEAD_DIM).astype(BF16)

    c_all = jnp.concatenate([c_prompt, c_sample], axis=0)
    m_all = c_all.shape[0]
    m_pad = -(-m_all // 8) * 8
    c_all = jnp.pad(c_all, ((0, m_pad - m_all), (0, 0)))
    mods = _mods(c_all, ada_w.reshape(depth * 2, d, 3 * d), ada_b.reshape(depth * 2, 1, 3 * d)).reshape(depth, 2, m_pad, 3 * d)
    mod_kv = _mods(c_all, ada_kv_w[None], ada_kv_b[None, None, :])[0]

    def mod_p(l, j):
        return mods[l, j, :b_p][:, None, :]

    def mod_s_tok(l, j):
        return mods[l, j, b_p:m_all][None]

    def mod_s_seq(l, j):
        return mods[l, j, b_p:m_all][:, None, :]

    tm = min(512, seq)
    tf = d_ff // 2 if (d_ff // 2) % LANES == 0 else d_ff
    tq = 256
    tq_sel = min(1024, seq)
    ck = 512
    ck_big = min(2048, seq // 2)

    x = x_prompt
    pool_p = []
    for l in range(n_a):
        x, npool = _pool_layer(x, jnp.zeros((b_p, POOL_HALO, d), F32), mod_p(l, 0), norm_mix[l][None], pool_w_b[l],
                               pool_scale[l][None], 0, tm)
        pool_p.append(npool[:, 1:])
        x = _ffn(x, mod_p(l, 1), norm_ffn[l][None], w_gu_b[l], w_dn_b[l], tm, tf)

    tabs_p = _rope_tables(jnp.arange(seq))
    rows_t, win_t, kst, vs, kwt, vw = _kv_proj(x, mod_kv[:b_p][:, None, :], norm_kv[None], w_kv_b, kn, seg, tabs_p,
                                               place, one_row, tm, True)
    nb_p = seq // SEL_BLOCK
    pt_p = jnp.zeros((b_p, seq // PAGE_SIZE), jnp.int32)
    raw_p = _compress(pt_p, rows_t, pe_t, w1cat, w2bd, False)
    tabs_blk_p = _rope_tables((jnp.arange(nb_p) + 1) * SEL_BLOCK - 1)
    kct_p, vcp_p = _cmp_finish(raw_p[0], raw_p[1], kn0, seg, tabs_blk_p, place)
    e_mat = (jnp.arange(seq)[None, :] // SEL_BLOCK == jnp.arange(nb_p)[:, None]).astype(BF16)
    for l in range(n_a, depth):
        j = l - n_a
        q, gates = _q_proj(x, mod_p(l, 0), norm_mix[l][None], w_q_b[j], w_g_b[j], _tile_heads(q_norm[j], N_KV_HEADS),
                           seg, tabs_p, tm, sm * LOG2E, BF16)
        bias, o_cmp = _select(q, kct_p, vcp_p, pin, tq_sel, 1)
        o = _attention(q, gates, bias, o_cmp, kst, vs, kwt, vw, e_mat, pin, pout, tq, ck, ck_big)
        x = _ffn(x, mod_p(l, 1), norm_ffn[l][None], w_gu_b[l], w_dn_b[l], tm, tf, (o, mod_p(l, 0), w_o_b[j]))
    y_prompt = x
    kv_rows_prompt = rows_t.transpose(0, 4, 1, 2, 3)
    win_keep_p = min(WINDOW, seq)
    win_prompt = win_t[..., seq - win_keep_p:].transpose(0, 4, 1, 2, 3)
    pool_prompt = jnp.stack(pool_p)

    pos_s = past_len
    xs = x_sample
    pool_s = []
    for l in range(n_a):
        pre = jnp.pad(state_pool[l], ((0, 0), (POOL_HALO - state_pool.shape[2], 0), (0, 0)))
        xs, npool = _pool_layer(xs, pre, mod_s_seq(l, 0), norm_mix[l][None], pool_w_b[l], pool_scale[l][None], pos_s, 1)
        pool_s.append(npool[:, 1:])
        xs = _ffn(xs.reshape(1, b_s, d), mod_s_tok(l, 1), norm_ffn[l][None], w_gu_b[l], w_dn_b[l], b_s, tf).reshape(b_s, 1, d)
    xt = xs.reshape(1, b_s, d)
    tabs_s = _rope_tables(jnp.full((b_s,), pos_s))
    rows_s, win_s = _kv_proj(xt, mod_kv[b_p:m_all][None], norm_kv[None], w_kv_b, kn, seg, tabs_s, place, one_row,
                             b_s, False)
    rows_s = rows_s.reshape(b_s, 4, N_KV_HEADS, HEAD_DIM)
    win_s = win_s.reshape(b_s, 2, N_KV_HEADS, HEAD_DIM)

    cache_t = cache_kv.transpose(0, 2, 3, 4, 1)
    state_t = state_kv_win.transpose(0, 2, 3, 4, 1)
    raw_s = _compress(page_table, cache_t, pe_t, w1cat, w2bd, True)
    nb_s = -(-(past_len + 1) // SEL_BLOCK)
    z_tail = jnp.pad(rows_s[:, :2].transpose(1, 0, 2, 3).reshape(2 * b_s * N_KV_HEADS, HEAD_DIM),
                     ((0, 0), (0, (SEL_BLOCK - 1) * HEAD_DIM)))
    raw_tail = _compress_tail(z_tail, cmp_pe.reshape(2, 1, SEL_BLOCK * HEAD_DIM), cmp_w1.astype(BF16),
                              cmp_w2.astype(BF16)).reshape(2, b_s, 1, hq)
    nbp_s = -(-nb_s // LANES) * LANES
    raw_all = jnp.pad(jnp.concatenate([raw_s, raw_tail], axis=2), ((0, 0), (0, 0), (0, nbp_s - nb_s), (0, 0)))
    tabs_blk_s = _rope_tables((jnp.arange(nbp_s) + 1) * SEL_BLOCK - 1)
    kct_s, vcp_s = _cmp_finish(raw_all[0], raw_all[1], kn0, seg, tabs_blk_s, place)

    new_h = jnp.repeat(jnp.concatenate([rows_s[:, 2:4], win_s], axis=1), gq, axis=2)
    bb = math.gcd(b_s, 8)
    for l in range(n_a, depth):
        j = l - n_a
        q_s, gates_s = _q_proj(xt, mod_s_tok(l, 0), norm_mix[l][None], w_q_b[j], w_g_b[j],
                               _tile_heads(q_norm[j], N_KV_HEADS), seg, tabs_s, b_s, sm, F32)
        q3 = q_s.reshape(b_s, n_heads, HEAD_DIM)
        q_cmp = jnp.pad(q3, ((0, 0), (0, 0), (0, LANES - HEAD_DIM)))
        g3 = gates_s[0, :, :, :3 * gq].transpose(1, 0, 2).reshape(b_s, n_heads, 3)
        g3 = jnp.pad(g3, ((0, 0), (0, 0), (0, LANES - 3)))
        o_c, idx = _decode_a(q_cmp, kct_s, vcp_s, pos_s, nb_s, gq, bb)
        o_s = _decode_b(idx.reshape(b_s, N_KV_HEADS * N_SEL), page_table, q3, g3, o_c, new_h, cache_t, state_t,
                        pos_s, gq)
        xt = _ffn(xt, mod_s_tok(l, 1), norm_ffn[l][None], w_gu_b[l], w_dn_b[l], b_s, tf,
                  (o_s.reshape(1, b_s, d), mod_s_tok(l, 0), w_o_b[j]))
    y_sample = xt.reshape(b_s, 1, d)
    kv_rows_sample = rows_s.reshape(b_s, 1, 4, N_KV_HEADS, HEAD_DIM)
    win_sample = jnp.concatenate([state_t[..., 1:], win_s[..., None]], axis=-1).transpose(0, 4, 1, 2, 3)
    pool_sample = jnp.stack(pool_s)
    return (y_prompt, y_sample, kv_rows_prompt, kv_rows_sample, win_prompt, win_sample, pool_prompt, pool_sample)
```

```python
import functools
import math

import jax
import jax.numpy as jnp
from jax import lax
from jax.experimental import pallas as pl
from jax.experimental.pallas import tpu as pltpu

F32 = jnp.float32
BF16 = jnp.bfloat16

POOL_WINDOWS = (2, 4, 8, 16)
POOL_HALO = 16
HEAD_DIM = 64
N_KV_HEADS = 4
ROT_DIM = 16
ROPE_THETA = 500000.0
SEL_BLOCK = 64
N_SEL = 16
N_FORCED = 3
WINDOW = 512
PAGE_SIZE = 128
FORCE_SCORE = 1.0e4
EPS = 1e-6
NEG = -float(2 ** 100)
LOG2E = 1.4426950408889634
LANES = 128
TILE_ROW_STRIDE = 72
VMEM_LIMIT = 56 * 1024 * 1024
NT_DIMS = (((1,), (1,)), ((), ()))


def _cparams(sem):
    return pltpu.CompilerParams(dimension_semantics=sem, vmem_limit_bytes=VMEM_LIMIT)


def _dot(a, b):
    return jnp.dot(a, b, preferred_element_type=F32)


def _dot_nt(a, b):
    return lax.dot_general(a, b, NT_DIMS, preferred_element_type=F32)


def _split(a):
    hi = a.astype(BF16)
    lo = (a - hi.astype(F32)).astype(BF16)
    return hi, lo


def _adaln(x, g, shift, scale):
    ms = jnp.mean(x * x, axis=-1, keepdims=True)
    return x * lax.rsqrt(ms + EPS) * g * (1.0 + scale) + shift


def _head_rms(x, gain, seg):
    hi, lo = _split(x * x)
    ss = _dot(hi, seg) + _dot(lo, seg)
    return x * lax.rsqrt(ss * (1.0 / HEAD_DIM) + EPS) * gain


def _rope(x, cos, sin_up, sin_dn):
    outs = []
    for a in range(x.shape[1] // LANES):
        xa = x[:, a * LANES:(a + 1) * LANES]
        up = pltpu.roll(xa, LANES - ROT_DIM // 2, 1)
        dn = pltpu.roll(xa, ROT_DIM // 2, 1)
        outs.append(xa * cos + up * sin_up + dn * sin_dn)
    return outs[0] if len(outs) == 1 else jnp.concatenate(outs, axis=1)


def _gelu_tanh(x):
    return x * (0.5 * (1.0 + jnp.tanh(math.sqrt(2.0 / math.pi) * (x + 0.044715 * (x * x * x)))))


def _mods_kernel(c_ref, w_ref, b_ref, o_ref):
    ch, cl = _split(c_ref[...])
    wh, wl = _split(w_ref[0])
    o_ref[0] = _dot(ch, wh) + _dot(ch, wl) + _dot(cl, wh) + b_ref[0]


def _mods(c_all, w, b, tn=1024):
    n_l, d, n = w.shape
    m = c_all.shape[0]
    return pl.pallas_call(
        _mods_kernel,
        out_shape=jax.ShapeDtypeStruct((n_l, m, n), F32),
        grid=(n_l, n // tn),
        in_specs=[pl.BlockSpec((m, d), lambda l, j: (0, 0)),
                  pl.BlockSpec((1, d, tn), lambda l, j: (l, 0, j)),
                  pl.BlockSpec((1, 1, tn), lambda l, j: (l, 0, j))],
        out_specs=pl.BlockSpec((1, m, tn), lambda l, j: (l, 0, j)),
        compiler_params=_cparams(("parallel", "parallel")),
        name="mods",
    )(c_all, w, b)


def _pool_kernel(x_ref, pre_ref, mod_ref, g_ref, pw_ref, ps_ref, o_ref, np_ref, hb_ref, *, tt, pos0, d):
    i = pl.program_id(1)
    x = x_ref[0]
    mod = mod_ref[0]
    h = _adaln(x, g_ref[...], mod[:, :d], mod[:, d:2 * d])

    @pl.when(i == 0)
    def _():
        hb_ref[0:POOL_HALO, :] = pre_ref[0]

    hb_ref[POOL_HALO:POOL_HALO + tt, :] = h
    pos = pos0 + i * tt + lax.broadcasted_iota(jnp.int32, (tt, 1), 0)
    pg = d // len(POOL_WINDOWS)
    ys = []
    for gi, w in enumerate(POOL_WINDOWS):
        c0 = gi * pg
        hg = h[:, c0:c0 + pg]
        s = hg
        for j in range(1, w):
            s = s + hb_ref[POOL_HALO - j:POOL_HALO - j + tt, c0:c0 + pg]
        cnt = jnp.minimum(w, pos + 1).astype(F32)
        pooled = s / cnt - hg
        ys.append(_dot(pooled.astype(BF16), pw_ref[gi]))
    y = jnp.concatenate(ys, axis=1) * ps_ref[...]
    o_ref[0] = x + mod[:, 2 * d:] * y
    last = hb_ref[tt:tt + POOL_HALO, :]
    np_ref[0] = last
    hb_ref[0:POOL_HALO, :] = last


def _pool_layer(x, prefix16, mod, g, pw, ps, pos0, tt):
    b, t, d = x.shape
    kern = functools.partial(_pool_kernel, tt=tt, pos0=pos0, d=d)
    pg = d // len(POOL_WINDOWS)
    return pl.pallas_call(
        kern,
        out_shape=(jax.ShapeDtypeStruct((b, t, d), F32), jax.ShapeDtypeStruct((b, POOL_HALO, d), F32)),
        grid=(b, t // tt),
        in_specs=[pl.BlockSpec((1, tt, d), lambda bi, i: (bi, i, 0)),
                  pl.BlockSpec((1, POOL_HALO, d), lambda bi, i: (bi, 0, 0)),
                  pl.BlockSpec((1, 1, 3 * d), lambda bi, i: (bi, 0, 0)),
                  pl.BlockSpec((1, d), lambda bi, i: (0, 0)),
                  pl.BlockSpec((len(POOL_WINDOWS), pg, pg), lambda bi, i: (0, 0, 0)),
                  pl.BlockSpec((1, d), lambda bi, i: (0, 0))],
        out_specs=(pl.BlockSpec((1, tt, d), lambda bi, i: (bi, i, 0)),
                   pl.BlockSpec((1, POOL_HALO, d), lambda bi, i: (bi, 0, 0))),
        scratch_shapes=[pltpu.VMEM((POOL_HALO + max(tt, 8), d), F32)],
        compiler_params=_cparams(("parallel", "arbitrary")),
        name="pool_layer",
    )(x, prefix16, mod, g, pw, ps)


def _ffn_kernel(x_ref, mod_ref, g_ref, wgu_ref, wd_ref, *rest, d, tf, with_attn):
    ff = wd_ref.shape[0]
    x = x_ref[0]
    if with_attn:
        a_ref, amod_ref, wo_ref, o_ref = rest
        x = x + amod_ref[0][:, 2 * d:] * _dot(a_ref[0].astype(BF16), wo_ref[...])
    else:
        (o_ref,) = rest
    mod = mod_ref[0]
    hb = _adaln(x, g_ref[...], mod[:, :d], mod[:, d:2 * d]).astype(BF16)
    acc = jnp.zeros(x.shape, F32)
    for f in range(ff // tf):
        g = _dot(hb, wgu_ref[:, f * tf:(f + 1) * tf])
        u = _dot(hb, wgu_ref[:, ff + f * tf:ff + (f + 1) * tf])
        a = (g * (1.0 / (1.0 + jnp.exp(-g)))) * u
        acc = acc + _dot(a.astype(BF16), wd_ref[f * tf:(f + 1) * tf, :])
    o_ref[0] = x + mod[:, 2 * d:] * acc


def _ffn(x, mod, g, w_gu, w_dn, tm, tf, attn=None):
    b, t, d = x.shape
    rm = mod.shape[1]
    ff = w_dn.shape[0]
    kern = functools.partial(_ffn_kernel, d=d, tf=tf, with_attn=attn is not None)
    once = pl.Buffered(1)
    operands = [x, mod, g, w_gu, w_dn]
    in_specs = [pl.BlockSpec((1, tm, d), lambda bi, i: (bi, i, 0)),
                pl.BlockSpec((1, rm, 3 * d), lambda bi, i: (bi, 0, 0)),
                pl.BlockSpec((1, d), lambda bi, i: (0, 0)),
                pl.BlockSpec((d, 2 * ff), lambda bi, i: (0, 0), pipeline_mode=once),
                pl.BlockSpec((ff, d), lambda bi, i: (0, 0), pipeline_mode=once)]
    if attn is not None:
        operands += list(attn)
        in_specs += [pl.BlockSpec((1, tm, d), lambda bi, i: (bi, i, 0)),
                     pl.BlockSpec((1, rm, 3 * d), lambda bi, i: (bi, 0, 0)),
                     pl.BlockSpec((d, d), lambda bi, i: (0, 0), pipeline_mode=once)]
    return pl.pallas_call(
        kern,
        out_shape=jax.ShapeDtypeStruct((b, t, d), F32),
        grid=(b, t // tm),
        in_specs=in_specs,
        out_specs=pl.BlockSpec((1, tm, d), lambda bi, i: (bi, i, 0)),
        compiler_params=_cparams(("parallel", "parallel")),
        name="ffn",
    )(*operands)


def _kv_kernel(x_ref, mod_ref, g_ref, w_ref, kn_ref, seg_ref, cos_ref, su_ref, sd_ref, place_ref, one_ref,
               *out_refs, d, transposed):
    hq = N_KV_HEADS * HEAD_DIM
    mod = mod_ref[0]
    h = _adaln(x_ref[0], g_ref[...], mod[:, :d], mod[:, d:2 * d])
    proj = _dot(h.astype(BF16), w_ref[...])
    seg = seg_ref[...]
    cos, su, sd = cos_ref[...], su_ref[...], sd_ref[...]
    k_sel = _rope(_head_rms(proj[:, 2 * hq:3 * hq], kn_ref[0], seg), cos, su, sd)
    k_win = _rope(_head_rms(proj[:, 4 * hq:5 * hq], kn_ref[1], seg), cos, su, sd)
    v_sel = proj[:, 3 * hq:4 * hq]
    v_win = proj[:, 5 * hq:6 * hq]
    if not transposed:
        rows_ref, win_ref = out_refs
        rows_ref[0] = jnp.concatenate([proj[:, :2 * hq], k_sel, v_sel], axis=1)
        win_ref[0] = jnp.concatenate([k_win, v_win], axis=1)
        return
    rows_t_ref, win_t_ref, kst_ref, vs_ref, kwt_ref, vw_ref = out_refs
    tm = k_sel.shape[0]
    zeros = jnp.zeros((HEAD_DIM, tm), BF16)
    k_sel_t = k_sel.T
    k_win_t = k_win.T
    for t_ref, slabs in ((rows_t_ref, (proj[:, :hq].T, proj[:, hq:2 * hq].T, k_sel_t, v_sel.T)),
                         (win_t_ref, (k_win_t, v_win.T))):
        for sl, slab in enumerate(slabs):
            for k in range(N_KV_HEADS):
                t_ref[0, sl, k] = slab[k * HEAD_DIM:(k + 1) * HEAD_DIM, :]
    for kt_ref, kt32 in ((kst_ref, k_sel_t), (kwt_ref, k_win_t)):
        kt = kt32.astype(BF16)
        for k in range(N_KV_HEADS):
            kt_ref[0, k, 0:HEAD_DIM, :] = kt[k * HEAD_DIM:(k + 1) * HEAD_DIM, :]
            kt_ref[0, k, HEAD_DIM:2 * HEAD_DIM, :] = zeros
    for va_ref, vv in ((vs_ref, v_sel), (vw_ref, v_win)):
        va = (_dot(vv.astype(BF16), place_ref[...]) + one_ref[...]).astype(BF16)
        for k in range(N_KV_HEADS):
            va_ref[0, k] = va[:, k * LANES:(k + 1) * LANES]


def _kv_proj(x, mod, g, w_kv, kn, seg, tabs, place, one_row, tm, transposed):
    b, t, d = x.shape
    rm = mod.shape[1]
    hq = N_KV_HEADS * HEAD_DIM
    kern = functools.partial(_kv_kernel, d=d, transposed=transposed)
    const2 = lambda bi, i: (0, 0)
    tab_spec = pl.BlockSpec((tm, LANES), lambda bi, i: (i, 0))
    if transposed:
        kt_shape = jax.ShapeDtypeStruct((b, N_KV_HEADS, 2 * HEAD_DIM, t), BF16)
        va_shape = jax.ShapeDtypeStruct((b, N_KV_HEADS, t, LANES), BF16)
        kt_spec = pl.BlockSpec((1, N_KV_HEADS, 2 * HEAD_DIM, tm), lambda bi, i: (bi, 0, 0, i))
        va_spec = pl.BlockSpec((1, N_KV_HEADS, tm, LANES), lambda bi, i: (bi, 0, i, 0))
        out_shape = [jax.ShapeDtypeStruct((b, 4, N_KV_HEADS, HEAD_DIM, t), F32),
                     jax.ShapeDtypeStruct((b, 2, N_KV_HEADS, HEAD_DIM, t), F32), kt_shape, va_shape, kt_shape, va_shape]
        out_specs = [pl.BlockSpec((1, 4, N_KV_HEADS, HEAD_DIM, tm), lambda bi, i: (bi, 0, 0, 0, i)),
                     pl.BlockSpec((1, 2, N_KV_HEADS, HEAD_DIM, tm), lambda bi, i: (bi, 0, 0, 0, i)),
                     kt_spec, va_spec, kt_spec, va_spec]
    else:
        out_shape = [jax.ShapeDtypeStruct((b, t, 4 * hq), F32), jax.ShapeDtypeStruct((b, t, 2 * hq), F32)]
        out_specs = [pl.BlockSpec((1, tm, 4 * hq), lambda bi, i: (bi, i, 0)),
                     pl.BlockSpec((1, tm, 2 * hq), lambda bi, i: (bi, i, 0))]
    return pl.pallas_call(
        kern,
        out_shape=tuple(out_shape),
        grid=(b, t // tm),
        in_specs=[pl.BlockSpec((1, tm, d), lambda bi, i: (bi, i, 0)),
                  pl.BlockSpec((1, rm, 2 * d), lambda bi, i: (bi, 0, 0)),
                  pl.BlockSpec((1, d), const2),
                  pl.BlockSpec((d, 6 * hq), const2),
                  pl.BlockSpec((2, 1, hq), lambda bi, i: (0, 0, 0)),
                  pl.BlockSpec((hq, hq), const2),
                  tab_spec, tab_spec, tab_spec,
                  pl.BlockSpec((hq, N_KV_HEADS * LANES), const2),
                  pl.BlockSpec((1, N_KV_HEADS * LANES), const2)],
        out_specs=tuple(out_specs),
        compiler_params=_cparams(("parallel", "parallel")),
        name="kv_proj",
    )(x, mod, g, w_kv, kn, seg, *tabs, place, one_row)


def _compress_kernel(pt_ref, src_ref, pe_ref, w1_ref, w2_ref, o_ref, buf_ref, sem_ref, *, n_pages, n_b, paged):
    n_tiles = n_pages * N_KV_HEADS
    sl = pl.program_id(0)
    b = pl.program_id(1)
    step = sl * n_b + b
    cur = step % 2

    def tile_copy(sl_, b_, slot_, p, h):
        if paged:
            src = src_ref.at[pt_ref[b_, p], sl_, h]
        else:
            src = src_ref.at[b_, sl_, h, :, pl.ds(p * PAGE_SIZE, PAGE_SIZE)]
        ti = p * N_KV_HEADS + h
        return pltpu.make_async_copy(src, buf_ref.at[slot_, pl.ds(ti * TILE_ROW_STRIDE, HEAD_DIM), :],
                                     sem_ref.at[slot_])

    def issue(sl_, b_, slot_):
        for p in range(n_pages):
            for h in range(N_KV_HEADS):
                tile_copy(sl_, b_, slot_, p, h).start()

    @pl.when(step == 0)
    def _():
        issue(sl, b, 0)

    @pl.when(step + 1 < 2 * n_b)
    def _():
        nxt = step + 1
        issue(nxt // n_b, nxt % n_b, 1 - cur)

    for p in range(n_pages):
        for h in range(N_KV_HEADS):
            tile_copy(sl, b, cur, p, h).wait()

    pe = pe_ref[0]
    hid = jnp.zeros((n_tiles, 2 * LANES), F32)
    for dp in range(HEAD_DIM // 2):
        xs = [buf_ref[cur, pl.ds(dd, n_tiles, stride=TILE_ROW_STRIDE), :] + pe[dd:dd + 1, :]
              for dd in (2 * dp, 2 * dp + 1)]
        hid = hid + _dot(jnp.concatenate(xs, axis=1).astype(BF16), w1_ref[0, dp])
    o_ref[0, 0] = _dot(_gelu_tanh(hid).astype(BF16), w2_ref[0])


def _compress(page_table, src, pe_t, w1cat, w2bd, paged):
    n_b, n_pages = page_table.shape
    n_tiles = n_pages * N_KV_HEADS
    kern = functools.partial(_compress_kernel, n_pages=n_pages, n_b=n_b, paged=paged)
    grid_spec = pltpu.PrefetchScalarGridSpec(
        num_scalar_prefetch=1,
        grid=(2, n_b),
        in_specs=[pl.BlockSpec(memory_space=pl.ANY),
                  pl.BlockSpec((1, HEAD_DIM, LANES), lambda s, b, pt: (s, 0, 0)),
                  pl.BlockSpec((1, HEAD_DIM // 2, 2 * LANES, 2 * LANES), lambda s, b, pt: (s, 0, 0, 0)),
                  pl.BlockSpec((1, 2 * LANES, LANES), lambda s, b, pt: (s, 0, 0))],
        out_specs=pl.BlockSpec((1, 1, n_tiles, LANES), lambda s, b, pt: (s, b, 0, 0)),
        scratch_shapes=[pltpu.VMEM((2, n_tiles * TILE_ROW_STRIDE, LANES), F32),
                        pltpu.SemaphoreType.DMA((2,))])
    raw = pl.pallas_call(
        kern,
        out_shape=jax.ShapeDtypeStruct((2, n_b, n_tiles, LANES), F32),
        grid_spec=grid_spec,
        compiler_params=_cparams(("arbitrary", "arbitrary")),
        name="compress",
    )(page_table, src, pe_t, w1cat, w2bd)
    raw = raw.reshape(2, n_b, n_pages, N_KV_HEADS, 2, HEAD_DIM).transpose(0, 1, 2, 4, 3, 5)
    return raw.reshape(2, n_b, 2 * n_pages, N_KV_HEADS * HEAD_DIM)


def _tail_kernel(z_ref, pe_ref, w1_ref, w2_ref, o_ref):
    z = z_ref[...] + pe_ref[0]
    hid = _gelu_tanh(_dot(z.astype(BF16), w1_ref[0]))
    o_ref[0] = _dot(hid.astype(BF16), w2_ref[0])


def _compress_tail(z, pe_flat, w1, w2):
    m, kdim = z.shape[0] // 2, z.shape[1]
    hid = w1.shape[2]
    return pl.pallas_call(
        _tail_kernel,
        out_shape=jax.ShapeDtypeStruct((2, m, HEAD_DIM), F32),
        grid=(2,),
        in_specs=[pl.BlockSpec((m, kdim), lambda s: (s, 0)),
                  pl.BlockSpec((1, 1, kdim), lambda s: (s, 0, 0)),
                  pl.BlockSpec((1, kdim, hid), lambda s: (s, 0, 0)),
                  pl.BlockSpec((1, hid, HEAD_DIM), lambda s: (s, 0, 0))],
        out_specs=pl.BlockSpec((1, m, HEAD_DIM), lambda s: (s, 0, 0)),
        compiler_params=_cparams(("parallel",)),
        name="compress_tail",
    )(z, pe_flat, w1, w2)


def _cmp_finish_kernel(kc_ref, vc_ref, kn_ref, seg_ref, cos_ref, su_ref, sd_ref, place_ref, kt_ref, vp_ref):
    kc = _rope(_head_rms(kc_ref[0], kn_ref[...], seg_ref[...]), cos_ref[...], su_ref[...], sd_ref[...])
    kt = kc.T.astype(BF16)
    nbp = kc.shape[0]
    zeros = jnp.zeros((HEAD_DIM, nbp), BF16)
    vp = _dot(vc_ref[0].astype(BF16), place_ref[...]).astype(BF16)
    for k in range(N_KV_HEADS):
        kt_ref[0, k, 0:HEAD_DIM, :] = kt[k * HEAD_DIM:(k + 1) * HEAD_DIM, :]
        kt_ref[0, k, HEAD_DIM:2 * HEAD_DIM, :] = zeros
        vp_ref[0, k] = vp[:, k * LANES:(k + 1) * LANES]


def _cmp_finish(kc_raw, vc_raw, kn0, seg, tabs, place):
    b, nbp, hq = kc_raw.shape
    const2 = lambda bi: (0, 0)
    tab_spec = pl.BlockSpec((nbp, LANES), const2)
    return pl.pallas_call(
        _cmp_finish_kernel,
        out_shape=(jax.ShapeDtypeStruct((b, N_KV_HEADS, 2 * HEAD_DIM, nbp), BF16),
                   jax.ShapeDtypeStruct((b, N_KV_HEADS, nbp, LANES), BF16)),
        grid=(b,),
        in_specs=[pl.BlockSpec((1, nbp, hq), lambda bi: (bi, 0, 0)),
                  pl.BlockSpec((1, nbp, hq), lambda bi: (bi, 0, 0)),
                  pl.BlockSpec((1, hq), const2),
                  pl.BlockSpec((hq, hq), const2),
                  tab_spec, tab_spec, tab_spec,
                  pl.BlockSpec((hq, N_KV_HEADS * LANES), const2)],
        out_specs=(pl.BlockSpec((1, N_KV_HEADS, 2 * HEAD_DIM, nbp), lambda bi: (bi, 0, 0, 0)),
                   pl.BlockSpec((1, N_KV_HEADS, nbp, LANES), lambda bi: (bi, 0, 0, 0))),
        compiler_params=_cparams(("parallel",)),
        name="cmp_finish",
    )(kc_raw, vc_raw, kn0, seg, *tabs, place)


def _q_kernel(x_ref, mod_ref, g_ref, wq_ref, wg_ref, qn_ref, seg_ref, cos_ref, su_ref, sd_ref,
              q_ref, gate_ref, *, d, q_scale):
    hq = N_KV_HEADS * HEAD_DIM
    mod = mod_ref[0]
    hb = _adaln(x_ref[0], g_ref[...], mod[:, :d], mod[:, d:2 * d]).astype(BF16)
    q = _dot(hb, wq_ref[...])
    seg = seg_ref[...]
    cos, su, sd = cos_ref[...], su_ref[...], sd_ref[...]
    for c in range(d // hq):
        qc = _rope(_head_rms(q[:, c * hq:(c + 1) * hq], qn_ref[...], seg), cos, su, sd) * q_scale
        q_ref[0, :, c * hq:(c + 1) * hq] = qc.astype(q_ref.dtype)
    gates = 1.0 / (1.0 + jnp.exp(-_dot(hb, wg_ref[...])))
    for k in range(N_KV_HEADS):
        gate_ref[0, k] = gates[:, k * LANES:(k + 1) * LANES]


def _q_proj(x, mod, g, w_q, w_g, qn, seg, tabs, tm, q_scale, q_dtype):
    b, t, d = x.shape
    rm = mod.shape[1]
    hq = N_KV_HEADS * HEAD_DIM
    ng = w_g.shape[1]
    kern = functools.partial(_q_kernel, d=d, q_scale=q_scale)
    const2 = lambda bi, i: (0, 0)
    tab_spec = pl.BlockSpec((tm, LANES), lambda bi, i: (i, 0))
    return pl.pallas_call(
        kern,
        out_shape=(jax.ShapeDtypeStruct((b, t, d), q_dtype),
                   jax.ShapeDtypeStruct((b, N_KV_HEADS, t, LANES), F32)),
        grid=(b, t // tm),
        in_specs=[pl.BlockSpec((1, tm, d), lambda bi, i: (bi, i, 0)),
                  pl.BlockSpec((1, rm, 3 * d), lambda bi, i: (bi, 0, 0)),
                  pl.BlockSpec((1, d), const2),
                  pl.BlockSpec((d, d), const2),
                  pl.BlockSpec((d, ng), const2),
                  pl.BlockSpec((1, hq), const2),
                  pl.BlockSpec((hq, hq), const2),
                  tab_spec, tab_spec, tab_spec],
        out_specs=(pl.BlockSpec((1, tm, d), lambda bi, i: (bi, i, 0)),
                   pl.BlockSpec((1, N_KV_HEADS, tm, LANES), lambda bi, i: (bi, 0, i, 0))),
        compiler_params=_cparams(("parallel", "parallel")),
        name="q_proj",
    )(x, mod, g, w_q, w_g, qn, seg, *tabs)


def _select_kernel(q_ref, kct_ref, vcp_ref, pin_ref, bias_ref, oc_ref, *, tq, n_sub, nb):
    gq = q_ref.shape[2] // HEAD_DIM
    rows = gq * tq
    i = pl.program_id(2)
    for u in range(n_sub):
        t0 = (i * n_sub + u) * tq
        qslab = q_ref[0, u * tq:(u + 1) * tq, :]
        qp = jnp.concatenate([_dot(qslab, pin_ref[g]).astype(BF16) for g in range(gq)], axis=0)
        pos_r = t0 + lax.broadcasted_iota(jnp.int32, (rows, 1), 0) % tq
        s_c = _dot(qp, kct_ref[0, 0])
        blk = lax.broadcasted_iota(jnp.int32, (rows, nb), 1)
        ok_c = (blk + 1) * SEL_BLOCK - 1 <= pos_r
        m_c = jnp.max(jnp.where(ok_c, s_c, -jnp.inf), axis=1, keepdims=True)
        m_c = jnp.where(m_c == -jnp.inf, 0.0, m_c)
        p_c = jnp.where(ok_c, jnp.exp2(s_c - m_c), 0.0)
        p_c = p_c / jnp.maximum(jnp.sum(p_c, axis=1, keepdims=True), 1e-30)
        o_c = _dot(p_c.astype(BF16), vcp_ref[0, 0]).astype(BF16)
        for g in range(gq):
            oc_ref[0, 0, g, u * tq:(u + 1) * tq, :] = o_c[g * tq:(g + 1) * tq]

        imp = p_c[0:tq]
        for g in range(1, gq):
            imp = imp + p_c[g * tq:(g + 1) * tq]
        imp_t = imp.T
        n_io = lax.broadcasted_iota(jnp.int32, (nb, tq), 0)
        cur = (t0 + lax.broadcasted_iota(jnp.int32, (nb, tq), 1)) // SEL_BLOCK
        forced = (n_io == 0) | (n_io == cur) | (n_io == cur - 1)
        v = jnp.where(forced, -2.0, jnp.where(n_io > cur, -1.0, imp_t))
        for _ in range(N_SEL - N_FORCED):
            mx = jnp.max(v, axis=0, keepdims=True)
            first = jnp.min(jnp.where(v == mx, n_io, nb), axis=0, keepdims=True)
            v = jnp.where(n_io == first, -2.0, v)
        bias_t = jnp.where((v == -2.0) & (n_io <= cur), 0.0, NEG)
        bias_ref[0, 0, u * tq:(u + 1) * tq, :] = bias_t.T.astype(BF16)


def _select(q, kct, vcp, pin, tq, n_sub):
    b, t, d = q.shape
    nb = kct.shape[3]
    gq = d // (N_KV_HEADS * HEAD_DIM)
    sw = gq * HEAD_DIM
    ts = tq * n_sub
    kern = functools.partial(_select_kernel, tq=tq, n_sub=n_sub, nb=nb)
    per_head4 = lambda bi, k, i: (bi, k, 0, 0)
    return pl.pallas_call(
        kern,
        out_shape=(jax.ShapeDtypeStruct((b, N_KV_HEADS, t, nb), BF16),
                   jax.ShapeDtypeStruct((b, N_KV_HEADS, gq, t, LANES), BF16)),
        grid=(b, N_KV_HEADS, t // ts),
        in_specs=[pl.BlockSpec((1, ts, sw), lambda bi, k, i: (bi, i, k)),
                  pl.BlockSpec((1, 1, 2 * HEAD_DIM, nb), per_head4),
                  pl.BlockSpec((1, 1, nb, LANES), per_head4),
                  pl.BlockSpec((gq, sw, LANES), lambda bi, k, i: (0, 0, 0))],
        out_specs=(pl.BlockSpec((1, 1, ts, nb), lambda bi, k, i: (bi, k, i, 0)),
                   pl.BlockSpec((1, 1, gq, ts, LANES), lambda bi, k, i: (bi, k, 0, i, 0))),
        compiler_params=_cparams(("parallel", "parallel", "parallel")),
        name="cmp_select",
    )(q, kct, vcp, pin)


def _attn_kernel(q_ref, gate_ref, bias_ref, oc_ref, kst_ref, vs_ref, kwt_ref, vw_ref, e_ref, pin_ref, pout_ref,
                 dmask_ref, wmask_ref, o_ref, qa_ref, m_ref, acc_ref, s_ref, *, tq, ck, ck_big, nb):
    gq = q_ref.shape[2] // HEAD_DIM
    rows = gq * tq
    i = pl.program_id(2)
    t0 = i * tq

    qslab = q_ref[0]
    qp = jnp.concatenate([_dot(qslab, pin_ref[g]).astype(BF16) for g in range(gq)], axis=0)
    qa_ref[...] = jnp.concatenate([jnp.concatenate([bias_ref[0, 0]] * gq, axis=0), qp], axis=1)

    m_ref[...] = jnp.full(m_ref.shape, NEG * 4.0, F32)

    def score_chunk(k0, width, causal):
        kaug = jnp.concatenate([e_ref[:, pl.ds(k0, width)], kst_ref[0, 0, :, pl.ds(k0, width)]], axis=0)
        s = _dot(qa_ref[...], kaug)
        if causal:
            s = s + jnp.concatenate([dmask_ref[0]] * gq, axis=0)
        s_ref[:, pl.ds(k0, width)] = s
        m = m_ref[...]
        for a in range(width // LANES):
            m = jnp.maximum(m, s[:, a * LANES:(a + 1) * LANES])
        m_ref[...] = m

    def value_chunk(k0, width):
        mb = m_ref[...]
        p = jnp.exp2(s_ref[:, pl.ds(k0, width)] - jnp.concatenate([mb] * (width // LANES), axis=1))
        acc_ref[...] += _dot(p.astype(BF16), vs_ref[0, 0, pl.ds(k0, width), :])

    ck_mid = 2 * ck
    c_last = t0 // ck
    n_big = c_last // (ck_big // ck)
    rem = c_last - n_big * (ck_big // ck)
    n_mid = rem // 2
    n_small = rem - 2 * n_mid
    mid0 = n_big * ck_big
    small0 = pl.multiple_of(mid0 + n_mid * ck_mid, ck)

    def loop(n, fn):
        def body(c, carry):
            fn(c)
            return carry
        lax.fori_loop(0, n, body, 0)

    loop(n_big, lambda c: score_chunk(pl.multiple_of(c * ck_big, ck_big), ck_big, False))
    loop(n_mid, lambda c: score_chunk(pl.multiple_of(mid0 + c * ck_mid, ck_mid), ck_mid, False))
    loop(n_small, lambda c: score_chunk(small0, ck, False))
    score_chunk(pl.multiple_of(c_last * ck, ck), ck, True)

    m_row = jnp.max(m_ref[...], axis=1, keepdims=True)
    m_ref[...] = jnp.broadcast_to(m_row, m_ref.shape)
    acc_ref[...] = jnp.zeros_like(acc_ref)
    loop(n_big, lambda c: value_chunk(pl.multiple_of(c * ck_big, ck_big), ck_big))
    loop(n_mid, lambda c: value_chunk(pl.multiple_of(mid0 + c * ck_mid, ck_mid), ck_mid))

    @pl.when(n_small == 1)
    def _():
        value_chunk(small0, ck_mid)

    @pl.when(n_small == 0)
    def _():
        value_chunk(small0, ck)

    acc = acc_ref[...]
    o_s = acc / acc[:, HEAD_DIM:HEAD_DIM + 1]

    wl = WINDOW + tq
    w0 = pl.multiple_of(jnp.maximum(t0 - WINDOW, 0), LANES)
    s_w = _dot(qp, kwt_ref[0, 0, :, pl.ds(w0, wl)]) + jnp.concatenate([wmask_ref[0]] * gq, axis=0)
    p_w = jnp.exp2(s_w - jnp.max(s_w, axis=1, keepdims=True))
    acc_w = _dot(p_w.astype(BF16), vw_ref[0, 0, pl.ds(w0, wl), :])
    o_w = acc_w / acc_w[:, HEAD_DIM:HEAD_DIM + 1]

    o_c = oc_ref[0, 0].reshape(rows, LANES).astype(F32)
    gates = gate_ref[0, 0]
    out = jnp.zeros((tq, gq * HEAD_DIM), F32)
    for g in range(gq):
        r = slice(g * tq, (g + 1) * tq)
        mix = (gates[:, 3 * g:3 * g + 1] * o_c[r] + gates[:, 3 * g + 1:3 * g + 2] * o_s[r]
               + gates[:, 3 * g + 2:3 * g + 3] * o_w[r])
        out = out + _dot(mix.astype(BF16), pout_ref[g])
    o_ref[0] = out.astype(o_ref.dtype)


def _attention(q, gates, bias, oc, kst, vs, kwt, vw, e_mat, pin, pout, tq, ck, ck_big):
    b, t, d = q.shape
    nb = bias.shape[3]
    gq = d // (N_KV_HEADS * HEAD_DIM)
    sw = gq * HEAD_DIM
    rows = gq * tq
    kern = functools.partial(_attn_kernel, tq=tq, ck=ck, ck_big=ck_big, nb=nb)
    per_head4 = lambda bi, k, i: (bi, k, 0, 0)
    once = pl.Buffered(1)
    tt = jnp.arange(tq)[None, :, None]
    n_phase = ck // tq
    jd = jnp.arange(ck)[None, None, :]
    dmask = jnp.where(jd <= jnp.arange(n_phase)[:, None, None] * tq + tt, 0.0, NEG).astype(F32)
    wl = WINDOW + tq
    n_early = WINDOW // tq
    jw = jnp.arange(wl)[None, None, :]
    pos_e = jnp.arange(n_early)[:, None, None] * tq + tt
    early = (jw <= pos_e) & (pos_e - jw < WINDOW)
    steady = (jw > tt) & (jw <= tt + WINDOW)
    wmask = jnp.where(jnp.concatenate([early, steady], axis=0), 0.0, NEG).astype(F32)
    return pl.pallas_call(
        kern,
        out_shape=jax.ShapeDtypeStruct((b, t, d), BF16),
        grid=(b, N_KV_HEADS, t // tq),
        in_specs=[pl.BlockSpec((1, tq, sw), lambda bi, k, i: (bi, i, k)),
                  pl.BlockSpec((1, 1, tq, LANES), lambda bi, k, i: (bi, k, i, 0)),
                  pl.BlockSpec((1, 1, tq, nb), lambda bi, k, i: (bi, k, i, 0)),
                  pl.BlockSpec((1, 1, gq, tq, LANES), lambda bi, k, i: (bi, k, 0, i, 0)),
                  pl.BlockSpec((1, 1, 2 * HEAD_DIM, t), per_head4, pipeline_mode=once),
                  pl.BlockSpec((1, 1, t, LANES), per_head4, pipeline_mode=once),
                  pl.BlockSpec((1, 1, 2 * HEAD_DIM, t), per_head4, pipeline_mode=once),
                  pl.BlockSpec((1, 1, t, LANES), per_head4, pipeline_mode=once),
                  pl.BlockSpec((nb, t), lambda bi, k, i: (0, 0), pipeline_mode=once),
                  pl.BlockSpec((gq, sw, LANES), lambda bi, k, i: (0, 0, 0)),
                  pl.BlockSpec((gq, LANES, sw), lambda bi, k, i: (0, 0, 0)),
                  pl.BlockSpec((1, tq, ck), lambda bi, k, i: (i % n_phase, 0, 0)),
                  pl.BlockSpec((1, tq, wl), lambda bi, k, i: (jnp.minimum(i, n_early), 0, 0))],
        out_specs=pl.BlockSpec((1, tq, sw), lambda bi, k, i: (bi, i, k)),
        scratch_shapes=[pltpu.VMEM((rows, nb + LANES), BF16),
                        pltpu.VMEM((rows, LANES), F32),
                        pltpu.VMEM((rows, LANES), F32),
                        pltpu.VMEM((rows, t), F32)],
        compiler_params=_cparams(("parallel", "parallel", "arbitrary")),
        name="nsa_attention",
    )(q, gates, bias, oc, kst, vs, kwt, vw, e_mat, pin, pout, dmask, wmask)


def _row_kvh(shape, gq):
    return lax.broadcasted_iota(jnp.int32, shape, 0) // gq


def _dec_a_kernel(q_ref, kct_ref, vcp_ref, oc_ref, idx_ref, *, pos, nb, gq, bb):
    nh = q_ref.shape[1]
    nbp = kct_ref.shape[3]
    rk = _row_kvh((nh, nbp), gq)
    rk_o = _row_kvh((nh, LANES), gq)
    blk = lax.broadcasted_iota(jnp.int32, (nh, nbp), 1)
    ok = ((blk + 1) * SEL_BLOCK - 1 <= pos) & (blk < nb)
    imps = []
    for lb in range(bb):
        qb = q_ref[lb].astype(BF16)
        s = jnp.zeros((nh, nbp), F32)
        for k in range(N_KV_HEADS):
            s = jnp.where(rk == k, _dot(qb, kct_ref[lb, k]), s)
        m = jnp.max(jnp.where(ok, s, -jnp.inf), axis=1, keepdims=True)
        m = jnp.where(m == -jnp.inf, 0.0, m)
        p = jnp.where(ok, jnp.exp(s - m), 0.0)
        p = p / jnp.maximum(jnp.sum(p, axis=1, keepdims=True), 1e-30)
        pb = p.astype(BF16)
        o_c = jnp.zeros((nh, LANES), F32)
        for k in range(N_KV_HEADS):
            o_c = jnp.where(rk_o == k, _dot(pb, vcp_ref[lb, k]), o_c)
        oc_ref[lb] = o_c
        imps += [jnp.sum(jnp.where(rk == k, p, 0.0), axis=0, keepdims=True) for k in range(N_KV_HEADS)]
    imp = jnp.concatenate(imps, axis=0)
    nr = bb * N_KV_HEADS
    n_io = lax.broadcasted_iota(jnp.int32, (nr, nbp), 1)
    cur = pos // SEL_BLOCK
    forced = (n_io == 0) | (n_io == cur) | (n_io == cur - 1)
    v = jnp.where(forced, FORCE_SCORE, jnp.where(n_io > cur, -1.0, imp))
    v = jnp.where(n_io < nb, v, -3.0)
    col = lax.broadcasted_iota(jnp.int32, (nr, N_SEL), 1)
    idx = jnp.zeros((nr, N_SEL), jnp.int32)
    for r in range(N_SEL):
        mx = jnp.max(v, axis=1, keepdims=True)
        first = jnp.min(jnp.where(v == mx, n_io, nbp), axis=1, keepdims=True)
        idx = jnp.where(col == r, first, idx)
        v = jnp.where(n_io == first, -4.0, v)
    idx_ref[...] = idx


def _decode_a(q_pad, kct, vcp, pos, nb, gq, bb):
    b, nh, _ = q_pad.shape
    nbp = kct.shape[3]
    kern = functools.partial(_dec_a_kernel, pos=pos, nb=nb, gq=gq, bb=bb)
    return pl.pallas_call(
        kern,
        out_shape=(jax.ShapeDtypeStruct((b, nh, LANES), F32),
                   jax.ShapeDtypeStruct((b * N_KV_HEADS, N_SEL), jnp.int32)),
        grid=(b // bb,),
        in_specs=[pl.BlockSpec((bb, nh, LANES), lambda bi: (bi, 0, 0)),
                  pl.BlockSpec((bb, N_KV_HEADS, 2 * HEAD_DIM, nbp), lambda bi: (bi, 0, 0, 0)),
                  pl.BlockSpec((bb, N_KV_HEADS, nbp, LANES), lambda bi: (bi, 0, 0, 0))],
        out_specs=(pl.BlockSpec((bb, nh, LANES), lambda bi: (bi, 0, 0)),
                   pl.BlockSpec((bb * N_KV_HEADS, N_SEL), lambda bi: (bi, 0))),
        compiler_params=_cparams(("parallel",)),
        name="decode_cmp_topk",
    )(q_pad, kct, vcp)


def _dec_b_kernel(idx_ref, pt_ref, q_ref, gate_ref, oc_ref, new_ref, cache_ref, swin_ref,
                  o_ref, kbuf_ref, vbuf_ref, sem_ref, *, pos, nb_past, gq, n_b):
    nh = q_ref.shape[1]
    b = pl.program_id(0)
    cur = b % 2
    nkeys = N_SEL * PAGE_SIZE

    def copies(b_, slot_, k, j):
        n = idx_ref[b_, k * N_SEL + j]
        page = pt_ref[b_, lax.shift_right_logical(jnp.minimum(n, nb_past - 1), 1)]
        ck = pltpu.make_async_copy(cache_ref.at[page, 2, k], kbuf_ref.at[slot_, k, j], sem_ref.at[slot_])
        cv = pltpu.make_async_copy(cache_ref.at[page, 3, k], vbuf_ref.at[slot_, k, j], sem_ref.at[slot_])
        return n, ck, cv

    def issue(b_, slot_):
        for k in range(N_KV_HEADS):
            for j in range(N_SEL):
                n, ck, cv = copies(b_, slot_, k, j)

                @pl.when(n < nb_past)
                def _():
                    ck.start()
                    cv.start()

    @pl.when(b == 0)
    def _():
        issue(b, 0)

    @pl.when(b + 1 < n_b)
    def _():
        issue(b + 1, 1 - cur)

    q = q_ref[0]
    qb = q.astype(BF16)
    new = new_ref[0]
    eye = (lax.broadcasted_iota(jnp.int32, (HEAD_DIM, HEAD_DIM), 0)
           == lax.broadcasted_iota(jnp.int32, (HEAD_DIM, HEAD_DIM), 1))
    col0 = (lax.broadcasted_iota(jnp.int32, (HEAD_DIM, PAGE_SIZE), 1) == 0).astype(BF16)
    lane = lax.broadcasted_iota(jnp.int32, (1, nkeys), 1)
    r_in = lane % PAGE_SIZE
    rk = _row_kvh((nh, nkeys), gq)
    rk_h = _row_kvh((nh, HEAD_DIM), gq)

    kpos = jnp.zeros((nh, nkeys), jnp.int32)
    half = jnp.zeros((nh, nkeys), jnp.int32)
    for k in range(N_KV_HEADS):
        tail_k = _dot(jnp.where(eye, new[0, k * gq:k * gq + 1, :], 0.0).astype(BF16), col0)
        tail_v = _dot(jnp.where(eye, new[1, k * gq:k * gq + 1, :], 0.0).astype(BF16), col0)
        kp = r_in
        hv = jnp.zeros((1, nkeys), jnp.int32)
        for j in range(N_SEL):
            n, ck, cv = copies(b, cur, k, j)

            @pl.when(n < nb_past)
            def _():
                ck.wait()
                cv.wait()

            @pl.when(n >= nb_past)
            def _():
                kbuf_ref[cur, k, j] = tail_k
                vbuf_ref[cur, k, j] = tail_v

            in_j = lane // PAGE_SIZE == j
            kp = kp + jnp.where(in_j, lax.shift_right_logical(n, 1) * PAGE_SIZE, 0)
            hv = hv + jnp.where(in_j, n & 1, 0)
        kpos = jnp.where(rk == k, kp, kpos)
        half = jnp.where(rk == k, hv, half)
    s = jnp.zeros((nh, nkeys), F32)
    for k in range(N_KV_HEADS):
        sk = jnp.concatenate([_dot(qb, kbuf_ref[cur, k, j].astype(BF16)) for j in range(N_SEL)], axis=1)
        s = jnp.where(rk == k, sk, s)
    ok = (r_in // SEL_BLOCK == half) & (kpos <= pos)
    m = jnp.max(jnp.where(ok, s, -jnp.inf), axis=1, keepdims=True)
    p = jnp.where(ok, jnp.exp(s - m), 0.0)
    p = p / jnp.maximum(jnp.sum(p, axis=1, keepdims=True), 1e-30)
    pb = p.astype(BF16)
    o_s = jnp.zeros((nh, HEAD_DIM), F32)
    for k in range(N_KV_HEADS):
        o_k = jnp.zeros((nh, HEAD_DIM), F32)
        for j in range(N_SEL):
            o_k = o_k + _dot_nt(pb[:, j * PAGE_SIZE:(j + 1) * PAGE_SIZE], vbuf_ref[cur, k, j].astype(BF16))
        o_s = jnp.where(rk_h == k, o_k, o_s)

    w_buf = swin_ref.shape[4]
    rk_w = _row_kvh((nh, w_buf), gq)
    s_w = jnp.zeros((nh, w_buf), F32)
    for k in range(N_KV_HEADS):
        s_w = jnp.where(rk_w == k, _dot(qb, swin_ref[0, 0, k].astype(BF16)), s_w)
    s_n = jnp.sum(q * new[2], axis=1, keepdims=True)
    ridx = lax.broadcasted_iota(jnp.int32, (nh, w_buf), 1)
    ok_w = w_buf - ridx < WINDOW
    m_w = jnp.maximum(jnp.max(jnp.where(ok_w, s_w, -jnp.inf), axis=1, keepdims=True), s_n)
    p_w = jnp.where(ok_w, jnp.exp(s_w - m_w), 0.0)
    p_n = jnp.exp(s_n - m_w)
    l_w = jnp.sum(p_w, axis=1, keepdims=True) + p_n
    pwb = (p_w / l_w).astype(BF16)
    o_w = jnp.zeros((nh, HEAD_DIM), F32)
    for k in range(N_KV_HEADS):
        o_w = jnp.where(rk_h == k, _dot_nt(pwb, swin_ref[0, 1, k].astype(BF16)), o_w)
    o_w = o_w + (p_n / l_w) * new[3]

    gates = gate_ref[0]
    o_ref[0] = gates[:, 0:1] * oc_ref[0][:, :HEAD_DIM] + gates[:, 1:2] * o_s + gates[:, 2:3] * o_w


def _decode_b(idx_flat, page_table, q3, gates, o_c, new_h, cache_t, state_t, pos, gq):
    b, nh, _ = q3.shape
    w_buf = state_t.shape[4]
    nb_past = page_table.shape[1] * (PAGE_SIZE // SEL_BLOCK)
    kern = functools.partial(_dec_b_kernel, pos=pos, nb_past=nb_past, gq=gq, n_b=b)
    grid_spec = pltpu.PrefetchScalarGridSpec(
        num_scalar_prefetch=2,
        grid=(b,),
        in_specs=[pl.BlockSpec((1, nh, HEAD_DIM), lambda bi, ix, pt: (bi, 0, 0)),
                  pl.BlockSpec((1, nh, LANES), lambda bi, ix, pt: (bi, 0, 0)),
                  pl.BlockSpec((1, nh, LANES), lambda bi, ix, pt: (bi, 0, 0)),
                  pl.BlockSpec((1, 4, nh, HEAD_DIM), lambda bi, ix, pt: (bi, 0, 0, 0)),
                  pl.BlockSpec(memory_space=pl.ANY),
                  pl.BlockSpec((1, 2, N_KV_HEADS, HEAD_DIM, w_buf), lambda bi, ix, pt: (bi, 0, 0, 0, 0))],
        out_specs=pl.BlockSpec((1, nh, HEAD_DIM), lambda bi, ix, pt: (bi, 0, 0)),
        scratch_shapes=[pltpu.VMEM((2, N_KV_HEADS, N_SEL, HEAD_DIM, PAGE_SIZE), F32),
                        pltpu.VMEM((2, N_KV_HEADS, N_SEL, HEAD_DIM, PAGE_SIZE), F32),
                        pltpu.SemaphoreType.DMA((2,))])
    return pl.pallas_call(
        kern,
        out_shape=jax.ShapeDtypeStruct((b, nh, HEAD_DIM), F32),
        grid_spec=grid_spec,
        compiler_params=_cparams(("arbitrary",)),
        name="decode_sel_win",
    )(idx_flat, page_table, q3, gates, o_c, new_h, cache_t, state_t)


def _rope_tables(pos):
    half = ROT_DIM // 2
    inv = jnp.power(jnp.float32(ROPE_THETA), -jnp.arange(half, dtype=F32) * 2.0 / ROT_DIM)
    ang = pos.astype(F32)[:, None] * inv[None, :]
    cos, sin = jnp.cos(ang), jnp.sin(ang)
    r = jnp.arange(LANES) % HEAD_DIM
    f = r % half
    cos_t = jnp.where(r < ROT_DIM, cos[:, f], 1.0)
    sin_up = jnp.where(r < half, -sin[:, f], 0.0)
    sin_dn = jnp.where((r >= half) & (r < ROT_DIM), sin[:, f], 0.0)
    return cos_t, sin_up, sin_dn


def _tile_heads(v, n):
    return jnp.tile(v.astype(F32), n)[None, :]


def kernel(x_prompt, x_sample, cache_kv, state_kv_win, state_pool, page_table, c_prompt, c_sample, ada_w, ada_b, norm_mix, norm_ffn, pool_w, pool_scale, ada_kv_w, ada_kv_b, norm_kv, w_kv, k_norm, cmp_pe, cmp_w1, cmp_w2, w_qg, q_norm, w_o, w_gate_up, w_down):
    b_p, seq, d = x_prompt.shape
    b_s, dec_seq, _ = x_sample.shape
    depth = ada_w.shape[0]
    n_a = pool_w.shape[0]
    n_heads = d // HEAD_DIM
    gq = n_heads // N_KV_HEADS
    hq = N_KV_HEADS * HEAD_DIM
    n_pages = page_table.shape[1]
    past_len = n_pages * PAGE_SIZE
    w_buf = state_kv_win.shape[1]
    d_ff = w_down.shape[1]
    assert dec_seq == 1 and hq == 2 * LANES and seq % PAGE_SIZE == 0
    sm = HEAD_DIM ** -0.5

    w_gu_b = w_gate_up.astype(BF16)
    w_dn_b = w_down.astype(BF16)
    w_kv_b = w_kv.astype(BF16)
    pool_w_b = pool_w.astype(BF16)
    w_q_b = w_qg[:, :, :d].astype(BF16)
    wg_cols = w_qg[:, :, d:].reshape(-1, d, N_KV_HEADS, 3 * gq)
    w_g_b = jnp.pad(wg_cols, ((0, 0), (0, 0), (0, 0), (0, LANES - 3 * gq))).reshape(-1, d, N_KV_HEADS * LANES).astype(BF16)
    w_o_b = w_o.astype(BF16)

    head_of = jnp.arange(hq) // HEAD_DIM
    seg = (head_of[:, None] == head_of[None, :]).astype(BF16)
    lane_in = jnp.arange(hq)
    lane_out = jnp.arange(N_KV_HEADS * LANES)
    place = ((lane_out[None, :] // LANES == lane_in[:, None] // HEAD_DIM)
             & (lane_out[None, :] % LANES == lane_in[:, None] % HEAD_DIM)).astype(BF16)
    one_row = (lane_out % LANES == HEAD_DIM).astype(F32)[None, :]
    sw = gq * HEAD_DIM
    cin = jnp.arange(sw)
    pin = jnp.stack([((cin[:, None] // HEAD_DIM == g) & (jnp.arange(LANES)[None, :] == cin[:, None] % HEAD_DIM))
                     for g in range(gq)]).astype(BF16)
    pout = jnp.stack([((jnp.arange(LANES)[:, None] < HEAD_DIM)
                       & (cin[None, :] == g * HEAD_DIM + jnp.arange(LANES)[:, None]))
                      for g in range(gq)]).astype(BF16)

    kn = jnp.stack([_tile_heads(k_norm[1], N_KV_HEADS), _tile_heads(k_norm[2], N_KV_HEADS)])
    kn0 = _tile_heads(k_norm[0], N_KV_HEADS)
    pe_t = jnp.tile(cmp_pe.transpose(0, 2, 1), (1, 1, 2))
    eye2 = jnp.eye(2, dtype=F32)
    w1_dsj = cmp_w1.reshape(2, SEL_BLOCK, HEAD_DIM, -1).transpose(0, 2, 1, 3)
    n_hid = w1_dsj.shape[-1]
    w1cat = jnp.einsum('ab,zdsj->zdasbj', eye2, w1_dsj).reshape(2, HEAD_DIM // 2, 2 * 2 * SEL_BLOCK, 2 * n_hid).astype(BF16)
    w2bd = jnp.einsum('ab,zje->zajbe', eye2, cmp_w2).reshape(2, 2 * n_hid, 2 * HEAD_DIM).astype(BF16)

    c_all = jnp.concatenate([c_prompt, c_sample], axis=0)
    m_all = c_all.shape[0]
    m_pad = -(-m_all // 8) * 8
    c_all = jnp.pad(c_all, ((0, m_pad - m_all), (0, 0)))
    mods = _mods(c_all, ada_w.reshape(depth * 2, d, 3 * d), ada_b.reshape(depth * 2, 1, 3 * d)).reshape(depth, 2, m_pad, 3 * d)
    mod_kv = _mods(c_all, ada_kv_w[None], ada_kv_b[None, None, :])[0]

    def mod_p(l, j):
        return mods[l, j, :b_p][:, None, :]

    def mod_s_tok(l, j):
        return mods[l, j, b_p:m_all][None]

    def mod_s_seq(l, j):
        return mods[l, j, b_p:m_all][:, None, :]

    tm = min(512, seq)
    tf = d_ff // 2 if (d_ff // 2) % LANES == 0 else d_ff
    tq = 256
    tq_sel = min(1024, seq)
    ck = 512
    ck_big = min(2048, seq // 2)

    x = x_prompt
    pool_p = []
    for l in range(n_a):
        x, npool = _pool_layer(x, jnp.zeros((b_p, POOL_HALO, d), F32), mod_p(l, 0), norm_mix[l][None], pool_w_b[l],
                               pool_scale[l][None], 0, tm)
        pool_p.append(npool[:, 1:])
        x = _ffn(x, mod_p(l, 1), norm_ffn[l][None], w_gu_b[l], w_dn_b[l], tm, tf)

    tabs_p = _rope_tables(jnp.arange(seq))
    rows_t, win_t, kst, vs, kwt, vw = _kv_proj(x, mod_kv[:b_p][:, None, :], norm_kv[None], w_kv_b, kn, seg, tabs_p,
                                               place, one_row, tm, True)
    nb_p = seq // SEL_BLOCK
    pt_p = jnp.zeros((b_p, seq // PAGE_SIZE), jnp.int32)
    raw_p = _compress(pt_p, rows_t, pe_t, w1cat, w2bd, False)
    tabs_blk_p = _rope_tables((jnp.arange(nb_p) + 1) * SEL_BLOCK - 1)
    kct_p, vcp_p = _cmp_finish(raw_p[0], raw_p[1], kn0, seg, tabs_blk_p, place)
    e_mat = (jnp.arange(seq)[None, :] // SEL_BLOCK == jnp.arange(nb_p)[:, None]).astype(BF16)
    for l in range(n_a, depth):
        j = l - n_a
        q, gates = _q_proj(x, mod_p(l, 0), norm_mix[l][None], w_q_b[j], w_g_b[j], _tile_heads(q_norm[j], N_KV_HEADS),
                           seg, tabs_p, tm, sm * LOG2E, BF16)
        bias, o_cmp = _select(q, kct_p, vcp_p, pin, tq_sel, 1)
        o = _attention(q, gates, bias, o_cmp, kst, vs, kwt, vw, e_mat, pin, pout, tq, ck, ck_big)
        x = _ffn(x, mod_p(l, 1), norm_ffn[l][None], w_gu_b[l], w_dn_b[l], tm, tf, (o, mod_p(l, 0), w_o_b[j]))
    y_prompt = x
    kv_rows_prompt = rows_t.transpose(0, 4, 1, 2, 3)
    win_keep_p = min(WINDOW, seq)
    win_prompt = win_t[..., seq - win_keep_p:].transpose(0, 4, 1, 2, 3)
    pool_prompt = jnp.stack(pool_p)

    pos_s = past_len
    xs = x_sample
    pool_s = []
    for l in range(n_a):
        pre = jnp.pad(state_pool[l], ((0, 0), (POOL_HALO - state_pool.shape[2], 0), (0, 0)))
        xs, npool = _pool_layer(xs, pre, mod_s_seq(l, 0), norm_mix[l][None], pool_w_b[l], pool_scale[l][None], pos_s, 1)
        pool_s.append(npool[:, 1:])
        xs = _ffn(xs.reshape(1, b_s, d), mod_s_tok(l, 1), norm_ffn[l][None], w_gu_b[l], w_dn_b[l], b_s, tf).reshape(b_s, 1, d)
    xt = xs.reshape(1, b_s, d)
    tabs_s = _rope_tables(jnp.full((b_s,), pos_s))
    rows_s, win_s = _kv_proj(xt, mod_kv[b_p:m_all][None], norm_kv[None], w_kv_b, kn, seg, tabs_s, place, one_row,
                             b_s, False)
    rows_s = rows_s.reshape(b_s, 4, N_KV_HEADS, HEAD_DIM)
    win_s = win_s.reshape(b_s, 2, N_KV_HEADS, HEAD_DIM)

    cache_t = cache_kv.transpose(0, 2, 3, 4, 1)
    state_t = state_kv_win.transpose(0, 2, 3, 4, 1)
    raw_s = _compress(page_table, cache_t, pe_t, w1cat, w2bd, True)
    nb_s = -(-(past_len + 1) // SEL_BLOCK)
    z_tail = jnp.pad(rows_s[:, :2].transpose(1, 0, 2, 3).reshape(2 * b_s * N_KV_HEADS, HEAD_DIM),
                     ((0, 0), (0, (SEL_BLOCK - 1) * HEAD_DIM)))
    raw_tail = _compress_tail(z_tail, cmp_pe.reshape(2, 1, SEL_BLOCK * HEAD_DIM), cmp_w1.astype(BF16),
                              cmp_w2.astype(BF16)).reshape(2, b_s, 1, hq)
    nbp_s = -(-nb_s // LANES) * LANES
    raw_all = jnp.pad(jnp.concatenate([raw_s, raw_tail], axis=2), ((0, 0), (0, 0), (0, nbp_s - nb_s), (0, 0)))
    tabs_blk_s = _rope_tables((jnp.arange(nbp_s) + 1) * SEL_BLOCK - 1)
    kct_s, vcp_s = _cmp_finish(raw_all[0], raw_all[1], kn0, seg, tabs_blk_s, place)

    new_h = jnp.repeat(jnp.concatenate([rows_s[:, 2:4], win_s], axis=1), gq, axis=2)
    bb = math.gcd(b_s, 8)
    for l in range(n_a, depth):
        j = l - n_a
        q_s, gates_s = _q_proj(xt, mod_s_tok(l, 0), norm_mix[l][None], w_q_b[j], w_g_b[j],
                               _tile_heads(q_norm[j], N_KV_HEADS), seg, tabs_s, b_s, sm, F32)
        q3 = q_s.reshape(b_s, n_heads, HEAD_DIM)
        q_cmp = jnp.pad(q3, ((0, 0), (0, 0), (0, LANES - HEAD_DIM)))
        g3 = gates_s[0, :, :, :3 * gq].transpose(1, 0, 2).reshape(b_s, n_heads, 3)
        g3 = jnp.pad(g3, ((0, 0), (0, 0), (0, LANES - 3)))
        o_c, idx = _decode_a(q_cmp, kct_s, vcp_s, pos_s, nb_s, gq, bb)
        o_s = _decode_b(idx.reshape(b_s, N_KV_HEADS * N_SEL), page_table, q3, g3, o_c, new_h, cache_t, state_t,
                        pos_s, gq)
        xt = _ffn(xt, mod_s_tok(l, 1), norm_ffn[l][None], w_gu_b[l], w_dn_b[l], b_s, tf,
                  (o_s.reshape(1, b_s, d), mod_s_tok(l, 0), w_o_b[j]))
    y_sample = xt.reshape(b_s, 1, d)
    kv_rows_sample = rows_s.reshape(b_s, 1, 4, N_KV_HEADS, HEAD_DIM)
    win_sample = jnp.concatenate([state_t[..., 1:], win_s[..., None]], axis=-1).transpose(0, 4, 1, 2, 3)
    pool_sample = jnp.stack(pool_s)
    return (y_prompt, y_sample, kv_rows_prompt, kv_rows_sample, win_prompt, win_sample, pool_prompt, pool_sample)
```

```python
import functools
import math

import jax
import jax.numpy as jnp
from jax import lax
from jax.experimental import pallas as pl
from jax.experimental.pallas import tpu as pltpu

F32 = jnp.float32
BF16 = jnp.bfloat16

POOL_WINDOWS = (2, 4, 8, 16)
POOL_HALO = 16
HEAD_DIM = 64
N_KV_HEADS = 4
ROT_DIM = 16
ROPE_THETA = 500000.0
SEL_BLOCK = 64
N_SEL = 16
N_FORCED = 3
WINDOW = 512
PAGE_SIZE = 128
FORCE_SCORE = 1.0e4
EPS = 1e-6
NEG = -float(2 ** 100)
LOG2E = 1.4426950408889634
LANES = 128
TILE_ROW_STRIDE = 72
VMEM_LIMIT = 56 * 1024 * 1024
NT_DIMS = (((1,), (1,)), ((), ()))


def _cparams(sem):
    return pltpu.CompilerParams(dimension_semantics=sem, vmem_limit_bytes=VMEM_LIMIT)


def _dot(a, b):
    return jnp.dot(a, b, preferred_element_type=F32)


def _dot_nt(a, b):
    return lax.dot_general(a, b, NT_DIMS, preferred_element_type=F32)


def _split(a):
    hi = a.astype(BF16)
    lo = (a - hi.astype(F32)).astype(BF16)
    return hi, lo


def _adaln(x, g, shift, scale):
    ms = jnp.mean(x * x, axis=-1, keepdims=True)
    return x * lax.rsqrt(ms + EPS) * g * (1.0 + scale) + shift


def _head_rms(x, gain, seg):
    hi, lo = _split(x * x)
    ss = _dot(hi, seg) + _dot(lo, seg)
    return x * lax.rsqrt(ss * (1.0 / HEAD_DIM) + EPS) * gain


def _rope(x, cos, sin_up, sin_dn):
    outs = []
    for a in range(x.shape[1] // LANES):
        xa = x[:, a * LANES:(a + 1) * LANES]
        up = pltpu.roll(xa, LANES - ROT_DIM // 2, 1)
        dn = pltpu.roll(xa, ROT_DIM // 2, 1)
        outs.append(xa * cos + up * sin_up + dn * sin_dn)
    return outs[0] if len(outs) == 1 else jnp.concatenate(outs, axis=1)


def _gelu_tanh(x):
    return x * (0.5 * (1.0 + jnp.tanh(math.sqrt(2.0 / math.pi) * (x + 0.044715 * (x * x * x)))))


def _mods_kernel(c_ref, w_ref, b_ref, o_ref):
    ch, cl = _split(c_ref[...])
    wh, wl = _split(w_ref[0])
    o_ref[0] = _dot(ch, wh) + _dot(ch, wl) + _dot(cl, wh) + b_ref[0]


def _mods(c_all, w, b, tn=1024):
    n_l, d, n = w.shape
    m = c_all.shape[0]
    return pl.pallas_call(
        _mods_kernel,
        out_shape=jax.ShapeDtypeStruct((n_l, m, n), F32),
        grid=(n_l, n // tn),
        in_specs=[pl.BlockSpec((m, d), lambda l, j: (0, 0)),
                  pl.BlockSpec((1, d, tn), lambda l, j: (l, 0, j)),
                  pl.BlockSpec((1, 1, tn), lambda l, j: (l, 0, j))],
        out_specs=pl.BlockSpec((1, m, tn), lambda l, j: (l, 0, j)),
        compiler_params=_cparams(("parallel", "parallel")),
        name="mods",
    )(c_all, w, b)


def _pool_kernel(x_ref, pre_ref, mod_ref, g_ref, pw_ref, ps_ref, o_ref, np_ref, hb_ref, *, tt, pos0, d):
    i = pl.program_id(1)
    x = x_ref[0]
    mod = mod_ref[0]
    h = _adaln(x, g_ref[...], mod[:, :d], mod[:, d:2 * d])

    @pl.when(i == 0)
    def _():
        hb_ref[0:POOL_HALO, :] = pre_ref[0]

    hb_ref[POOL_HALO:POOL_HALO + tt, :] = h
    pos = pos0 + i * tt + lax.broadcasted_iota(jnp.int32, (tt, 1), 0)
    pg = d // len(POOL_WINDOWS)
    ys = []
    for gi, w in enumerate(POOL_WINDOWS):
        c0 = gi * pg
        hg = h[:, c0:c0 + pg]
        s = hg
        for j in range(1, w):
            s = s + hb_ref[POOL_HALO - j:POOL_HALO - j + tt, c0:c0 + pg]
        cnt = jnp.minimum(w, pos + 1).astype(F32)
        pooled = s / cnt - hg
        ys.append(_dot(pooled.astype(BF16), pw_ref[gi]))
    y = jnp.concatenate(ys, axis=1) * ps_ref[...]
    o_ref[0] = x + mod[:, 2 * d:] * y
    last = hb_ref[tt:tt + POOL_HALO, :]
    np_ref[0] = last
    hb_ref[0:POOL_HALO, :] = last


def _pool_layer(x, prefix16, mod, g, pw, ps, pos0, tt):
    b, t, d = x.shape
    kern = functools.partial(_pool_kernel, tt=tt, pos0=pos0, d=d)
    pg = d // len(POOL_WINDOWS)
    return pl.pallas_call(
        kern,
        out_shape=(jax.ShapeDtypeStruct((b, t, d), F32), jax.ShapeDtypeStruct((b, POOL_HALO, d), F32)),
        grid=(b, t // tt),
        in_specs=[pl.BlockSpec((1, tt, d), lambda bi, i: (bi, i, 0)),
                  pl.BlockSpec((1, POOL_HALO, d), lambda bi, i: (bi, 0, 0)),
                  pl.BlockSpec((1, 1, 3 * d), lambda bi, i: (bi, 0, 0)),
                  pl.BlockSpec((1, d), lambda bi, i: (0, 0)),
                  pl.BlockSpec((len(POOL_WINDOWS), pg, pg), lambda bi, i: (0, 0, 0)),
                  pl.BlockSpec((1, d), lambda bi, i: (0, 0))],
        out_specs=(pl.BlockSpec((1, tt, d), lambda bi, i: (bi, i, 0)),
                   pl.BlockSpec((1, POOL_HALO, d), lambda bi, i: (bi, 0, 0))),
        scratch_shapes=[pltpu.VMEM((POOL_HALO + max(tt, 8), d), F32)],
        compiler_params=_cparams(("parallel", "arbitrary")),
        name="pool_layer",
    )(x, prefix16, mod, g, pw, ps)


def _ffn_kernel(x_ref, mod_ref, g_ref, wgu_ref, wd_ref, *rest, d, tf, with_attn):
    ff = wd_ref.shape[0]
    x = x_ref[0]
    if with_attn:
        a_ref, amod_ref, wo_ref, o_ref = rest
        x = x + amod_ref[0][:, 2 * d:] * _dot(a_ref[0].astype(BF16), wo_ref[...])
    else:
        (o_ref,) = rest
    mod = mod_ref[0]
    hb = _adaln(x, g_ref[...], mod[:, :d], mod[:, d:2 * d]).astype(BF16)
    acc = jnp.zeros(x.shape, F32)
    for f in range(ff // tf):
        g = _dot(hb, wgu_ref[:, f * tf:(f + 1) * tf])
        u = _dot(hb, wgu_ref[:, ff + f * tf:ff + (f + 1) * tf])
        a = (g * (1.0 / (1.0 + jnp.exp(-g)))) * u
        acc = acc + _dot(a.astype(BF16), wd_ref[f * tf:(f + 1) * tf, :])
    o_ref[0] = x + mod[:, 2 * d:] * acc


def _ffn(x, mod, g, w_gu, w_dn, tm, tf, attn=None):
    b, t, d = x.shape
    rm = mod.shape[1]
    ff = w_dn.shape[0]
    kern = functools.partial(_ffn_kernel, d=d, tf=tf, with_attn=attn is not None)
    once = pl.Buffered(1)
    operands = [x, mod, g, w_gu, w_dn]
    in_specs = [pl.BlockSpec((1, tm, d), lambda bi, i: (bi, i, 0)),
                pl.BlockSpec((1, rm, 3 * d), lambda bi, i: (bi, 0, 0)),
                pl.BlockSpec((1, d), lambda bi, i: (0, 0)),
                pl.BlockSpec((d, 2 * ff), lambda bi, i: (0, 0), pipeline_mode=once),
                pl.BlockSpec((ff, d), lambda bi, i: (0, 0), pipeline_mode=once)]
    if attn is not None:
        operands += list(attn)
        in_specs += [pl.BlockSpec((1, tm, d), lambda bi, i: (bi, i, 0)),
                     pl.BlockSpec((1, rm, 3 * d), lambda bi, i: (bi, 0, 0)),
                     pl.BlockSpec((d, d), lambda bi, i: (0, 0), pipeline_mode=once)]
    return pl.pallas_call(
        kern,
        out_shape=jax.ShapeDtypeStruct((b, t, d), F32),
        grid=(b, t // tm),
        in_specs=in_specs,
        out_specs=pl.BlockSpec((1, tm, d), lambda bi, i: (bi, i, 0)),
        compiler_params=_cparams(("parallel", "parallel")),
        name="ffn",
    )(*operands)


def _kv_kernel(x_ref, mod_ref, g_ref, w_ref, kn_ref, seg_ref, cos_ref, su_ref, sd_ref, place_ref, one_ref,
               *out_refs, d, transposed):
    hq = N_KV_HEADS * HEAD_DIM
    mod = mod_ref[0]
    h = _adaln(x_ref[0], g_ref[...], mod[:, :d], mod[:, d:2 * d])
    proj = _dot(h.astype(BF16), w_ref[...])
    seg = seg_ref[...]
    cos, su, sd = cos_ref[...], su_ref[...], sd_ref[...]
    k_sel = _rope(_head_rms(proj[:, 2 * hq:3 * hq], kn_ref[0], seg), cos, su, sd)
    k_win = _rope(_head_rms(proj[:, 4 * hq:5 * hq], kn_ref[1], seg), cos, su, sd)
    v_sel = proj[:, 3 * hq:4 * hq]
    v_win = proj[:, 5 * hq:6 * hq]
    if not transposed:
        rows_ref, win_ref = out_refs
        rows_ref[0] = jnp.concatenate([proj[:, :2 * hq], k_sel, v_sel], axis=1)
        win_ref[0] = jnp.concatenate([k_win, v_win], axis=1)
        return
    rows_t_ref, win_t_ref, kst_ref, vs_ref, kwt_ref, vw_ref = out_refs
    tm = k_sel.shape[0]
    zeros = jnp.zeros((HEAD_DIM, tm), BF16)
    k_sel_t = k_sel.T
    k_win_t = k_win.T
    for t_ref, slabs in ((rows_t_ref, (proj[:, :hq].T, proj[:, hq:2 * hq].T, k_sel_t, v_sel.T)),
                         (win_t_ref, (k_win_t, v_win.T))):
        for sl, slab in enumerate(slabs):
            for k in range(N_KV_HEADS):
                t_ref[0, sl, k] = slab[k * HEAD_DIM:(k + 1) * HEAD_DIM, :]
    for kt_ref, kt32 in ((kst_ref, k_sel_t), (kwt_ref, k_win_t)):
        kt = kt32.astype(BF16)
        for k in range(N_KV_HEADS):
            kt_ref[0, k, 0:HEAD_DIM, :] = kt[k * HEAD_DIM:(k + 1) * HEAD_DIM, :]
            kt_ref[0, k, HEAD_DIM:2 * HEAD_DIM, :] = zeros
    for va_ref, vv in ((vs_ref, v_sel), (vw_ref, v_win)):
        va = (_dot(vv.astype(BF16), place_ref[...]) + one_ref[...]).astype(BF16)
        for k in range(N_KV_HEADS):
            va_ref[0, k] = va[:, k * LANES:(k + 1) * LANES]


def _kv_proj(x, mod, g, w_kv, kn, seg, tabs, place, one_row, tm, transposed):
    b, t, d = x.shape
    rm = mod.shape[1]
    hq = N_KV_HEADS * HEAD_DIM
    kern = functools.partial(_kv_kernel, d=d, transposed=transposed)
    const2 = lambda bi, i: (0, 0)
    tab_spec = pl.BlockSpec((tm, LANES), lambda bi, i: (i, 0))
    if transposed:
        kt_shape = jax.ShapeDtypeStruct((b, N_KV_HEADS, 2 * HEAD_DIM, t), BF16)
        va_shape = jax.ShapeDtypeStruct((b, N_KV_HEADS, t, LANES), BF16)
        kt_spec = pl.BlockSpec((1, N_KV_HEADS, 2 * HEAD_DIM, tm), lambda bi, i: (bi, 0, 0, i))
        va_spec = pl.BlockSpec((1, N_KV_HEADS, tm, LANES), lambda bi, i: (bi, 0, i, 0))
        out_shape = [jax.ShapeDtypeStruct((b, 4, N_KV_HEADS, HEAD_DIM, t), F32),
                     jax.ShapeDtypeStruct((b, 2, N_KV_HEADS, HEAD_DIM, t), F32), kt_shape, va_shape, kt_shape, va_shape]
        out_specs = [pl.BlockSpec((1, 4, N_KV_HEADS, HEAD_DIM, tm), lambda bi, i: (bi, 0, 0, 0, i)),
                     pl.BlockSpec((1, 2, N_KV_HEADS, HEAD_DIM, tm), lambda bi, i: (bi, 0, 0, 0, i)),
                     kt_spec, va_spec, kt_spec, va_spec]
    else:
        out_shape = [jax.ShapeDtypeStruct((b, t, 4 * hq), F32), jax.ShapeDtypeStruct((b, t, 2 * hq), F32)]
        out_specs = [pl.BlockSpec((1, tm, 4 * hq), lambda bi, i: (bi, i, 0)),
                     pl.BlockSpec((1, tm, 2 * hq), lambda bi, i: (bi, i, 0))]
    return pl.pallas_call(
        kern,
        out_shape=tuple(out_shape),
        grid=(b, t // tm),
        in_specs=[pl.BlockSpec((1, tm, d), lambda bi, i: (bi, i, 0)),
                  pl.BlockSpec((1, rm, 2 * d), lambda bi, i: (bi, 0, 0)),
                  pl.BlockSpec((1, d), const2),
                  pl.BlockSpec((d, 6 * hq), const2),
                  pl.BlockSpec((2, 1, hq), lambda bi, i: (0, 0, 0)),
                  pl.BlockSpec((hq, hq), const2),
                  tab_spec, tab_spec, tab_spec,
                  pl.BlockSpec((hq, N_KV_HEADS * LANES), const2),
                  pl.BlockSpec((1, N_KV_HEADS * LANES), const2)],
        out_specs=tuple(out_specs),
        compiler_params=_cparams(("parallel", "parallel")),
        name="kv_proj",
    )(x, mod, g, w_kv, kn, seg, *tabs, place, one_row)


def _compress_kernel(pt_ref, src_ref, pe_ref, w1_ref, w2_ref, o_ref, buf_ref, sem_ref, *, n_pages, n_b, paged):
    n_tiles = n_pages * N_KV_HEADS
    sl = pl.program_id(0)
    b = pl.program_id(1)
    step = sl * n_b + b
    cur = step % 2

    def tile_copy(sl_, b_, slot_, p, h):
        if paged:
            src = src_ref.at[pt_ref[b_, p], sl_, h]
        else:
            src = src_ref.at[b_, sl_, h, :, pl.ds(p * PAGE_SIZE, PAGE_SIZE)]
        ti = p * N_KV_HEADS + h
        return pltpu.make_async_copy(src, buf_ref.at[slot_, pl.ds(ti * TILE_ROW_STRIDE, HEAD_DIM), :],
                                     sem_ref.at[slot_])

    def issue(sl_, b_, slot_):
        for p in range(n_pages):
            for h in range(N_KV_HEADS):
                tile_copy(sl_, b_, slot_, p, h).start()

    @pl.when(step == 0)
    def _():
        issue(sl, b, 0)

    @pl.when(step + 1 < 2 * n_b)
    def _():
        nxt = step + 1
        issue(nxt // n_b, nxt % n_b, 1 - cur)

    for p in range(n_pages):
        for h in range(N_KV_HEADS):
            tile_copy(sl, b, cur, p, h).wait()

    pe = pe_ref[0]
    hid = jnp.zeros((n_tiles, 2 * LANES), F32)
    for dp in range(HEAD_DIM // 2):
        xs = [buf_ref[cur, pl.ds(dd, n_tiles, stride=TILE_ROW_STRIDE), :] + pe[dd:dd + 1, :]
              for dd in (2 * dp, 2 * dp + 1)]
        hid = hid + _dot(jnp.concatenate(xs, axis=1).astype(BF16), w1_ref[0, dp])
    o_ref[0, 0] = _dot(_gelu_tanh(hid).astype(BF16), w2_ref[0])


def _compress(page_table, src, pe_t, w1cat, w2bd, paged):
    n_b, n_pages = page_table.shape
    n_tiles = n_pages * N_KV_HEADS
    kern = functools.partial(_compress_kernel, n_pages=n_pages, n_b=n_b, paged=paged)
    grid_spec = pltpu.PrefetchScalarGridSpec(
        num_scalar_prefetch=1,
        grid=(2, n_b),
        in_specs=[pl.BlockSpec(memory_space=pl.ANY),
                  pl.BlockSpec((1, HEAD_DIM, LANES), lambda s, b, pt: (s, 0, 0)),
                  pl.BlockSpec((1, HEAD_DIM // 2, 2 * LANES, 2 * LANES), lambda s, b, pt: (s, 0, 0, 0)),
                  pl.BlockSpec((1, 2 * LANES, LANES), lambda s, b, pt: (s, 0, 0))],
        out_specs=pl.BlockSpec((1, 1, n_tiles, LANES), lambda s, b, pt: (s, b, 0, 0)),
        scratch_shapes=[pltpu.VMEM((2, n_tiles * TILE_ROW_STRIDE, LANES), F32),
                        pltpu.SemaphoreType.DMA((2,))])
    raw = pl.pallas_call(
        kern,
        out_shape=jax.ShapeDtypeStruct((2, n_b, n_tiles, LANES), F32),
        grid_spec=grid_spec,
        compiler_params=_cparams(("arbitrary", "arbitrary")),
        name="compress",
    )(page_table, src, pe_t, w1cat, w2bd)
    raw = raw.reshape(2, n_b, n_pages, N_KV_HEADS, 2, HEAD_DIM).transpose(0, 1, 2, 4, 3, 5)
    return raw.reshape(2, n_b, 2 * n_pages, N_KV_HEADS * HEAD_DIM)


def _tail_kernel(z_ref, pe_ref, w1_ref, w2_ref, o_ref):
    z = z_ref[...] + pe_ref[0]
    hid = _gelu_tanh(_dot(z.astype(BF16), w1_ref[0]))
    o_ref[0] = _dot(hid.astype(BF16), w2_ref[0])


def _compress_tail(z, pe_flat, w1, w2):
    m, kdim = z.shape[0] // 2, z.shape[1]
    hid = w1.shape[2]
    return pl.pallas_call(
        _tail_kernel,
        out_shape=jax.ShapeDtypeStruct((2, m, HEAD_DIM), F32),
        grid=(2,),
        in_specs=[pl.BlockSpec((m, kdim), lambda s: (s, 0)),
                  pl.BlockSpec((1, 1, kdim), lambda s: (s, 0, 0)),
                  pl.BlockSpec((1, kdim, hid), lambda s: (s, 0, 0)),
                  pl.BlockSpec((1, hid, HEAD_DIM), lambda s: (s, 0, 0))],
        out_specs=pl.BlockSpec((1, m, HEAD_DIM), lambda s: (s, 0, 0)),
        compiler_params=_cparams(("parallel",)),
        name="compress_tail",
    )(z, pe_flat, w1, w2)


def _cmp_finish_kernel(kc_ref, vc_ref, kn_ref, seg_ref, cos_ref, su_ref, sd_ref, place_ref, kt_ref, vp_ref):
    kc = _rope(_head_rms(kc_ref[0], kn_ref[...], seg_ref[...]), cos_ref[...], su_ref[...], sd_ref[...])
    kt = kc.T.astype(BF16)
    nbp = kc.shape[0]
    zeros = jnp.zeros((HEAD_DIM, nbp), BF16)
    vp = _dot(vc_ref[0].astype(BF16), place_ref[...]).astype(BF16)
    for k in range(N_KV_HEADS):
        kt_ref[0, k, 0:HEAD_DIM, :] = kt[k * HEAD_DIM:(k + 1) * HEAD_DIM, :]
        kt_ref[0, k, HEAD_DIM:2 * HEAD_DIM, :] = zeros
        vp_ref[0, k] = vp[:, k * LANES:(k + 1) * LANES]


def _cmp_finish(kc_raw, vc_raw, kn0, seg, tabs, place):
    b, nbp, hq = kc_raw.shape
    const2 = lambda bi: (0, 0)
    tab_spec = pl.BlockSpec((nbp, LANES), const2)
    return pl.pallas_call(
        _cmp_finish_kernel,
        out_shape=(jax.ShapeDtypeStruct((b, N_KV_HEADS, 2 * HEAD_DIM, nbp), BF16),
                   jax.ShapeDtypeStruct((b, N_KV_HEADS, nbp, LANES), BF16)),
        grid=(b,),
        in_specs=[pl.BlockSpec((1, nbp, hq), lambda bi: (bi, 0, 0)),
                  pl.BlockSpec((1, nbp, hq), lambda bi: (bi, 0, 0)),
                  pl.BlockSpec((1, hq), const2),
                  pl.BlockSpec((hq, hq), const2),
                  tab_spec, tab_spec, tab_spec,
                  pl.BlockSpec((hq, N_KV_HEADS * LANES), const2)],
        out_specs=(pl.BlockSpec((1, N_KV_HEADS, 2 * HEAD_DIM, nbp), lambda bi: (bi, 0, 0, 0)),
                   pl.BlockSpec((1, N_KV_HEADS, nbp, LANES), lambda bi: (bi, 0, 0, 0))),
        compiler_params=_cparams(("parallel",)),
        name="cmp_finish",
    )(kc_raw, vc_raw, kn0, seg, *tabs, place)


def _q_kernel(x_ref, mod_ref, g_ref, wq_ref, wg_ref, qn_ref, seg_ref, cos_ref, su_ref, sd_ref,
              q_ref, gate_ref, *, d, q_scale):
    hq = N_KV_HEADS * HEAD_DIM
    mod = mod_ref[0]
    hb = _adaln(x_ref[0], g_ref[...], mod[:, :d], mod[:, d:2 * d]).astype(BF16)
    q = _dot(hb, wq_ref[...])
    seg = seg_ref[...]
    cos, su, sd = cos_ref[...], su_ref[...], sd_ref[...]
    for c in range(d // hq):
        qc = _rope(_head_rms(q[:, c * hq:(c + 1) * hq], qn_ref[...], seg), cos, su, sd) * q_scale
        q_ref[0, :, c * hq:(c + 1) * hq] = qc.astype(q_ref.dtype)
    gates = 1.0 / (1.0 + jnp.exp(-_dot(hb, wg_ref[...])))
    for k in range(N_KV_HEADS):
        gate_ref[0, k] = gates[:, k * LANES:(k + 1) * LANES]


def _q_proj(x, mod, g, w_q, w_g, qn, seg, tabs, tm, q_scale, q_dtype):
    b, t, d = x.shape
    rm = mod.shape[1]
    hq = N_KV_HEADS * HEAD_DIM
    ng = w_g.shape[1]
    kern = functools.partial(_q_kernel, d=d, q_scale=q_scale)
    const2 = lambda bi, i: (0, 0)
    tab_spec = pl.BlockSpec((tm, LANES), lambda bi, i: (i, 0))
    return pl.pallas_call(
        kern,
        out_shape=(jax.ShapeDtypeStruct((b, t, d), q_dtype),
                   jax.ShapeDtypeStruct((b, N_KV_HEADS, t, LANES), F32)),
        grid=(b, t // tm),
        in_specs=[pl.BlockSpec((1, tm, d), lambda bi, i: (bi, i, 0)),
                  pl.BlockSpec((1, rm, 3 * d), lambda bi, i: (bi, 0, 0)),
                  pl.BlockSpec((1, d), const2),
                  pl.BlockSpec((d, d), const2),
                  pl.BlockSpec((d, ng), const2),
                  pl.BlockSpec((1, hq), const2),
                  pl.BlockSpec((hq, hq), const2),
                  tab_spec, tab_spec, tab_spec],
        out_specs=(pl.BlockSpec((1, tm, d), lambda bi, i: (bi, i, 0)),
                   pl.BlockSpec((1, N_KV_HEADS, tm, LANES), lambda bi, i: (bi, 0, i, 0))),
        compiler_params=_cparams(("parallel", "parallel")),
        name="q_proj",
    )(x, mod, g, w_q, w_g, qn, seg, *tabs)


def _select_kernel(q_ref, kct_ref, vcp_ref, pin_ref, bias_ref, oc_ref, *, tq, n_sub, nb):
    gq = q_ref.shape[2] // HEAD_DIM
    rows = gq * tq
    i = pl.program_id(2)
    for u in range(n_sub):
        t0 = (i * n_sub + u) * tq
        qslab = q_ref[0, u * tq:(u + 1) * tq, :]
        qp = jnp.concatenate([_dot(qslab, pin_ref[g]).astype(BF16) for g in range(gq)], axis=0)
        pos_r = t0 + lax.broadcasted_iota(jnp.int32, (rows, 1), 0) % tq
        s_c = _dot(qp, kct_ref[0, 0])
        blk = lax.broadcasted_iota(jnp.int32, (rows, nb), 1)
        ok_c = (blk + 1) * SEL_BLOCK - 1 <= pos_r
        m_c = jnp.max(jnp.where(ok_c, s_c, -jnp.inf), axis=1, keepdims=True)
        m_c = jnp.where(m_c == -jnp.inf, 0.0, m_c)
        p_c = jnp.where(ok_c, jnp.exp2(s_c - m_c), 0.0)
        p_c = p_c / jnp.maximum(jnp.sum(p_c, axis=1, keepdims=True), 1e-30)
        o_c = _dot(p_c.astype(BF16), vcp_ref[0, 0]).astype(BF16)
        for g in range(gq):
            oc_ref[0, 0, g, u * tq:(u + 1) * tq, :] = o_c[g * tq:(g + 1) * tq]

        imp = p_c[0:tq]
        for g in range(1, gq):
            imp = imp + p_c[g * tq:(g + 1) * tq]
        imp_t = imp.T
        n_io = lax.broadcasted_iota(jnp.int32, (nb, tq), 0)
        cur = (t0 + lax.broadcasted_iota(jnp.int32, (nb, tq), 1)) // SEL_BLOCK
        forced = (n_io == 0) | (n_io == cur) | (n_io == cur - 1)
        v = jnp.where(forced, -2.0, jnp.where(n_io > cur, -1.0, imp_t))
        for _ in range(N_SEL - N_FORCED):
            mx = jnp.max(v, axis=0, keepdims=True)
            first = jnp.min(jnp.where(v == mx, n_io, nb), axis=0, keepdims=True)
            v = jnp.where(n_io == first, -2.0, v)
        bias_t = jnp.where((v == -2.0) & (n_io <= cur), 0.0, NEG)
        bias_ref[0, 0, u * tq:(u + 1) * tq, :] = bias_t.T.astype(BF16)


def _select(q, kct, vcp, pin, tq, n_sub):
    b, t, d = q.shape
    nb = kct.shape[3]
    gq = d // (N_KV_HEADS * HEAD_DIM)
    sw = gq * HEAD_DIM
    ts = tq * n_sub
    kern = functools.partial(_select_kernel, tq=tq, n_sub=n_sub, nb=nb)
    per_head4 = lambda bi, k, i: (bi, k, 0, 0)
    return pl.pallas_call(
        kern,
        out_shape=(jax.ShapeDtypeStruct((b, N_KV_HEADS, t, nb), BF16),
                   jax.ShapeDtypeStruct((b, N_KV_HEADS, gq, t, LANES), BF16)),
        grid=(b, N_KV_HEADS, t // ts),
        in_specs=[pl.BlockSpec((1, ts, sw), lambda bi, k, i: (bi, i, k)),
                  pl.BlockSpec((1, 1, 2 * HEAD_DIM, nb), per_head4),
                  pl.BlockSpec((1, 1, nb, LANES), per_head4),
                  pl.BlockSpec((gq, sw, LANES), lambda bi, k, i: (0, 0, 0))],
        out_specs=(pl.BlockSpec((1, 1, ts, nb), lambda bi, k, i: (bi, k, i, 0)),
                   pl.BlockSpec((1, 1, gq, ts, LANES), lambda bi, k, i: (bi, k, 0, i, 0))),
        compiler_params=_cparams(("parallel", "parallel", "parallel")),
        name="cmp_select",
    )(q, kct, vcp, pin)


def _attn_kernel(q_ref, gate_ref, bias_ref, oc_ref, kst_ref, vs_ref, kwt_ref, vw_ref, e_ref, pin_ref, pout_ref,
                 dmask_ref, wmask_ref, o_ref, qa_ref, m_ref, acc_ref, s_ref, *, tq, ck, ck_big, nb):
    gq = q_ref.shape[2] // HEAD_DIM
    rows = gq * tq
    i = pl.program_id(2)
    t0 = i * tq

    qslab = q_ref[0]
    qp = jnp.concatenate([_dot(qslab, pin_ref[g]).astype(BF16) for g in range(gq)], axis=0)
    qa_ref[...] = jnp.concatenate([jnp.concatenate([bias_ref[0, 0]] * gq, axis=0), qp], axis=1)

    m_ref[...] = jnp.full(m_ref.shape, NEG * 4.0, F32)

    def score_chunk(k0, width, causal):
        kaug = jnp.concatenate([e_ref[:, pl.ds(k0, width)], kst_ref[0, 0, :, pl.ds(k0, width)]], axis=0)
        s = _dot(qa_ref[...], kaug)
        if causal:
            s = s + jnp.concatenate([dmask_ref[0]] * gq, axis=0)
        s_ref[:, pl.ds(k0, width)] = s
        m = m_ref[...]
        for a in range(width // LANES):
            m = jnp.maximum(m, s[:, a * LANES:(a + 1) * LANES])
        m_ref[...] = m

    def value_chunk(k0, width):
        mb = m_ref[...]
        p = jnp.exp2(s_ref[:, pl.ds(k0, width)] - jnp.concatenate([mb] * (width // LANES), axis=1))
        acc_ref[...] += _dot(p.astype(BF16), vs_ref[0, 0, pl.ds(k0, width), :])

    ck_mid = 2 * ck
    c_last = t0 // ck
    n_big = c_last // (ck_big // ck)
    rem = c_last - n_big * (ck_big // ck)
    n_mid = rem // 2
    n_small = rem - 2 * n_mid
    mid0 = n_big * ck_big
    small0 = pl.multiple_of(mid0 + n_mid * ck_mid, ck)

    def loop(n, fn):
        def body(c, carry):
            fn(c)
            return carry
        lax.fori_loop(0, n, body, 0)

    loop(n_big, lambda c: score_chunk(pl.multiple_of(c * ck_big, ck_big), ck_big, False))
    loop(n_mid, lambda c: score_chunk(pl.multiple_of(mid0 + c * ck_mid, ck_mid), ck_mid, False))
    loop(n_small, lambda c: score_chunk(small0, ck, False))
    score_chunk(pl.multiple_of(c_last * ck, ck), ck, True)

    m_row = jnp.max(m_ref[...], axis=1, keepdims=True)
    m_ref[...] = jnp.broadcast_to(m_row, m_ref.shape)
    acc_ref[...] = jnp.zeros_like(acc_ref)
    loop(n_big, lambda c: value_chunk(pl.multiple_of(c * ck_big, ck_big), ck_big))
    loop(n_mid, lambda c: value_chunk(pl.multiple_of(mid0 + c * ck_mid, ck_mid), ck_mid))

    @pl.when(n_small == 1)
    def _():
        value_chunk(small0, ck_mid)

    @pl.when(n_small == 0)
    def _():
        value_chunk(small0, ck)

    acc = acc_ref[...]
    o_s = acc / acc[:, HEAD_DIM:HEAD_DIM + 1]

    wl = WINDOW + tq
    w0 = pl.multiple_of(jnp.maximum(t0 - WINDOW, 0), LANES)
    s_w = _dot(qp, kwt_ref[0, 0, :, pl.ds(w0, wl)]) + jnp.concatenate([wmask_ref[0]] * gq, axis=0)
    p_w = jnp.exp2(s_w - jnp.max(s_w, axis=1, keepdims=True))
    acc_w = _dot(p_w.astype(BF16), vw_ref[0, 0, pl.ds(w0, wl), :])
    o_w = acc_w / acc_w[:, HEAD_DIM:HEAD_DIM + 1]

    o_c = oc_ref[0, 0].reshape(rows, LANES).astype(F32)
    gates = gate_ref[0, 0]
    out = jnp.zeros((tq, gq * HEAD_DIM), F32)
    for g in range(gq):
        r = slice(g * tq, (g + 1) * tq)
        mix = (gates[:, 3 * g:3 * g + 1] * o_c[r] + gates[:, 3 * g + 1:3 * g + 2] * o_s[r]
               + gates[:, 3 * g + 2:3 * g + 3] * o_w[r])
        out = out + _dot(mix.astype(BF16), pout_ref[g])
    o_ref[0] = out.astype(o_ref.dtype)


def _attention(q, gates, bias, oc, kst, vs, kwt, vw, e_mat, pin, pout, tq, ck, ck_big):
    b, t, d = q.shape
    nb = bias.shape[3]
    gq = d // (N_KV_HEADS * HEAD_DIM)
    sw = gq * HEAD_DIM
    rows = gq * tq
    kern = functools.partial(_attn_kernel, tq=tq, ck=ck, ck_big=ck_big, nb=nb)
    per_head4 = lambda bi, k, i: (bi, k, 0, 0)
    once = pl.Buffered(1)
    tt = jnp.arange(tq)[None, :, None]
    n_phase = ck // tq
    jd = jnp.arange(ck)[None, None, :]
    dmask = jnp.where(jd <= jnp.arange(n_phase)[:, None, None] * tq + tt, 0.0, NEG).astype(F32)
    wl = WINDOW + tq
    n_early = WINDOW // tq
    jw = jnp.arange(wl)[None, None, :]
    pos_e = jnp.arange(n_early)[:, None, None] * tq + tt
    early = (jw <= pos_e) & (pos_e - jw < WINDOW)
    steady = (jw > tt) & (jw <= tt + WINDOW)
    wmask = jnp.where(jnp.concatenate([early, steady], axis=0), 0.0, NEG).astype(F32)
    return pl.pallas_call(
        kern,
        out_shape=jax.ShapeDtypeStruct((b, t, d), BF16),
        grid=(b, N_KV_HEADS, t // tq),
        in_specs=[pl.BlockSpec((1, tq, sw), lambda bi, k, i: (bi, i, k)),
                  pl.BlockSpec((1, 1, tq, LANES), lambda bi, k, i: (bi, k, i, 0)),
                  pl.BlockSpec((1, 1, tq, nb), lambda bi, k, i: (bi, k, i, 0)),
                  pl.BlockSpec((1, 1, gq, tq, LANES), lambda bi, k, i: (bi, k, 0, i, 0)),
                  pl.BlockSpec((1, 1, 2 * HEAD_DIM, t), per_head4, pipeline_mode=once),
                  pl.BlockSpec((1, 1, t, LANES), per_head4, pipeline_mode=once),
                  pl.BlockSpec((1, 1, 2 * HEAD_DIM, t), per_head4, pipeline_mode=once),
                  pl.BlockSpec((1, 1, t, LANES), per_head4, pipeline_mode=once),
                  pl.BlockSpec((nb, t), lambda bi, k, i: (0, 0), pipeline_mode=once),
                  pl.BlockSpec((gq, sw, LANES), lambda bi, k, i: (0, 0, 0)),
                  pl.BlockSpec((gq, LANES, sw), lambda bi, k, i: (0, 0, 0)),
                  pl.BlockSpec((1, tq, ck), lambda bi, k, i: (i % n_phase, 0, 0)),
                  pl.BlockSpec((1, tq, wl), lambda bi, k, i: (jnp.minimum(i, n_early), 0, 0))],
        out_specs=pl.BlockSpec((1, tq, sw), lambda bi, k, i: (bi, i, k)),
        scratch_shapes=[pltpu.VMEM((rows, nb + LANES), BF16),
                        pltpu.VMEM((rows, LANES), F32),
                        pltpu.VMEM((rows, LANES), F32),
                        pltpu.VMEM((rows, t), F32)],
        compiler_params=_cparams(("parallel", "parallel", "arbitrary")),
        name="nsa_attention",
    )(q, gates, bias, oc, kst, vs, kwt, vw, e_mat, pin, pout, dmask, wmask)


def _row_kvh(shape, gq):
    return lax.broadcasted_iota(jnp.int32, shape, 0) // gq


def _dec_a_kernel(q_ref, kct_ref, vcp_ref, oc_ref, idx_ref, *, pos, nb, gq, bb):
    nh = q_ref.shape[1]
    nbp = kct_ref.shape[3]
    rk = _row_kvh((nh, nbp), gq)
    rk_o = _row_kvh((nh, LANES), gq)
    blk = lax.broadcasted_iota(jnp.int32, (nh, nbp), 1)
    ok = ((blk + 1) * SEL_BLOCK - 1 <= pos) & (blk < nb)
    imps = []
    for lb in range(bb):
        qb = q_ref[lb].astype(BF16)
        s = jnp.zeros((nh, nbp), F32)
        for k in range(N_KV_HEADS):
            s = jnp.where(rk == k, _dot(qb, kct_ref[lb, k]), s)
        m = jnp.max(jnp.where(ok, s, -jnp.inf), axis=1, keepdims=True)
        m = jnp.where(m == -jnp.inf, 0.0, m)
        p = jnp.where(ok, jnp.exp(s - m), 0.0)
        p = p / jnp.maximum(jnp.sum(p, axis=1, keepdims=True), 1e-30)
        pb = p.astype(BF16)
        o_c = jnp.zeros((nh, LANES), F32)
        for k in range(N_KV_HEADS):
            o_c = jnp.where(rk_o == k, _dot(pb, vcp_ref[lb, k]), o_c)
        oc_ref[lb] = o_c
        imps += [jnp.sum(jnp.where(rk == k, p, 0.0), axis=0, keepdims=True) for k in range(N_KV_HEADS)]
    imp = jnp.concatenate(imps, axis=0)
    nr = bb * N_KV_HEADS
    n_io = lax.broadcasted_iota(jnp.int32, (nr, nbp), 1)
    cur = pos // SEL_BLOCK
    forced = (n_io == 0) | (n_io == cur) | (n_io == cur - 1)
    v = jnp.where(forced, FORCE_SCORE, jnp.where(n_io > cur, -1.0, imp))
    v = jnp.where(n_io < nb, v, -3.0)
    col = lax.broadcasted_iota(jnp.int32, (nr, N_SEL), 1)
    idx = jnp.zeros((nr, N_SEL), jnp.int32)
    for r in range(N_SEL):
        mx = jnp.max(v, axis=1, keepdims=True)
        first = jnp.min(jnp.where(v == mx, n_io, nbp), axis=1, keepdims=True)
        idx = jnp.where(col == r, first, idx)
        v = jnp.where(n_io == first, -4.0, v)
    idx_ref[...] = idx


def _decode_a(q_pad, kct, vcp, pos, nb, gq, bb):
    b, nh, _ = q_pad.shape
    nbp = kct.shape[3]
    kern = functools.partial(_dec_a_kernel, pos=pos, nb=nb, gq=gq, bb=bb)
    return pl.pallas_call(
        kern,
        out_shape=(jax.ShapeDtypeStruct((b, nh, LANES), F32),
                   jax.ShapeDtypeStruct((b * N_KV_HEADS, N_SEL), jnp.int32)),
        grid=(b // bb,),
        in_specs=[pl.BlockSpec((bb, nh, LANES), lambda bi: (bi, 0, 0)),
                  pl.BlockSpec((bb, N_KV_HEADS, 2 * HEAD_DIM, nbp), lambda bi: (bi, 0, 0, 0)),
                  pl.BlockSpec((bb, N_KV_HEADS, nbp, LANES), lambda bi: (bi, 0, 0, 0))],
        out_specs=(pl.BlockSpec((bb, nh, LANES), lambda bi: (bi, 0, 0)),
                   pl.BlockSpec((bb * N_KV_HEADS, N_SEL), lambda bi: (bi, 0))),
        compiler_params=_cparams(("parallel",)),
        name="decode_cmp_topk",
    )(q_pad, kct, vcp)


def _dec_b_kernel(idx_ref, pt_ref, q_ref, gate_ref, oc_ref, new_ref, cache_ref, swin_ref,
                  o_ref, kbuf_ref, vbuf_ref, sem_ref, *, pos, nb_past, gq, n_b):
    nh = q_ref.shape[1]
    b = pl.program_id(0)
    cur = b % 2
    nkeys = N_SEL * PAGE_SIZE

    def copies(b_, slot_, k, j):
        n = idx_ref[b_, k * N_SEL + j]
        page = pt_ref[b_, lax.shift_right_logical(jnp.minimum(n, nb_past - 1), 1)]
        ck = pltpu.make_async_copy(cache_ref.at[page, 2, k], kbuf_ref.at[slot_, k, j], sem_ref.at[slot_])
        cv = pltpu.make_async_copy(cache_ref.at[page, 3, k], vbuf_ref.at[slot_, k, j], sem_ref.at[slot_])
        return n, ck, cv

    def issue(b_, slot_):
        for k in range(N_KV_HEADS):
            for j in range(N_SEL):
                n, ck, cv = copies(b_, slot_, k, j)

                @pl.when(n < nb_past)
                def _():
                    ck.start()
                    cv.start()

    @pl.when(b == 0)
    def _():
        issue(b, 0)

    @pl.when(b + 1 < n_b)
    def _():
        issue(b + 1, 1 - cur)

    q = q_ref[0]
    qb = q.astype(BF16)
    new = new_ref[0]
    eye = (lax.broadcasted_iota(jnp.int32, (HEAD_DIM, HEAD_DIM), 0)
           == lax.broadcasted_iota(jnp.int32, (HEAD_DIM, HEAD_DIM), 1))
    col0 = (lax.broadcasted_iota(jnp.int32, (HEAD_DIM, PAGE_SIZE), 1) == 0).astype(BF16)
    lane = lax.broadcasted_iota(jnp.int32, (1, nkeys), 1)
    r_in = lane % PAGE_SIZE
    rk = _row_kvh((nh, nkeys), gq)
    rk_h = _row_kvh((nh, HEAD_DIM), gq)

    kpos = jnp.zeros((nh, nkeys), jnp.int32)
    half = jnp.zeros((nh, nkeys), jnp.int32)
    for k in range(N_KV_HEADS):
        tail_k = _dot(jnp.where(eye, new[0, k * gq:k * gq + 1, :], 0.0).astype(BF16), col0)
        tail_v = _dot(jnp.where(eye, new[1, k * gq:k * gq + 1, :], 0.0).astype(BF16), col0)
        kp = r_in
        hv = jnp.zeros((1, nkeys), jnp.int32)
        for j in range(N_SEL):
            n, ck, cv = copies(b, cur, k, j)

            @pl.when(n < nb_past)
            def _():
                ck.wait()
                cv.wait()

            @pl.when(n >= nb_past)
            def _():
                kbuf_ref[cur, k, j] = tail_k
                vbuf_ref[cur, k, j] = tail_v

            in_j = lane // PAGE_SIZE == j
            kp = kp + jnp.where(in_j, lax.shift_right_logical(n, 1) * PAGE_SIZE, 0)
            hv = hv + jnp.where(in_j, n & 1, 0)
        kpos = jnp.where(rk == k, kp, kpos)
        half = jnp.where(rk == k, hv, half)
    s = jnp.zeros((nh, nkeys), F32)
    for k in range(N_KV_HEADS):
        sk = jnp.concatenate([_dot(qb, kbuf_ref[cur, k, j].astype(BF16)) for j in range(N_SEL)], axis=1)
        s = jnp.where(rk == k, sk, s)
    ok = (r_in // SEL_BLOCK == half) & (kpos <= pos)
    m = jnp.max(jnp.where(ok, s, -jnp.inf), axis=1, keepdims=True)
    p = jnp.where(ok, jnp.exp(s - m), 0.0)
    p = p / jnp.maximum(jnp.sum(p, axis=1, keepdims=True), 1e-30)
    pb = p.astype(BF16)
    o_s = jnp.zeros((nh, HEAD_DIM), F32)
    for k in range(N_KV_HEADS):
        o_k = jnp.zeros((nh, HEAD_DIM), F32)
        for j in range(N_SEL):
            o_k = o_k + _dot_nt(pb[:, j * PAGE_SIZE:(j + 1) * PAGE_SIZE], vbuf_ref[cur, k, j].astype(BF16))
        o_s = jnp.where(rk_h == k, o_k, o_s)

    w_buf = swin_ref.shape[4]
    rk_w = _row_kvh((nh, w_buf), gq)
    s_w = jnp.zeros((nh, w_buf), F32)
    for k in range(N_KV_HEADS):
        s_w = jnp.where(rk_w == k, _dot(qb, swin_ref[0, 0, k].astype(BF16)), s_w)
    s_n = jnp.sum(q * new[2], axis=1, keepdims=True)
    ridx = lax.broadcasted_iota(jnp.int32, (nh, w_buf), 1)
    ok_w = w_buf - ridx < WINDOW
    m_w = jnp.maximum(jnp.max(jnp.where(ok_w, s_w, -jnp.inf), axis=1, keepdims=True), s_n)
    p_w = jnp.where(ok_w, jnp.exp(s_w - m_w), 0.0)
    p_n = jnp.exp(s_n - m_w)
    l_w = jnp.sum(p_w, axis=1, keepdims=True) + p_n
    pwb = (p_w / l_w).astype(BF16)
    o_w = jnp.zeros((nh, HEAD_DIM), F32)
    for k in range(N_KV_HEADS):
        o_w = jnp.where(rk_h == k, _dot_nt(pwb, swin_ref[0, 1, k].astype(BF16)), o_w)
    o_w = o_w + (p_n / l_w) * new[3]

    gates = gate_ref[0]
    o_ref[0] = gates[:, 0:1] * oc_ref[0][:, :HEAD_DIM] + gates[:, 1:2] * o_s + gates[:, 2:3] * o_w


def _decode_b(idx_flat, page_table, q3, gates, o_c, new_h, cache_t, state_t, pos, gq):
    b, nh, _ = q3.shape
    w_buf = state_t.shape[4]
    nb_past = page_table.shape[1] * (PAGE_SIZE // SEL_BLOCK)
    kern = functools.partial(_dec_b_kernel, pos=pos, nb_past=nb_past, gq=gq, n_b=b)
    grid_spec = pltpu.PrefetchScalarGridSpec(
        num_scalar_prefetch=2,
        grid=(b,),
        in_specs=[pl.BlockSpec((1, nh, HEAD_DIM), lambda bi, ix, pt: (bi, 0, 0)),
                  pl.BlockSpec((1, nh, LANES), lambda bi, ix, pt: (bi, 0, 0)),
                  pl.BlockSpec((1, nh, LANES), lambda bi, ix, pt: (bi, 0, 0)),
                  pl.BlockSpec((1, 4, nh, HEAD_DIM), lambda bi, ix, pt: (bi, 0, 0, 0)),
                  pl.BlockSpec(memory_space=pl.ANY),
                  pl.BlockSpec((1, 2, N_KV_HEADS, HEAD_DIM, w_buf), lambda bi, ix, pt: (bi, 0, 0, 0, 0))],
        out_specs=pl.BlockSpec((1, nh, HEAD_DIM), lambda bi, ix, pt: (bi, 0, 0)),
        scratch_shapes=[pltpu.VMEM((2, N_KV_HEADS, N_SEL, HEAD_DIM, PAGE_SIZE), F32),
                        pltpu.VMEM((2, N_KV_HEADS, N_SEL, HEAD_DIM, PAGE_SIZE), F32),
                        pltpu.SemaphoreType.DMA((2,))])
    return pl.pallas_call(
        kern,
        out_shape=jax.ShapeDtypeStruct((b, nh, HEAD_DIM), F32),
        grid_spec=grid_spec,
        compiler_params=_cparams(("arbitrary",)),
        name="decode_sel_win",
    )(idx_flat, page_table, q3, gates, o_c, new_h, cache_t, state_t)


def _rope_tables(pos):
    half = ROT_DIM // 2
    inv = jnp.power(jnp.float32(ROPE_THETA), -jnp.arange(half, dtype=F32) * 2.0 / ROT_DIM)
    ang = pos.astype(F32)[:, None] * inv[None, :]
    cos, sin = jnp.cos(ang), jnp.sin(ang)
    r = jnp.arange(LANES) % HEAD_DIM
    f = r % half
    cos_t = jnp.where(r < ROT_DIM, cos[:, f], 1.0)
    sin_up = jnp.where(r < half, -sin[:, f], 0.0)
    sin_dn = jnp.where((r >= half) & (r < ROT_DIM), sin[:, f], 0.0)
    return cos_t, sin_up, sin_dn


def _tile_heads(v, n):
    return jnp.tile(v.astype(F32), n)[None, :]


def kernel(x_prompt, x_sample, cache_kv, state_kv_win, state_pool, page_table, c_prompt, c_sample, ada_w, ada_b, norm_mix, norm_ffn, pool_w, pool_scale, ada_kv_w, ada_kv_b, norm_kv, w_kv, k_norm, cmp_pe, cmp_w1, cmp_w2, w_qg, q_norm, w_o, w_gate_up, w_down):
    b_p, seq, d = x_prompt.shape
    b_s, dec_seq, _ = x_sample.shape
    depth = ada_w.shape[0]
    n_a = pool_w.shape[0]
    n_heads = d // HEAD_DIM
    gq = n_heads // N_KV_HEADS
    hq = N_KV_HEADS * HEAD_DIM
    n_pages = page_table.shape[1]
    past_len = n_pages * PAGE_SIZE
    w_buf = state_kv_win.shape[1]
    d_ff = w_down.shape[1]
    assert dec_seq == 1 and hq == 2 * LANES and seq % PAGE_SIZE == 0
    sm = HEAD_DIM ** -0.5

    w_gu_b = w_gate_up.astype(BF16)
    w_dn_b = w_down.astype(BF16)
    w_kv_b = w_kv.astype(BF16)
    pool_w_b = pool_w.astype(BF16)
    w_q_b = w_qg[:, :, :d].astype(BF16)
    wg_cols = w_qg[:, :, d:].reshape(-1, d, N_KV_HEADS, 3 * gq)
    w_g_b = jnp.pad(wg_cols, ((0, 0), (0, 0), (0, 0), (0, LANES - 3 * gq))).reshape(-1, d, N_KV_HEADS * LANES).astype(BF16)
    w_o_b = w_o.astype(BF16)

    head_of = jnp.arange(hq) // HEAD_DIM
    seg = (head_of[:, None] == head_of[None, :]).astype(BF16)
    lane_in = jnp.arange(hq)
    lane_out = jnp.arange(N_KV_HEADS * LANES)
    place = ((lane_out[None, :] // LANES == lane_in[:, None] // HEAD_DIM)
             & (lane_out[None, :] % LANES == lane_in[:, None] % HEAD_DIM)).astype(BF16)
    one_row = (lane_out % LANES == HEAD_DIM).astype(F32)[None, :]
    sw = gq * HEAD_DIM
    cin = jnp.arange(sw)
    pin = jnp.stack([((cin[:, None] // HEAD_DIM == g) & (jnp.arange(LANES)[None, :] == cin[:, None] % HEAD_DIM))
                     for g in range(gq)]).astype(BF16)
    pout = jnp.stack([((jnp.arange(LANES)[:, None] < HEAD_DIM)
                       & (cin[None, :] == g * HEAD_DIM + jnp.arange(LANES)[:, None]))
                      for g in range(gq)]).astype(BF16)

    kn = jnp.stack([_tile_heads(k_norm[1], N_KV_HEADS), _tile_heads(k_norm[2], N_KV_HEADS)])
    kn0 = _tile_heads(k_norm[0], N_KV_HEADS)
    pe_t = jnp.tile(cmp_pe.transpose(0, 2, 1), (1, 1, 2))
    eye2 = jnp.eye(2, dtype=F32)
    w1_dsj = cmp_w1.reshape(2, SEL_BLOCK, HEAD_DIM, -1).transpose(0, 2, 1, 3)
    n_hid = w1_dsj.shape[-1]
    w1cat = jnp.einsum('ab,zdsj->zdasbj', eye2, w1_dsj).reshape(2, HEAD_DIM // 2, 2 * 2 * SEL_BLOCK, 2 * n_hid).astype(BF16)
    w2bd = jnp.einsum('ab,zje->zajbe', eye2, cmp_w2).reshape(2, 2 * n_hid, 2 * HEAD_DIM).astype(BF16)

    c_all = jnp.concatenate([c_prompt, c_sample], axis=0)
    m_all = c_all.shape[0]
    m_pad = -(-m_all // 8) * 8
    c_all = jnp.pad(c_all, ((0, m_pad - m_all), (0, 0)))
    mods = _mods(c_all, ada_w.reshape(depth * 2, d, 3 * d), ada_b.reshape(depth * 2, 1, 3 * d)).reshape(depth, 2, m_pad, 3 * d)
    mod_kv = _mods(c_all, ada_kv_w[None], ada_kv_b[None, None, :])[0]

    def mod_p(l, j):
        return mods[l, j, :b_p][:, None, :]

    def mod_s_tok(l, j):
        return mods[l, j, b_p:m_all][None]

    def mod_s_seq(l, j):
        return mods[l, j, b_p:m_all][:, None, :]

    tm = min(512, seq)
    tf = d_ff
    tq = 256
    tq_sel = min(1024, seq)
    ck = 512
    ck_big = min(2048, seq // 2)

    x = x_prompt
    pool_p = []
    for l in range(n_a):
        x, npool = _pool_layer(x, jnp.zeros((b_p, POOL_HALO, d), F32), mod_p(l, 0), norm_mix[l][None], pool_w_b[l],
                               pool_scale[l][None], 0, tm)
        pool_p.append(npool[:, 1:])
        x = _ffn(x, mod_p(l, 1), norm_ffn[l][None], w_gu_b[l], w_dn_b[l], tm, tf)

    tabs_p = _rope_tables(jnp.arange(seq))
    rows_t, win_t, kst, vs, kwt, vw = _kv_proj(x, mod_kv[:b_p][:, None, :], norm_kv[None], w_kv_b, kn, seg, tabs_p,
                                               place, one_row, tm, True)
    nb_p = seq // SEL_BLOCK
    pt_p = jnp.zeros((b_p, seq // PAGE_SIZE), jnp.int32)
    raw_p = _compress(pt_p, rows_t, pe_t, w1cat, w2bd, False)
    tabs_blk_p = _rope_tables((jnp.arange(nb_p) + 1) * SEL_BLOCK - 1)
    kct_p, vcp_p = _cmp_finish(raw_p[0], raw_p[1], kn0, seg, tabs_blk_p, place)
    e_mat = (jnp.arange(seq)[None, :] // SEL_BLOCK == jnp.arange(nb_p)[:, None]).astype(BF16)
    for l in range(n_a, depth):
        j = l - n_a
        q, gates = _q_proj(x, mod_p(l, 0), norm_mix[l][None], w_q_b[j], w_g_b[j], _tile_heads(q_norm[j], N_KV_HEADS),
                           seg, tabs_p, tm, sm * LOG2E, BF16)
        bias, o_cmp = _select(q, kct_p, vcp_p, pin, tq_sel, 1)
        o = _attention(q, gates, bias, o_cmp, kst, vs, kwt, vw, e_mat, pin, pout, tq, ck, ck_big)
        x = _ffn(x, mod_p(l, 1), norm_ffn[l][None], w_gu_b[l], w_dn_b[l], tm, tf, (o, mod_p(l, 0), w_o_b[j]))
    y_prompt = x
    kv_rows_prompt = rows_t.transpose(0, 4, 1, 2, 3)
    win_keep_p = min(WINDOW, seq)
    win_prompt = win_t[..., seq - win_keep_p:].transpose(0, 4, 1, 2, 3)
    pool_prompt = jnp.stack(pool_p)

    pos_s = past_len
    xs = x_sample
    pool_s = []
    for l in range(n_a):
        pre = jnp.pad(state_pool[l], ((0, 0), (POOL_HALO - state_pool.shape[2], 0), (0, 0)))
        xs, npool = _pool_layer(xs, pre, mod_s_seq(l, 0), norm_mix[l][None], pool_w_b[l], pool_scale[l][None], pos_s, 1)
        pool_s.append(npool[:, 1:])
        xs = _ffn(xs.reshape(1, b_s, d), mod_s_tok(l, 1), norm_ffn[l][None], w_gu_b[l], w_dn_b[l], b_s, tf).reshape(b_s, 1, d)
    xt = xs.reshape(1, b_s, d)
    tabs_s = _rope_tables(jnp.full((b_s,), pos_s))
    rows_s, win_s = _kv_proj(xt, mod_kv[b_p:m_all][None], norm_kv[None], w_kv_b, kn, seg, tabs_s, place, one_row,
                             b_s, False)
    rows_s = rows_s.reshape(b_s, 4, N_KV_HEADS, HEAD_DIM)
    win_s = win_s.reshape(b_s, 2, N_KV_HEADS, HEAD_DIM)

    cache_t = cache_kv.transpose(0, 2, 3, 4, 1)
    state_t = state_kv_win.transpose(0, 2, 3, 4, 1)
    raw_s = _compress(page_table, cache_t, pe_t, w1cat, w2bd, True)
    nb_s = -(-(past_len + 1) // SEL_BLOCK)
    z_tail = jnp.pad(rows_s[:, :2].transpose(1, 0, 2, 3).reshape(2 * b_s * N_KV_HEADS, HEAD_DIM),
                     ((0, 0), (0, (SEL_BLOCK - 1) * HEAD_DIM)))
    raw_tail = _compress_tail(z_tail, cmp_pe.reshape(2, 1, SEL_BLOCK * HEAD_DIM), cmp_w1.astype(BF16),
                              cmp_w2.astype(BF16)).reshape(2, b_s, 1, hq)
    nbp_s = -(-nb_s // LANES) * LANES
    raw_all = jnp.pad(jnp.concatenate([raw_s, raw_tail], axis=2), ((0, 0), (0, 0), (0, nbp_s - nb_s), (0, 0)))
    tabs_blk_s = _rope_tables((jnp.arange(nbp_s) + 1) * SEL_BLOCK - 1)
    kct_s, vcp_s = _cmp_finish(raw_all[0], raw_all[1], kn0, seg, tabs_blk_s, place)

    new_h = jnp.repeat(jnp.concatenate([rows_s[:, 2:4], win_s], axis=1), gq, axis=2)
    bb = math.gcd(b_s, 8)
    for l in range(n_a, depth):
        j = l - n_a
        q_s, gates_s = _q_proj(xt, mod_s_tok(l, 0), norm_mix[l][None], w_q_b[j], w_g_b[j],
                               _tile_heads(q_norm[j], N_KV_HEADS), seg, tabs_s, b_s, sm, F32)
        q3 = q_s.reshape(b_s, n_heads, HEAD_DIM)
        q_cmp = jnp.pad(q3, ((0, 0), (0, 0), (0, LANES - HEAD_DIM)))
        g3 = gates_s[0, :, :, :3 * gq].transpose(1, 0, 2).reshape(b_s, n_heads, 3)
        g3 = jnp.pad(g3, ((0, 0), (0, 0), (0, LANES - 3)))
        o_c, idx = _decode_a(q_cmp, kct_s, vcp_s, pos_s, nb_s, gq, bb)
        o_s = _decode_b(idx.reshape(b_s, N_KV_HEADS * N_SEL), page_table, q3, g3, o_c, new_h, cache_t, state_t,
                        pos_s, gq)
        xt = _ffn(xt, mod_s_tok(l, 1), norm_ffn[l][None], w_gu_b[l], w_dn_b[l], b_s, tf,
                  (o_s.reshape(1, b_s, d), mod_s_tok(l, 0), w_o_b[j]))
    y_sample = xt.reshape(b_s, 1, d)
    kv_rows_sample = rows_s.reshape(b_s, 1, 4, N_KV_HEADS, HEAD_DIM)
    win_sample = jnp.concatenate([state_t[..., 1:], win_s[..., None]], axis=-1).transpose(0, 4, 1, 2, 3)
    pool_sample = jnp.stack(pool_s)
    return (y_prompt, y_sample, kv_rows_prompt, kv_rows_sample, win_prompt, win_sample, pool_prompt, pool_sample)
```
